```python
import math
import jax
import jax.numpy as jnp
from jax import lax
import numpy as np

D_MODEL = 4096
BATCH = 4
SEQ = 2048
DEPTH = 4
DEC_BATCH = 32
DEC_SEQ = 1
PAST_LEN = 8192
PAGE_SIZE = 128

N_EVEN = (DEPTH + 1) // 2
N_ODD = DEPTH // 2
D_SSM = D_MODEL // 2
D_CONV = D_MODEL - D_SSM
SSM_GROUP = 16
SSM_GROUPS = D_SSM // SSM_GROUP
SSM_STATE = 64
SSM_CHUNK = 128
DT_MIN = 1e-3
DT_MAX = 1e-1
CONV_W = 3
HEAD_DIM = 128
N_HEADS = D_MODEL // HEAD_DIM
N_KV_HEADS = 8
KV_GROUP = N_HEADS // N_KV_HEADS
WINDOW = 128
ATT_BLOCK = WINDOW
D_FF = 11008
RMS_EPS = 1e-5
NEG_INF = -1e30

kernel_name = "hybrid_s5_shortconv_swa_sink_convffn_step"

F32 = jnp.float32


def rmsnorm(x, g):
    xf = x.astype(F32)
    y = xf * lax.rsqrt(jnp.mean(xf * xf, axis=-1, keepdims=True) + RMS_EPS)
    return (y * g.astype(F32)).astype(x.dtype)


def causal_dwconv(x, w, prev):
    xp = jnp.concatenate([prev.astype(x.dtype), x], axis=1)
    L = x.shape[1]
    y = sum(xp[:, k:k + L] * w[k] for k in range(CONV_W))
    return y, xp[:, xp.shape[1] - (CONV_W - 1):]


def s5_discretise(lam_re, lam_im, log_dt, b_re, b_im):
    lr, li = lam_re.astype(F32), lam_im.astype(F32)
    dt = jnp.exp(log_dt.astype(F32))[:, None]
    mag = jnp.exp(lr * dt)
    ar, ai = mag * jnp.cos(li * dt), mag * jnp.sin(li * dt)
    den = lr * lr + li * li
    cr = ((ar - 1.0) * lr + ai * li) / den
    ci = (ai * lr - (ar - 1.0) * li) / den
    br, bi = b_re.astype(F32), b_im.astype(F32)
    bbr = cr[..., None] * br - ci[..., None] * bi
    bbi = cr[..., None] * bi + ci[..., None] * br
    return ar, ai, bbr, bbi


def _linrec_combine(e1, e2):
    a1r, a1i, b1r, b1i = e1
    a2r, a2i, b2r, b2i = e2
    return (a2r * a1r - a2i * a1i,
            a2r * a1i + a2i * a1r,
            a2r * b1r - a2i * b1i + b2r,
            a2r * b1i + a2i * b1r + b2i)


def s5_scan(u, s_re, s_im, lam_re, lam_im, log_dt, b_re, b_im, c_re, c_im, d_skip):
    bsz, L, _ = u.shape
    T = math.gcd(L, SSM_CHUNK)
    nc = L // T
    ar, ai, bbr, bbi = s5_discretise(lam_re, lam_im, log_dt, b_re, b_im)
    cr, cim = c_re.astype(F32), c_im.astype(F32)
    uf = u.astype(F32)
    ub = uf.reshape(bsz, nc, T, SSM_GROUPS, SSM_GROUP).transpose(1, 0, 2, 3, 4)
    a_r = jnp.broadcast_to(ar, (bsz, T, SSM_GROUPS, SSM_STATE))
    a_i = jnp.broadcast_to(ai, (bsz, T, SSM_GROUPS, SSM_STATE))

    def block(carry, uc):
        hr0, hi0 = carry
        bur = jnp.einsum('btgi,gpi->btgp', uc, bbr)
        bui = jnp.einsum('btgi,gpi->btgp', uc, bbi)
        pa_r, pa_i, hb_r, hb_i = lax.associative_scan(_linrec_combine, (a_r, a_i, bur, bui), axis=1)
        hr = pa_r * hr0[:, None] - pa_i * hi0[:, None] + hb_r
        hi = pa_r * hi0[:, None] + pa_i * hr0[:, None] + hb_i
        y = jnp.einsum('btgp,gip->btgi', hr, cr) - jnp.einsum('btgp,gip->btgi', hi, cim)
        return (hr[:, -1], hi[:, -1]), y

    (hr, hi), ys = lax.scan(block, (s_re.astype(F32), s_im.astype(F32)), ub)
    y = ys.transpose(1, 0, 2, 3, 4).reshape(bsz, L, D_SSM) + d_skip.astype(F32) * uf
    return y.astype(u.dtype), hr, hi


def even_mixer(h, p, j, ssm_re, ssm_im, sconv):
    z = h @ p["w_in_even"][j]
    u, v, bg, cg = jnp.split(z, [D_SSM, D_SSM + D_CONV, D_SSM + 2 * D_CONV], axis=-1)
    y, new_re, new_im = s5_scan(u, ssm_re, ssm_im, p["ssm_lambda_re"][j], p["ssm_lambda_im"][j],
                                p["ssm_log_dt"][j], p["ssm_b_re"][j], p["ssm_b_im"][j],
                                p["ssm_c_re"][j], p["ssm_c_im"][j], p["ssm_d"][j])
    g = jax.nn.gelu(y)
    out_a = g * jax.nn.sigmoid(g @ p["w_glu"][j] + p["b_glu"][j])
    cv, new_sconv = causal_dwconv(cg * v, p["sconv_w"][j], sconv)
    out_b = bg * cv
    out = jnp.concatenate([out_a, out_b], axis=-1) @ p["w_out_even"][j]
    return out, new_re, new_im, new_sconv


def qkv_proj(h, w_qkv, b_qkv):
    bsz, L, _ = h.shape
    z = h @ w_qkv + b_qkv
    nq, nk = N_HEADS * HEAD_DIM, N_KV_HEADS * HEAD_DIM
    q = z[..., :nq].reshape(bsz, L, N_KV_HEADS, KV_GROUP, HEAD_DIM)
    k = z[..., nq:nq + nk].reshape(bsz, L, N_KV_HEADS, HEAD_DIM)
    v = z[..., nq + nk:].reshape(bsz, L, N_KV_HEADS, HEAD_DIM)
    return q, k, v


def sink_attention(q, k, v, qpos, kpos, sinks):
    s = jnp.einsum('bnqhgd,bnkhd->bnhgqk', q, k).astype(F32) * (HEAD_DIM ** -0.5)
    diff = qpos[:, :, None] - kpos[:, None, :]
    vis = (diff >= 0) & (diff <= WINDOW) & (kpos[:, None, :] >= 0)
    s = jnp.where(vis[None, :, None, None], s, NEG_INF)
    sk = sinks.astype(F32).reshape(N_KV_HEADS, KV_GROUP)[None, None, :, :, None, None]
    m = jnp.maximum(jnp.max(s, axis=-1, keepdims=True), sk)
    pr = jnp.exp(s - m)
    w = pr / (jnp.sum(pr, axis=-1, keepdims=True) + jnp.exp(sk - m))
    return jnp.einsum('bnhgqk,bnkhd->bnqhgd', w.astype(v.dtype), v)


def attn_prompt(h, p, j):
    bsz, L, _ = h.shape
    q, k, v = qkv_proj(h, p["w_qkv"][j], p["b_qkv"][j])
    nb = L // ATT_BLOCK
    qb = q.reshape(bsz, nb, ATT_BLOCK, N_KV_HEADS, KV_GROUP, HEAD_DIM)

    def band(t):
        cur = t.reshape(bsz, nb, ATT_BLOCK, N_KV_HEADS, HEAD_DIM)
        prev = jnp.pad(cur, ((0, 0), (1, 0), (0, 0), (0, 0), (0, 0)))[:, :nb]
        return jnp.concatenate([prev, cur], axis=2)

    start = jnp.arange(nb)[:, None] * ATT_BLOCK
    qpos = start + jnp.arange(ATT_BLOCK)[None]
    kpos = start - ATT_BLOCK + jnp.arange(2 * ATT_BLOCK)[None]
    o = sink_attention(qb, band(k), band(v), qpos, kpos, p["attn_sinks"][j])
    y = o.reshape(bsz, L, N_HEADS * HEAD_DIM) @ p["w_o"][j] + p["b_o"][j]
    keep = min(WINDOW, L)
    return y, k[:, L - keep:], v[:, L - keep:]


def attn_sample(h, p, j, k_buf, v_buf):
    bsz, L, _ = h.shape
    q, k, v = qkv_proj(h, p["w_qkv"][j], p["b_qkv"][j])
    wb = k_buf.shape[1]
    k_all = jnp.concatenate([k_buf.astype(k.dtype), k], axis=1)
    v_all = jnp.concatenate([v_buf.astype(v.dtype), v], axis=1)
    qpos = (wb + jnp.arange(L))[None]
    kpos = jnp.arange(wb + L)[None]
    o = sink_attention(q[:, None], k_all[:, None], v_all[:, None], qpos, kpos, p["attn_sinks"][j])
    y = o.reshape(bsz, L, N_HEADS * HEAD_DIM) @ p["w_o"][j] + p["b_o"][j]
    return y, k_all[:, L:], v_all[:, L:]


def conv_ffn(h, w_gate, w_up, conv_w, w_down, prev):
    g, new_prev = causal_dwconv(h @ w_gate, conv_w, prev)
    return (jax.nn.silu(g) * (h @ w_up)) @ w_down, new_prev


def trunk(x, ssm_re, ssm_im, sconv, kbuf, vbuf, fconv, p, is_prompt):
    n_re, n_im, n_sc, n_k, n_v, n_fc = [], [], [], [], [], []
    for l in range(DEPTH):
        j = l // 2
        h = rmsnorm(x, p["norm_mix_g"][l])
        if l % 2 == 0:
            out, r, i, c = even_mixer(h, p, j, ssm_re[j], ssm_im[j], sconv[j])
            n_re.append(r)
            n_im.append(i)
            n_sc.append(c)
        else:
            if is_prompt:
                out, kk, vv = attn_prompt(h, p, j)
            else:
                out, kk, vv = attn_sample(h, p, j, kbuf[j], vbuf[j])
            n_k.append(kk)
            n_v.append(vv)
        x = x + out
        f, fc = conv_ffn(rmsnorm(x, p["norm_ffn_g"][l]), p["w_ffn_gate"][l], p["w_ffn_up"][l],
                         p["ffn_conv_w"][l], p["w_ffn_down"][l], fconv[l])
        n_fc.append(fc)
        x = x + f
    return (rmsnorm(x, p["norm_final_g"]), jnp.stack(n_re), jnp.stack(n_im), jnp.stack(n_sc),
            jnp.stack(n_k), jnp.stack(n_v), jnp.stack(n_fc))


def setup_inputs(seed: int = 0) -> dict:
    key = jax.random.key(seed)
    ks = list(jax.random.split(key, 40))

    def nrm(i, shape, scale):
        return jax.random.normal(ks[i], shape, F32) * scale

    wb = min(WINDOW, PAST_LEN)
    n_arange = jnp.arange(SSM_STATE, dtype=F32)
    qkv_w = (N_HEADS + 2 * N_KV_HEADS) * HEAD_DIM
    return {
        "x_prompt": nrm(0, (BATCH, SEQ, D_MODEL), 1.0),
        "x_sample": nrm(1, (DEC_BATCH, DEC_SEQ, D_MODEL), 1.0),
        "state_ssm_re": nrm(2, (N_EVEN, DEC_BATCH, SSM_GROUPS, SSM_STATE), 0.5),
        "state_ssm_im": nrm(3, (N_EVEN, DEC_BATCH, SSM_GROUPS, SSM_STATE), 0.5),
        "state_sconv": nrm(4, (N_EVEN, DEC_BATCH, CONV_W - 1, D_CONV), 1.0),
        "cache_k": nrm(5, (N_ODD, DEC_BATCH, wb, N_KV_HEADS, HEAD_DIM), 1.0),
        "cache_v": nrm(6, (N_ODD, DEC_BATCH, wb, N_KV_HEADS, HEAD_DIM), 1.0),
        "state_ffn_conv": nrm(7, (DEPTH, DEC_BATCH, CONV_W - 1, D_FF), 1.0),
        "norm_mix_g": 1.0 + nrm(8, (DEPTH, D_MODEL), 0.02),
        "norm_ffn_g": 1.0 + nrm(9, (DEPTH, D_MODEL), 0.02),
        "norm_final_g": 1.0 + nrm(10, (D_MODEL,), 0.02),
        "w_in_even": nrm(11, (N_EVEN, D_MODEL, D_SSM + 3 * D_CONV), D_MODEL ** -0.5),
        "ssm_lambda_re": -0.5 + nrm(12, (N_EVEN, SSM_GROUPS, SSM_STATE), 0.01),
        "ssm_lambda_im": jnp.pi * n_arange + nrm(13, (N_EVEN, SSM_GROUPS, SSM_STATE), 0.01),
        "ssm_log_dt": jax.random.uniform(ks[14], (N_EVEN, SSM_GROUPS), F32,
                                         minval=math.log(DT_MIN), maxval=math.log(DT_MAX)),
        "ssm_b_re": nrm(15, (N_EVEN, SSM_GROUPS, SSM_STATE, SSM_GROUP), (2 * SSM_GROUP) ** -0.5),
        "ssm_b_im": nrm(16, (N_EVEN, SSM_GROUPS, SSM_STATE, SSM_GROUP), (2 * SSM_GROUP) ** -0.5),
        "ssm_c_re": nrm(17, (N_EVEN, SSM_GROUPS, SSM_GROUP, SSM_STATE), (2 * SSM_STATE) ** -0.5),
        "ssm_c_im": nrm(18, (N_EVEN, SSM_GROUPS, SSM_GROUP, SSM_STATE), (2 * SSM_STATE) ** -0.5),
        "ssm_d": nrm(19, (N_EVEN, D_SSM), 1.0),
        "w_glu": nrm(20, (N_EVEN, D_SSM, D_SSM), D_SSM ** -0.5),
        "b_glu": nrm(21, (N_EVEN, D_SSM), 0.01),
        "sconv_w": nrm(22, (N_EVEN, CONV_W, D_CONV), CONV_W ** -0.5),
        "w_out_even": nrm(23, (N_EVEN, D_SSM + D_CONV, D_MODEL), D_MODEL ** -0.5),
        "w_qkv": nrm(24, (N_ODD, D_MODEL, qkv_w), D_MODEL ** -0.5),
        "b_qkv": nrm(25, (N_ODD, qkv_w), 0.01),
        "attn_sinks": nrm(26, (N_ODD, N_HEADS), 0.5),
        "w_o": nrm(27, (N_ODD, N_HEADS * HEAD_DIM, D_MODEL), (N_HEADS * HEAD_DIM) ** -0.5),
        "b_o": nrm(28, (N_ODD, D_MODEL), 0.01),
        "w_ffn_gate": nrm(29, (DEPTH, D_MODEL, D_FF), D_MODEL ** -0.5),
        "w_ffn_up": nrm(30, (DEPTH, D_MODEL, D_FF), D_MODEL ** -0.5),
        "ffn_conv_w": nrm(31, (DEPTH, CONV_W, D_FF), CONV_W ** -0.5),
        "w_ffn_down": nrm(32, (DEPTH, D_FF, D_MODEL), D_FF ** -0.5),
    }


def reference(x_prompt, x_sample, state_ssm_re, state_ssm_im, state_sconv, cache_k, cache_v,
              state_ffn_conv, norm_mix_g, norm_ffn_g, norm_final_g, w_in_even, ssm_lambda_re,
              ssm_lambda_im, ssm_log_dt, ssm_b_re, ssm_b_im, ssm_c_re, ssm_c_im, ssm_d, w_glu,
              b_glu, sconv_w, w_out_even, w_qkv, b_qkv, attn_sinks, w_o, b_o, w_ffn_gate,
              w_ffn_up, ffn_conv_w, w_ffn_down):
    p = dict(norm_mix_g=norm_mix_g, norm_ffn_g=norm_ffn_g, norm_final_g=norm_final_g,
             w_in_even=w_in_even, ssm_lambda_re=ssm_lambda_re, ssm_lambda_im=ssm_lambda_im,
             ssm_log_dt=ssm_log_dt, ssm_b_re=ssm_b_re, ssm_b_im=ssm_b_im, ssm_c_re=ssm_c_re,
             ssm_c_im=ssm_c_im, ssm_d=ssm_d, w_glu=w_glu, b_glu=b_glu, sconv_w=sconv_w,
             w_out_even=w_out_even, w_qkv=w_qkv, b_qkv=b_qkv, attn_sinks=attn_sinks, w_o=w_o,
             b_o=b_o, w_ffn_gate=w_ffn_gate, w_ffn_up=w_ffn_up, ffn_conv_w=ffn_conv_w,
             w_ffn_down=w_ffn_down)
    bp = x_prompt.shape[0]
    z_re = jnp.zeros((N_EVEN, bp, SSM_GROUPS, SSM_STATE), F32)
    z_sc = jnp.zeros((N_EVEN, bp, CONV_W - 1, D_CONV), x_prompt.dtype)
    z_fc = jnp.zeros((DEPTH, bp, CONV_W - 1, D_FF), x_prompt.dtype)
    y_prompt, p_re, p_im, p_sc, p_k, p_v, p_fc = trunk(
        x_prompt, z_re, z_re, z_sc, None, None, z_fc, p, True)
    y_sample, s_re, s_im, s_sc, s_k, s_v, s_fc = trunk(
        x_sample, state_ssm_re, state_ssm_im, state_sconv, cache_k, cache_v, state_ffn_conv, p, False)
    return (y_prompt, y_sample, p_re, p_im, p_sc, p_k, p_v, p_fc, s_re, s_im, s_sc, s_k, s_v, s_fc)
```

```python
import functools

import jax
import jax.numpy as jnp
from jax import lax
from jax.experimental import pallas as pl
from jax.experimental.pallas import tpu as pltpu

F32 = jnp.float32
BF16 = jnp.bfloat16

DEPTH = 4
SSM_GROUP = 16
SSM_STATE = 64
GROUPS_PER_BLOCK = 8
CONV_W = 3
HEAD_DIM = 128
N_KV_HEADS = 8
KV_GROUP = 4
WINDOW = 128
RMS_EPS = 1e-5
NEG_INF = -1e30
D_FF_PAD = 11264

VMEM_LIMIT_BYTES = 56 * 1024 * 1024


def _params(*sem):
    return pltpu.CompilerParams(dimension_semantics=sem, vmem_limit_bytes=VMEM_LIMIT_BYTES)


def _rmsnorm_kernel(x_ref, g_ref, o_ref):
    x = x_ref[...]
    y = x * lax.rsqrt(jnp.mean(x * x, axis=-1, keepdims=True) + RMS_EPS)
    o_ref[...] = (y * g_ref[...]).astype(o_ref.dtype)


def rmsnorm(x, g, out_dtype):
    m, d = x.shape
    tm = min(m, 256)
    return pl.pallas_call(
        _rmsnorm_kernel,
        grid=(m // tm,),
        in_specs=[pl.BlockSpec((tm, d), lambda i: (i, 0)), pl.BlockSpec((1, d), lambda i: (0, 0))],
        out_specs=pl.BlockSpec((tm, d), lambda i: (i, 0)),
        out_shape=jax.ShapeDtypeStruct((m, d), out_dtype),
        compiler_params=_params("parallel"),
        name="rmsnorm",
    )(x, g.reshape(1, d))


def _mm_kernel(*refs, nk, has_bias, mode):
    it = iter(refs)
    x_ref, w_ref = next(it), next(it)
    b_ref = next(it) if has_bias else None
    r_ref = next(it) if mode != "plain" else None
    o_ref = next(it)
    acc_ref = next(it) if nk > 1 else None

    def finish(acc):
        if has_bias:
            acc = acc + b_ref[...]
        if mode == "res":
            acc = r_ref[...] + acc
        elif mode == "glu":
            acc = r_ref[...] * jax.nn.sigmoid(acc)
        o_ref[...] = acc.astype(o_ref.dtype)

    part = jnp.dot(x_ref[...].astype(BF16), w_ref[...], preferred_element_type=F32)
    if nk == 1:
        finish(part)
    else:
        k = pl.program_id(2)

        @pl.when(k == 0)
        def _():
            acc_ref[...] = part

        @pl.when(k > 0)
        def _():
            acc_ref[...] += part

        @pl.when(k == nk - 1)
        def _():
            finish(acc_ref[...])


def matmul(x, w, *, bias=None, res=None, mode="plain", out_dtype=F32, tm=1024, tn=1024, tk=None):
    m, kdim = x.shape
    n = w.shape[1]
    tm, tn = min(tm, m), min(tn, n)
    tk = kdim if tk is None else tk
    nk = kdim // tk
    grid = (m // tm, n // tn, nk)
    in_specs = [pl.BlockSpec((tm, tk), lambda i, j, k: (i, k)), pl.BlockSpec((tk, tn), lambda i, j, k: (k, j))]
    args = [x, w]
    if bias is not None:
        in_specs.append(pl.BlockSpec((1, tn), lambda i, j, k: (0, j)))
        args.append(bias.reshape(1, n))
    if mode != "plain":
        in_specs.append(pl.BlockSpec((tm, tn), lambda i, j, k: (i, j)))
        args.append(res)
    return pl.pallas_call(
        functools.partial(_mm_kernel, nk=nk, has_bias=bias is not None, mode=mode),
        grid=grid,
        in_specs=in_specs,
        out_specs=pl.BlockSpec((tm, tn), lambda i, j, k: (i, j)),
        out_shape=jax.ShapeDtypeStruct((m, n), out_dtype),
        scratch_shapes=[pltpu.VMEM((tm, tn), F32)] if nk > 1 else [],
        compiler_params=_params("parallel", "parallel", "arbitrary"),
        name="matmul_" + mode,
    )(*args)


def _conv_seq_kernel(*refs, mul_in, act, has_prev, tt):
    it = iter(refs)
    p_ref = next(it)
    q_ref = next(it) if mul_in else None
    r_ref, w_ref = next(it), next(it)
    prev_ref = next(it) if has_prev else None
    o_ref, st_ref, carry_ref = next(it), next(it), next(it)
    t = pl.program_id(2)
    m = p_ref[...] * q_ref[...] if mul_in else p_ref[...]

    @pl.when(t == 0)
    def _():
        if has_prev:
            carry_ref[6:8, :] = prev_ref[...]
        else:
            carry_ref[...] = jnp.zeros_like(carry_ref)

    c1 = carry_ref[7:8, :]
    c2 = carry_ref[6:7, :]
    row = lax.broadcasted_iota(jnp.int32, m.shape, 0)
    m1 = jnp.where(row == 0, c1, pltpu.roll(m, 1, axis=0))
    m2 = jnp.where(row == 0, c2, jnp.where(row == 1, c1, pltpu.roll(m, 2, axis=0)))
    y = w_ref[0:1, :] * m2 + w_ref[1:2, :] * m1 + w_ref[2:3, :] * m
    if act == "silu":
        y = jax.nn.silu(y)
    o_ref[...] = (r_ref[...] * y).astype(o_ref.dtype)
    last = m[tt - 8:tt, :]
    carry_ref[...] = last
    st_ref[...] = last


def conv_seq(p, q, r, w, *, batch, col0, ncols, act, out_dtype, tt=512, tc=512):
    del col0
    mul_in = q is not None
    m_rows = p[0].shape[0]
    nt = m_rows // batch // tt
    nc = ncols // tc

    def spec(c0):
        b0 = c0 // tc
        return pl.BlockSpec((tt, tc), lambda j, b, t: (b * nt + t, b0 + j))

    in_specs, args = [spec(p[1])], [p[0]]
    if mul_in:
        in_specs.append(spec(q[1]))
        args.append(q[0])
    in_specs += [spec(r[1]), pl.BlockSpec((CONV_W, tc), lambda j, b, t: (0, j))]
    args += [r[0], w]
    return pl.pallas_call(
        functools.partial(_conv_seq_kernel, mul_in=mul_in, act=act, has_prev=False, tt=tt),
        grid=(nc, batch, nt),
        in_specs=in_specs,
        out_specs=[pl.BlockSpec((tt, tc), lambda j, b, t: (b * nt + t, j)),
                   pl.BlockSpec((None, 8, tc), lambda j, b, t: (b, 0, j))],
        out_shape=[jax.ShapeDtypeStruct((m_rows, ncols), out_dtype),
                   jax.ShapeDtypeStruct((batch, 8, ncols), F32)],
        scratch_shapes=[pltpu.VMEM((8, tc), F32)],
        compiler_params=_params("parallel", "parallel", "arbitrary"),
        name="conv_seq_" + act,
    )(*args)


def _conv_step_kernel(*refs, mul_in, act):
    it = iter(refs)
    p_ref = next(it)
    q_ref = next(it) if mul_in else None
    r_ref, w_ref, prev_ref, o_ref, st_ref = next(it), next(it), next(it), next(it), next(it)
    m = p_ref[...] * q_ref[...] if mul_in else p_ref[...]
    prev = prev_ref[...]
    x2, x1 = prev[:, 0, :], prev[:, 1, :]
    y = w_ref[0:1, :] * x2 + w_ref[1:2, :] * x1 + w_ref[2:3, :] * m
    if act == "silu":
        y = jax.nn.silu(y)
    o_ref[...] = (r_ref[...] * y).astype(o_ref.dtype)
    st_ref[:, 0, :] = x1
    st_ref[:, 1, :] = m


def conv_step(p, q, r, w, prev, *, ncols, act, out_dtype, tc=512):
    mul_in = q is not None
    rows = p[0].shape[0]
    nc = ncols // tc

    def spec(c0):
        b0 = c0 // tc
        return pl.BlockSpec((rows, tc), lambda j: (0, b0 + j))

    in_specs, args = [spec(p[1])], [p[0]]
    if mul_in:
        in_specs.append(spec(q[1]))
        args.append(q[0])
    in_specs += [spec(r[1]), pl.BlockSpec((CONV_W, tc), lambda j: (0, j)),
                 pl.BlockSpec((rows, CONV_W - 1, tc), lambda j: (0, 0, j))]
    args += [r[0], w, prev]
    return pl.pallas_call(
        functools.partial(_conv_step_kernel, mul_in=mul_in, act=act),
        grid=(nc,),
        in_specs=in_specs,
        out_specs=[pl.BlockSpec((rows, tc), lambda j: (0, j)),
                   pl.BlockSpec((rows, CONV_W - 1, tc), lambda j: (0, 0, j))],
        out_shape=[jax.ShapeDtypeStruct((rows, ncols), out_dtype),
                   jax.ShapeDtypeStruct((rows, CONV_W - 1, ncols), F32)],
        compiler_params=_params("parallel"),
        name="conv_step_" + act,
    )(*args)


def _s5_prep_kernel(lr_ref, li_ref, ldt_ref, br_ref, bi_ref, pow_ref, bb_ref):
    lr, li = lr_ref[...], li_ref[...]
    dt = jnp.exp(ldt_ref[...])
    mag = jnp.exp(lr * dt)
    ar, ai = mag * jnp.cos(li * dt), mag * jnp.sin(li * dt)
    den = lr * lr + li * li
    cr = ((ar - 1.0) * lr + ai * li) / den
    ci = (ai * lr - (ar - 1.0) * li) / den
    br, bi = br_ref[...], bi_ref[...]
    bb_ref[:, 0:SSM_GROUP, :] = cr * br - ci * bi
    bb_ref[:, SSM_GROUP:2 * SSM_GROUP, :] = cr * bi + ci * br
    pr, pi = ar, ai
    for k in range(8):
        pow_ref[:, k:k + 1, :] = pr
        pow_ref[:, 8 + k:9 + k, :] = pi
        pr, pi = pr * ar - pi * ai, pr * ai + pi * ar


def s5_prepare(lam_re, lam_im, log_dt, b_re, b_im, c_re, c_im, d_skip):
    g, p = lam_re.shape
    i = SSM_GROUP
    nb, gb = g // GROUPS_PER_BLOCK, GROUPS_PER_BLOCK
    pw, bb = pl.pallas_call(
        _s5_prep_kernel,
        out_shape=[jax.ShapeDtypeStruct((g, 16, p), F32), jax.ShapeDtypeStruct((g, 2 * i, p), F32)],
        name="s5_prep",
    )(lam_re.reshape(g, 1, p), lam_im.reshape(g, 1, p), log_dt.reshape(g, 1, 1),
      b_re.transpose(0, 2, 1), b_im.transpose(0, 2, 1))

    def lanes(a):
        return a.reshape(nb, gb, 8, p).transpose(0, 2, 1, 3).reshape(nb, 8, gb * p)

    pr, pi = lanes(pw[:, 0:8]), lanes(pw[:, 8:16])
    rows = jnp.arange(8)[None, :, None]

    def masked(a, k):
        return jnp.where(rows >= k, a[:, k - 1:k, :], 0.0)

    def bcast(a, k):
        return jnp.broadcast_to(a[:, k - 1:k, :], a.shape)

    tab = jnp.concatenate([masked(pr, 1), masked(pi, 1), masked(pr, 2), masked(pi, 2),
                           masked(pr, 4), masked(pi, 4), pr, pi, bcast(pr, 8), bcast(pi, 8)], axis=1)
    eye = jnp.eye(gb, dtype=F32)

    def wb_part(a):
        a = a.reshape(nb, gb, i, p)
        return jnp.einsum('jaip,ab->jaibp', a, eye).reshape(nb, gb * i, gb * p)

    def wc_part(a):
        a = a.reshape(nb, gb, i, p)
        return jnp.einsum('jaip,ab->jbpai', a, eye).reshape(nb, gb * p, gb * i)

    wb = jnp.concatenate([wb_part(bb[:, :i]), wb_part(bb[:, i:])], axis=2).astype(BF16)
    wc = jnp.concatenate([wc_part(c_re.astype(F32)), wc_part(-c_im.astype(F32))], axis=1).astype(BF16)
    return tab, wb, wc, d_skip.reshape(nb, 1, gb * i)


_T_M1, _T_M2, _T_M4, _T_P, _T_A8 = 0, 16, 32, 48, 64


def _s5_seq_kernel(u_ref, wb_ref, wc_ref, tab_ref, d_ref, g_ref, st_ref, h_ref, carry_ref, *, tt):
    half = h_ref.shape[1] // 2
    t = pl.program_id(2)
    u = u_ref[...]
    h_ref[...] = jnp.dot(u.astype(BF16), wb_ref[...], preferred_element_type=F32)

    @pl.when(t == 0)
    def _():
        carry_ref[...] = jnp.zeros_like(carry_ref)

    def tab(off):
        return tab_ref[off:off + 8, :], tab_ref[off + 8:off + 16, :]

    def block(i, carry):
        cr, ci = carry
        r0 = pl.multiple_of(i * 8, 8)
        xr = h_ref[pl.ds(r0, 8), 0:half]
        xi = h_ref[pl.ds(r0, 8), half:2 * half]
        for k, off in ((1, _T_M1), (2, _T_M2), (4, _T_M4)):
            mr, mi = tab(off)
            sr, si = pltpu.roll(xr, k, axis=0), pltpu.roll(xi, k, axis=0)
            xr, xi = xr + (mr * sr - mi * si), xi + (mr * si + mi * sr)
        pr, pi = tab(_T_P)
        h_ref[pl.ds(r0, 8), 0:half] = xr + (pr * cr - pi * ci)
        h_ref[pl.ds(r0, 8), half:2 * half] = xi + (pr * ci + pi * cr)
        ar, ai = tab(_T_A8)
        lr = jnp.broadcast_to(xr[7:8, :], xr.shape)
        li = jnp.broadcast_to(xi[7:8, :], xi.shape)
        return lr + (ar * cr - ai * ci), li + (ar * ci + ai * cr)

    cr, ci = lax.fori_loop(0, tt // 8, block, (carry_ref[:, 0:half], carry_ref[:, half:2 * half]), unroll=2)
    carry_ref[:, 0:half] = cr
    carry_ref[:, half:2 * half] = ci
    st_ref[:, 0:half] = cr
    st_ref[:, half:2 * half] = ci
    y = jnp.dot(h_ref[...].astype(BF16), wc_ref[...], preferred_element_type=F32) + d_ref[...] * u
    g_ref[...] = jax.nn.gelu(y)


def s5_seq(z, prep, *, batch, tt=512):
    tab, wb, wc, d = prep
    nb, cin, cst = wb.shape
    m_rows = z.shape[0]
    nt = m_rows // batch // tt
    return pl.pallas_call(
        functools.partial(_s5_seq_kernel, tt=tt),
        grid=(nb, batch, nt),
        in_specs=[pl.BlockSpec((tt, cin), lambda j, b, t: (b * nt + t, j)),
                  pl.BlockSpec((None, cin, cst), lambda j, b, t: (j, 0, 0)),
                  pl.BlockSpec((None, cst, cin), lambda j, b, t: (j, 0, 0)),
                  pl.BlockSpec((None, tab.shape[1], cst // 2), lambda j, b, t: (j, 0, 0)),
                  pl.BlockSpec((None, 1, cin), lambda j, b, t: (j, 0, 0))],
        out_specs=[pl.BlockSpec((tt, cin), lambda j, b, t: (b * nt + t, j)),
                   pl.BlockSpec((None, None, 8, cst), lambda j, b, t: (j, b, 0, 0))],
        out_shape=[jax.ShapeDtypeStruct((m_rows, nb * cin), F32),
                   jax.ShapeDtypeStruct((nb, batch, 8, cst), F32)],
        scratch_shapes=[pltpu.VMEM((tt, cst), F32), pltpu.VMEM((8, cst), F32)],
        compiler_params=_params("parallel", "parallel", "arbitrary"),
        name="s5_seq",
    )(z, wb, wc, tab, d)


def _s5_step_kernel(u_ref, s_ref, wb_ref, wc_ref, tab_ref, d_ref, g_ref, st_ref):
    half = s_ref.shape[1] // 2
    u = u_ref[...]
    bu = jnp.dot(u.astype(BF16), wb_ref[...], preferred_element_type=F32)
    ar, ai = tab_ref[_T_P:_T_P + 1, :], tab_ref[_T_P + 8:_T_P + 9, :]
    sr, si = s_ref[:, 0:half], s_ref[:, half:2 * half]
    hr = ar * sr - ai * si + bu[:, 0:half]
    hi = ar * si + ai * sr + bu[:, half:2 * half]
    st_ref[:, 0:half] = hr
    st_ref[:, half:2 * half] = hi
    h = jnp.concatenate([hr, hi], axis=1)
    y = jnp.dot(h.astype(BF16), wc_ref[...], preferred_element_type=F32) + d_ref[...] * u
    g_ref[...] = jax.nn.gelu(y)


def s5_step(z, state, prep):
    tab, wb, wc, d = prep
    nb, cin, cst = wb.shape
    rows = z.shape[0]
    return pl.pallas_call(
        _s5_step_kernel,
        grid=(nb,),
        in_specs=[pl.BlockSpec((rows, cin), lambda j: (0, j)),
                  pl.BlockSpec((None, rows, cst), lambda j: (j, 0, 0)),
                  pl.BlockSpec((None, cin, cst), lambda j: (j, 0, 0)),
                  pl.BlockSpec((None, cst, cin), lambda j: (j, 0, 0)),
                  pl.BlockSpec((None, tab.shape[1], cst // 2), lambda j: (j, 0, 0)),
                  pl.BlockSpec((None, 1, cin), lambda j: (j, 0, 0))],
        out_specs=[pl.BlockSpec((rows, cin), lambda j: (0, j)),
                   pl.BlockSpec((None, rows, cst), lambda j: (j, 0, 0))],
        out_shape=[jax.ShapeDtypeStruct((rows, nb * cin), F32),
                   jax.ShapeDtypeStruct((nb, rows, cst), F32)],
        compiler_params=_params("parallel"),
        name="s5_step",
    )(z, state, wb, wc, tab, d)


def _state_to_blocks(s_re, s_im):
    b, g, p = s_re.shape
    nb = g // GROUPS_PER_BLOCK

    def f(s):
        return s.reshape(b, nb, GROUPS_PER_BLOCK * p).transpose(1, 0, 2)

    return jnp.concatenate([f(s_re), f(s_im)], axis=2)


def _blocks_to_state(st, p=SSM_STATE):
    nb, b, c = st.shape
    half = c // 2

    def f(s):
        return s.transpose(1, 0, 2).reshape(b, nb * half // p, p)

    return f(st[:, :, :half]), f(st[:, :, half:])


def _attn_seq_kernel(q_ref, kp_ref, kc_ref, vp_ref, vc_ref, sk_ref, o_ref):
    hd, blk = HEAD_DIM, WINDOW
    n = pl.program_id(1)
    q = q_ref[...]
    qa = jnp.concatenate([q[:, g * hd:(g + 1) * hd] for g in range(KV_GROUP)], axis=0).astype(BF16)
    kb = jnp.concatenate([kp_ref[...], kc_ref[...]], axis=0).astype(BF16)
    vb = jnp.concatenate([vp_ref[...], vc_ref[...]], axis=0).astype(BF16)
    s = lax.dot_general(qa, kb, (((1,), (1,)), ((), ())), preferred_element_type=F32) * (hd ** -0.5)
    qi = lax.broadcasted_iota(jnp.int32, s.shape, 0) % blk
    kj = lax.broadcasted_iota(jnp.int32, s.shape, 1)
    vis = (kj >= qi) & (kj <= qi + WINDOW) & ((kj >= blk) | (n > 0))
    s = jnp.where(vis, s, NEG_INF)
    sk = sk_ref[...]
    m = jnp.maximum(jnp.max(s, axis=-1, keepdims=True), sk)
    p = jnp.exp(s - m)
    w = p / (jnp.sum(p, axis=-1, keepdims=True) + jnp.exp(sk - m))
    o = jnp.dot(w.astype(BF16), vb, preferred_element_type=F32)
    o_ref[...] = jnp.concatenate([o[g * blk:(g + 1) * blk] for g in range(KV_GROUP)], axis=1).astype(o_ref.dtype)


def attn_seq(qkv, sinks, *, batch, out_dtype):
    m_rows = qkv.shape[0]
    nblk = m_rows // batch // WINDOW
    nq = N_KV_HEADS * KV_GROUP
    k0, v0 = nq, nq + N_KV_HEADS
    sk = jnp.repeat(sinks.astype(F32).reshape(N_KV_HEADS, KV_GROUP), WINDOW, axis=1)[..., None]

    def cur(c0):
        return pl.BlockSpec((WINDOW, HEAD_DIM), lambda b, n, h: (b * nblk + n, c0 + h))

    def prev(c0):
        return pl.BlockSpec((WINDOW, HEAD_DIM), lambda b, n, h: (b * nblk + jnp.maximum(n - 1, 0), c0 + h))

    return pl.pallas_call(
        _attn_seq_kernel,
        grid=(batch, nblk, N_KV_HEADS),
        in_specs=[pl.BlockSpec((WINDOW, KV_GROUP * HEAD_DIM), lambda b, n, h: (b * nblk + n, h)),
                  prev(k0), cur(k0), prev(v0), cur(v0),
                  pl.BlockSpec((None, KV_GROUP * WINDOW, 1), lambda b, n, h: (h, 0, 0))],
        out_specs=pl.BlockSpec((WINDOW, KV_GROUP * HEAD_DIM), lambda b, n, h: (b * nblk + n, h)),
        out_shape=jax.ShapeDtypeStruct((m_rows, nq * HEAD_DIM), out_dtype),
        compiler_params=_params("parallel", "parallel", "parallel"),
        name="attn_seq",
    )(qkv, qkv, qkv, qkv, qkv, sk)


def _attn_step_kernel(q_ref, kn_ref, vn_ref, kc_ref, vc_ref, sk_ref, o_ref):
    hd = HEAD_DIM
    scale = hd ** -0.5
    for h in range(N_KV_HEADS):
        rows = slice(h * KV_GROUP, (h + 1) * KV_GROUP)
        qh = q_ref[rows, :].astype(BF16)
        kh = kc_ref[:, h * hd:(h + 1) * hd].astype(BF16)
        vh = vc_ref[:, h * hd:(h + 1) * hd].astype(BF16)
        kn = kn_ref[h:h + 1, :].astype(BF16).astype(F32)
        vn = vn_ref[h:h + 1, :].astype(BF16).astype(F32)
        sc = lax.dot_general(qh, kh, (((1,), (1,)), ((), ())), preferred_element_type=F32) * scale
        sn = jnp.sum(qh.astype(F32) * kn, axis=-1, keepdims=True) * scale
        sk = sk_ref[rows, :]
        m = jnp.maximum(jnp.maximum(jnp.max(sc, axis=-1, keepdims=True), sn), sk)
        pc, pn = jnp.exp(sc - m), jnp.exp(sn - m)
        den = jnp.sum(pc, axis=-1, keepdims=True) + pn + jnp.exp(sk - m)
        o = jnp.dot((pc / den).astype(BF16), vh, preferred_element_type=F32)
        o = o + (pn / den).astype(BF16).astype(F32) * vn
        o_ref[rows, :] = o.astype(o_ref.dtype)


def attn_step(qkv, k_buf, v_buf, sinks, *, out_dtype):
    rows = qkv.shape[0]
    nq = N_KV_HEADS * KV_GROUP
    wb = k_buf.shape[1]
    q3 = qkv[:, :nq * HEAD_DIM].reshape(rows, nq, HEAD_DIM)
    kn = qkv[:, nq * HEAD_DIM:(nq + N_KV_HEADS) * HEAD_DIM].reshape(rows, N_KV_HEADS, HEAD_DIM)
    vn = qkv[:, (nq + N_KV_HEADS) * HEAD_DIM:].reshape(rows, N_KV_HEADS, HEAD_DIM)
    kvw = N_KV_HEADS * HEAD_DIM
    o = pl.pallas_call(
        _attn_step_kernel,
        grid=(rows,),
        in_specs=[pl.BlockSpec((None, nq, HEAD_DIM), lambda b: (b, 0, 0)),
                  pl.BlockSpec((None, N_KV_HEADS, HEAD_DIM), lambda b: (b, 0, 0)),
                  pl.BlockSpec((None, N_KV_HEADS, HEAD_DIM), lambda b: (b, 0, 0)),
                  pl.BlockSpec((None, wb, kvw), lambda b: (b, 0, 0)),
                  pl.BlockSpec((None, wb, kvw), lambda b: (b, 0, 0)),
                  pl.BlockSpec((nq, 1), lambda b: (0, 0))],
        out_specs=pl.BlockSpec((None, nq, HEAD_DIM), lambda b: (b, 0, 0)),
        out_shape=jax.ShapeDtypeStruct((rows, nq, HEAD_DIM), out_dtype),
        compiler_params=_params("parallel"),
        name="attn_step",
    )(q3, kn, vn, k_buf.reshape(rows, wb, kvw), v_buf.reshape(rows, wb, kvw),
      sinks.astype(F32).reshape(nq, 1))
    new_k = jnp.concatenate([k_buf, kn[:, None]], axis=1)[:, 1:]
    new_v = jnp.concatenate([v_buf, vn[:, None]], axis=1)[:, 1:]
    return o.reshape(rows, nq * HEAD_DIM), new_k, new_v


def _trunk(x, wts, *, batch, state):
    seq = state is None
    m_rows, d_model = x.shape
    d_ssm = wts["ssm_d"].shape[1]
    d_conv = wts["sconv_w"].shape[2]
    d_ff = wts["d_ff"]
    tm = min(m_rows, 1024)
    n_re, n_im, n_sc, n_k, n_v, n_fc = [], [], [], [], [], []
    for l in range(DEPTH):
        j = l // 2
        h = rmsnorm(x, wts["norm_mix_g"][l], BF16)
        if l % 2 == 0:
            z = matmul(h, wts["w_in_even"][j], tm=tm)
            cv, cb, cc = d_ssm, d_ssm + d_conv, d_ssm + 2 * d_conv
            if seq:
                g, st = s5_seq(z, wts["s5"][j], batch=batch)
                st = st[:, :, 0, :]
                out_b, sc = conv_seq((z, cc), (z, cv), (z, cb), wts["sconv_w"][j], batch=batch, col0=0,
                                     ncols=d_conv, act="none", out_dtype=BF16)
                sc = sc[:, 8 - (CONV_W - 1):, :]
            else:
                g, st = s5_step(z, _state_to_blocks(state[0][j], state[1][j]), wts["s5"][j])
                out_b, sc = conv_step((z, cc), (z, cv), (z, cb), wts["sconv_w"][j], state[2][j],
                                      ncols=d_conv, act="none", out_dtype=BF16)
            re, im = _blocks_to_state(st)
            n_re.append(re)
            n_im.append(im)
            n_sc.append(sc)
            out_a = matmul(g, wts["w_glu"][j], bias=wts["b_glu"][j], res=g, mode="glu", out_dtype=BF16, tm=tm)
            x = matmul(jnp.concatenate([out_a, out_b], axis=1), wts["w_out_even"][j], res=x, mode="res", tm=tm)
        else:
            qkv = matmul(h, wts["w_qkv"][j], bias=wts["b_qkv"][j], tm=tm)
            nq = N_KV_HEADS * KV_GROUP * HEAD_DIM
            nkv = N_KV_HEADS * HEAD_DIM
            if seq:
                o = attn_seq(qkv, wts["attn_sinks"][j], batch=batch, out_dtype=BF16)
                length = m_rows // batch
                keep = min(WINDOW, length)
                kv = qkv.reshape(batch, length, -1)[:, length - keep:, nq:]
                kk = kv[..., :nkv].reshape(batch, keep, N_KV_HEADS, HEAD_DIM)
                vv = kv[..., nkv:].reshape(batch, keep, N_KV_HEADS, HEAD_DIM)
            else:
                o, kk, vv = attn_step(qkv, state[3][j], state[4][j], wts["attn_sinks"][j], out_dtype=BF16)
            n_k.append(kk)
            n_v.append(vv)
            x = matmul(o, wts["w_o"][j], bias=wts["b_o"][j], res=x, mode="res", tm=tm)
        h = rmsnorm(x, wts["norm_ffn_g"][l], BF16)
        gu = matmul(h, wts["w_ffn_gu"][l], tm=tm)
        if seq:
            a, fc = conv_seq((gu, 0), None, (gu, D_FF_PAD), wts["ffn_conv_w"][l], batch=batch, col0=0,
                             ncols=D_FF_PAD, act="silu", out_dtype=BF16)
            fc = fc[:, 8 - (CONV_W - 1):, :d_ff]
        else:
            prev = jnp.pad(state[5][l], ((0, 0), (0, 0), (0, D_FF_PAD - d_ff)))
            a, fc = conv_step((gu, 0), None, (gu, D_FF_PAD), wts["ffn_conv_w"][l], prev,
                              ncols=D_FF_PAD, act="silu", out_dtype=BF16)
            fc = fc[:, :, :d_ff]
        n_fc.append(fc)
        x = matmul(a, wts["w_ffn_down"][l], res=x, mode="res", tm=tm, tk=D_FF_PAD // 4)
    y = rmsnorm(x, wts["norm_final_g"], F32)
    return (y, jnp.stack(n_re), jnp.stack(n_im), jnp.stack(n_sc), jnp.stack(n_k), jnp.stack(n_v),
            jnp.stack(n_fc))


def kernel(x_prompt, x_sample, state_ssm_re, state_ssm_im, state_sconv, cache_k, cache_v, state_ffn_conv, norm_mix_g, norm_ffn_g, norm_final_g, w_in_even, ssm_lambda_re, ssm_lambda_im, ssm_log_dt, ssm_b_re, ssm_b_im, ssm_c_re, ssm_c_im, ssm_d, w_glu, b_glu, sconv_w, w_out_even, w_qkv, b_qkv, attn_sinks, w_o, b_o, w_ffn_gate, w_ffn_up, ffn_conv_w, w_ffn_down):
    n_even = w_in_even.shape[0]
    d_ff = w_ffn_gate.shape[2]
    pad_ff = D_FF_PAD - d_ff
    wts = dict(
        d_ff=d_ff, norm_mix_g=norm_mix_g, norm_ffn_g=norm_ffn_g, norm_final_g=norm_final_g,
        w_in_even=w_in_even.astype(BF16), w_glu=w_glu.astype(BF16), b_glu=b_glu,
        sconv_w=sconv_w, w_out_even=w_out_even.astype(BF16), ssm_d=ssm_d,
        w_qkv=w_qkv.astype(BF16), b_qkv=b_qkv, attn_sinks=attn_sinks, w_o=w_o.astype(BF16), b_o=b_o,
        w_ffn_gu=jnp.concatenate([jnp.pad(w_ffn_gate.astype(BF16), ((0, 0), (0, 0), (0, pad_ff))),
                                  jnp.pad(w_ffn_up.astype(BF16), ((0, 0), (0, 0), (0, pad_ff)))], axis=2),
        ffn_conv_w=jnp.pad(ffn_conv_w, ((0, 0), (0, 0), (0, pad_ff))),
        w_ffn_down=jnp.pad(w_ffn_down.astype(BF16), ((0, 0), (0, pad_ff), (0, 0))),
        s5=[s5_prepare(ssm_lambda_re[j], ssm_lambda_im[j], ssm_log_dt[j], ssm_b_re[j], ssm_b_im[j],
                       ssm_c_re[j], ssm_c_im[j], ssm_d[j]) for j in range(n_even)],
    )
    bp, lp, d_model = x_prompt.shape
    bs = x_sample.shape[0]
    yp, p_re, p_im, p_sc, p_k, p_v, p_fc = _trunk(x_prompt.reshape(bp * lp, d_model), wts, batch=bp, state=None)
    ys, s_re, s_im, s_sc, s_k, s_v, s_fc = _trunk(
        x_sample.reshape(bs, d_model), wts, batch=bs,
        state=(state_ssm_re, state_ssm_im, state_sconv, cache_k, cache_v, state_ffn_conv))
    return (yp.reshape(bp, lp, d_model), ys.reshape(bs, 1, d_model), p_re, p_im, p_sc, p_k, p_v, p_fc,
            s_re, s_im, s_sc, s_k, s_v, s_fc)
```

```python
import functools

import jax
import jax.numpy as jnp
from jax import lax
from jax.experimental import pallas as pl
from jax.experimental.pallas import tpu as pltpu

F32 = jnp.float32
BF16 = jnp.bfloat16

DEPTH = 4
SSM_GROUP = 16
SSM_STATE = 64
GROUPS_PER_BLOCK = 8
CONV_W = 3
HEAD_DIM = 128
N_KV_HEADS = 8
KV_GROUP = 4
WINDOW = 128
RMS_EPS = 1e-5
NEG_INF = -1e30

VMEM_LIMIT_BYTES = 56 * 1024 * 1024


def _params(*sem):
    return pltpu.CompilerParams(dimension_semantics=sem, vmem_limit_bytes=VMEM_LIMIT_BYTES)


def _rmsnorm_kernel(x_ref, g_ref, o_ref):
    x = x_ref[...]
    y = x * lax.rsqrt(jnp.mean(x * x, axis=-1, keepdims=True) + RMS_EPS)
    o_ref[...] = (y * g_ref[...]).astype(o_ref.dtype)


def rmsnorm(x, g, out_dtype):
    m, d = x.shape
    tm = min(m, 256)
    return pl.pallas_call(
        _rmsnorm_kernel,
        grid=(m // tm,),
        in_specs=[pl.BlockSpec((tm, d), lambda i: (i, 0)), pl.BlockSpec((1, d), lambda i: (0, 0))],
        out_specs=pl.BlockSpec((tm, d), lambda i: (i, 0)),
        out_shape=jax.ShapeDtypeStruct((m, d), out_dtype),
        compiler_params=_params("parallel"),
        name="rmsnorm",
    )(x, g.reshape(1, d))


def _mm_kernel(*refs, nk, has_bias, mode):
    it = iter(refs)
    x_ref, w_ref = next(it), next(it)
    b_ref = next(it) if has_bias else None
    r_ref = next(it) if mode != "plain" else None
    o_ref = next(it)
    acc_ref = next(it) if nk > 1 else None

    def finish(acc):
        if has_bias:
            acc = acc + b_ref[...]
        if mode == "res":
            acc = r_ref[...] + acc
        elif mode == "glu":
            acc = r_ref[...] * jax.nn.sigmoid(acc)
        o_ref[...] = acc.astype(o_ref.dtype)

    part = jnp.dot(x_ref[...].astype(BF16), w_ref[...], preferred_element_type=F32)
    if nk == 1:
        finish(part)
    else:
        k = pl.program_id(2)

        @pl.when(k == 0)
        def _():
            acc_ref[...] = part

        @pl.when(k > 0)
        def _():
            acc_ref[...] += part

        @pl.when(k == nk - 1)
        def _():
            finish(acc_ref[...])


def matmul(x, w, *, bias=None, res=None, mode="plain", out_dtype=F32, tm=1024, tn=1024, tk=None):
    m, kdim = x.shape
    n = w.shape[1]
    tm, tn = min(tm, m), min(tn, n)
    tk = kdim if tk is None else tk
    nk = kdim // tk
    grid = (m // tm, n // tn, nk)
    in_specs = [pl.BlockSpec((tm, tk), lambda i, j, k: (i, k)), pl.BlockSpec((tk, tn), lambda i, j, k: (k, j))]
    args = [x, w]
    if bias is not None:
        in_specs.append(pl.BlockSpec((1, tn), lambda i, j, k: (0, j)))
        args.append(bias.reshape(1, n))
    if mode != "plain":
        in_specs.append(pl.BlockSpec((tm, tn), lambda i, j, k: (i, j)))
        args.append(res)
    return pl.pallas_call(
        functools.partial(_mm_kernel, nk=nk, has_bias=bias is not None, mode=mode),
        grid=grid,
        in_specs=in_specs,
        out_specs=pl.BlockSpec((tm, tn), lambda i, j, k: (i, j)),
        out_shape=jax.ShapeDtypeStruct((m, n), out_dtype),
        scratch_shapes=[pltpu.VMEM((tm, tn), F32)] if nk > 1 else [],
        compiler_params=_params("parallel", "parallel", "arbitrary"),
        name="matmul_" + mode,
    )(*args)


def _mm_dual_kernel(*refs, has_bias, mode):
    it = iter(refs)
    xp_ref, xs_ref, w_ref = next(it), next(it), next(it)
    b_ref = next(it) if has_bias else None
    rp_ref, rs_ref = (next(it), next(it)) if mode != "plain" else (None, None)
    op_ref, os_ref, wbf_ref = next(it), next(it), next(it)

    def run(x_ref, r_ref, o_ref):
        acc = jnp.dot(x_ref[...].astype(BF16), wbf_ref[...], preferred_element_type=F32)
        if has_bias:
            acc = acc + b_ref[...]
        if mode == "res":
            acc = r_ref[...] + acc
        elif mode == "glu":
            acc = r_ref[...] * jax.nn.sigmoid(acc)
        o_ref[...] = acc.astype(o_ref.dtype)

    @pl.when(pl.program_id(1) == 0)
    def _():
        wbf_ref[...] = w_ref[...].astype(BF16)
        run(xs_ref, rs_ref, os_ref)

    run(xp_ref, rp_ref, op_ref)


def matmul_dual(xp, xs, w, *, bias=None, res=None, mode="plain", out_dtype=F32, tm=1024, tn=512):
    mp, kdim = xp.shape
    ms = xs.shape[0]
    n = w.shape[1]
    tm, tn = min(tm, mp), min(tn, n)
    in_specs = [pl.BlockSpec((tm, kdim), lambda j, i: (i, 0)),
                pl.BlockSpec((ms, kdim), lambda j, i: (0, 0)),
                pl.BlockSpec((kdim, tn), lambda j, i: (0, j))]
    args = [xp, xs, w]
    if bias is not None:
        in_specs.append(pl.BlockSpec((1, tn), lambda j, i: (0, j)))
        args.append(bias.reshape(1, n))
    if mode != "plain":
        in_specs += [pl.BlockSpec((tm, tn), lambda j, i: (i, j)), pl.BlockSpec((ms, tn), lambda j, i: (0, j))]
        args += list(res)
    return pl.pallas_call(
        functools.partial(_mm_dual_kernel, has_bias=bias is not None, mode=mode),
        grid=(n // tn, mp // tm),
        in_specs=in_specs,
        out_specs=[pl.BlockSpec((tm, tn), lambda j, i: (i, j)), pl.BlockSpec((ms, tn), lambda j, i: (0, j))],
        out_shape=[jax.ShapeDtypeStruct((mp, n), out_dtype), jax.ShapeDtypeStruct((ms, n), out_dtype)],
        scratch_shapes=[pltpu.VMEM((kdim, tn), BF16)],
        compiler_params=_params("parallel", "arbitrary"),
        name="matmul_dual_" + mode,
    )(*args)


def _conv3_rows(g, w_ref, carry_ref):
    c1 = carry_ref[7:8, :]
    c2 = carry_ref[6:7, :]
    row = lax.broadcasted_iota(jnp.int32, g.shape, 0)
    g1 = jnp.where(row == 0, c1, pltpu.roll(g, 1, axis=0))
    g2 = jnp.where(row == 0, c2, jnp.where(row == 1, c1, pltpu.roll(g, 2, axis=0)))
    carry_ref[...] = g[g.shape[0] - 8:, :]
    return w_ref[0:1, :] * g2 + w_ref[1:2, :] * g1 + w_ref[2:3, :] * g


def _ffn_gu_kernel(xp_ref, xs_ref, wg_ref, wu_ref, cw_ref, prev_ref, ap_ref, stp_ref, as_ref, sts_ref,
                   wgb_ref, wub_ref, carry_ref, *, tiles_per_seq, nchunk):
    i = pl.program_id(1)

    @pl.when(i == 0)
    def _():
        wgb_ref[...] = wg_ref[...].astype(BF16)
        wub_ref[...] = wu_ref[...].astype(BF16)
        xs = xs_ref[...]
        gs = jnp.dot(xs, wgb_ref[...], preferred_element_type=F32)
        us = jnp.dot(xs, wub_ref[...], preferred_element_type=F32)
        prev = prev_ref[...]
        x2, x1 = prev[:, 0, :], prev[:, 1, :]
        y = cw_ref[0:1, :] * x2 + cw_ref[1:2, :] * x1 + cw_ref[2:3, :] * gs
        as_ref[...] = (jax.nn.silu(y) * us).astype(as_ref.dtype)
        sts_ref[:, 0, :] = x1
        sts_ref[:, 1, :] = gs

    @pl.when(i % tiles_per_seq == 0)
    def _():
        carry_ref[...] = jnp.zeros_like(carry_ref)

    rc = xp_ref.shape[0] // nchunk
    for c in range(nchunk):
        x = xp_ref[c * rc:(c + 1) * rc, :]
        g = jnp.dot(x, wgb_ref[...], preferred_element_type=F32)
        u = jnp.dot(x, wub_ref[...], preferred_element_type=F32)
        y = _conv3_rows(g, cw_ref, carry_ref)
        ap_ref[c * rc:(c + 1) * rc, :] = (jax.nn.silu(y) * u).astype(ap_ref.dtype)

    @pl.when(i % tiles_per_seq == tiles_per_seq - 1)
    def _():
        stp_ref[...] = carry_ref[...]


def ffn_gate_up(xp, xs, wg, wu, cw, prev_s, *, batch, tm=1024, tn=256, nchunk=2):
    mp, d = xp.shape
    ms = xs.shape[0]
    f = wg.shape[1]
    tm = min(tm, mp // batch)
    tiles_per_seq = mp // batch // tm
    return pl.pallas_call(
        functools.partial(_ffn_gu_kernel, tiles_per_seq=tiles_per_seq, nchunk=nchunk),
        grid=(f // tn, mp // tm),
        in_specs=[pl.BlockSpec((tm, d), lambda j, i: (i, 0)),
                  pl.BlockSpec((ms, d), lambda j, i: (0, 0)),
                  pl.BlockSpec((d, tn), lambda j, i: (0, j)),
                  pl.BlockSpec((d, tn), lambda j, i: (0, j)),
                  pl.BlockSpec((CONV_W, tn), lambda j, i: (0, j)),
                  pl.BlockSpec((ms, CONV_W - 1, tn), lambda j, i: (0, 0, j))],
        out_specs=[pl.BlockSpec((tm, tn), lambda j, i: (i, j)),
                   pl.BlockSpec((None, 8, tn), lambda j, i: (i // tiles_per_seq, 0, j)),
                   pl.BlockSpec((ms, tn), lambda j, i: (0, j)),
                   pl.BlockSpec((ms, CONV_W - 1, tn), lambda j, i: (0, 0, j))],
        out_shape=[jax.ShapeDtypeStruct((mp, f), BF16),
                   jax.ShapeDtypeStruct((batch, 8, f), F32),
                   jax.ShapeDtypeStruct((ms, f), BF16),
                   jax.ShapeDtypeStruct((ms, CONV_W - 1, f), F32)],
        scratch_shapes=[pltpu.VMEM((d, tn), BF16), pltpu.VMEM((d, tn), BF16), pltpu.VMEM((8, tn), F32)],
        compiler_params=_params("parallel", "arbitrary"),
        name="ffn_gate_up",
    )(xp, xs, wg, wu, cw, prev_s)


def _conv_seq_kernel(*refs, mul_in, act):
    it = iter(refs)
    p_ref = next(it)
    q_ref = next(it) if mul_in else None
    r_ref, w_ref, o_ref, st_ref, carry_ref = next(it), next(it), next(it), next(it), next(it)
    m = p_ref[...] * q_ref[...] if mul_in else p_ref[...]

    @pl.when(pl.program_id(2) == 0)
    def _():
        carry_ref[...] = jnp.zeros_like(carry_ref)

    y = _conv3_rows(m, w_ref, carry_ref)
    if act == "silu":
        y = jax.nn.silu(y)
    o_ref[...] = (r_ref[...] * y).astype(o_ref.dtype)
    st_ref[...] = carry_ref[...]


def conv_seq(p, q, r, w, *, batch, ncols, act, out_dtype, tt=512, tc=512):
    mul_in = q is not None
    m_rows = p[0].shape[0]
    tt = min(tt, m_rows // batch)
    nt = m_rows // batch // tt
    nc = ncols // tc

    def spec(c0):
        b0 = c0 // tc
        return pl.BlockSpec((tt, tc), lambda j, b, t: (b * nt + t, b0 + j))

    in_specs, args = [spec(p[1])], [p[0]]
    if mul_in:
        in_specs.append(spec(q[1]))
        args.append(q[0])
    in_specs += [spec(r[1]), pl.BlockSpec((CONV_W, tc), lambda j, b, t: (0, j))]
    args += [r[0], w]
    return pl.pallas_call(
        functools.partial(_conv_seq_kernel, mul_in=mul_in, act=act),
        grid=(nc, batch, nt),
        in_specs=in_specs,
        out_specs=[pl.BlockSpec((tt, tc), lambda j, b, t: (b * nt + t, j)),
                   pl.BlockSpec((None, 8, tc), lambda j, b, t: (b, 0, j))],
        out_shape=[jax.ShapeDtypeStruct((m_rows, ncols), out_dtype),
                   jax.ShapeDtypeStruct((batch, 8, ncols), F32)],
        scratch_shapes=[pltpu.VMEM((8, tc), F32)],
        compiler_params=_params("parallel", "parallel", "arbitrary"),
        name="conv_seq_" + act,
    )(*args)


def _conv_step_kernel(*refs, mul_in, act):
    it = iter(refs)
    p_ref = next(it)
    q_ref = next(it) if mul_in else None
    r_ref, w_ref, prev_ref, o_ref, st_ref = next(it), next(it), next(it), next(it), next(it)
    m = p_ref[...] * q_ref[...] if mul_in else p_ref[...]
    prev = prev_ref[...]
    x2, x1 = prev[:, 0, :], prev[:, 1, :]
    y = w_ref[0:1, :] * x2 + w_ref[1:2, :] * x1 + w_ref[2:3, :] * m
    if act == "silu":
        y = jax.nn.silu(y)
    o_ref[...] = (r_ref[...] * y).astype(o_ref.dtype)
    st_ref[:, 0, :] = x1
    st_ref[:, 1, :] = m


def conv_step(p, q, r, w, prev, *, ncols, act, out_dtype, tc=512):
    mul_in = q is not None
    rows = p[0].shape[0]
    nc = ncols // tc

    def spec(c0):
        b0 = c0 // tc
        return pl.BlockSpec((rows, tc), lambda j: (0, b0 + j))

    in_specs, args = [spec(p[1])], [p[0]]
    if mul_in:
        in_specs.append(spec(q[1]))
        args.append(q[0])
    in_specs += [spec(r[1]), pl.BlockSpec((CONV_W, tc), lambda j: (0, j)),
                 pl.BlockSpec((rows, CONV_W - 1, tc), lambda j: (0, 0, j))]
    args += [r[0], w, prev]
    return pl.pallas_call(
        functools.partial(_conv_step_kernel, mul_in=mul_in, act=act),
        grid=(nc,),
        in_specs=in_specs,
        out_specs=[pl.BlockSpec((rows, tc), lambda j: (0, j)),
                   pl.BlockSpec((rows, CONV_W - 1, tc), lambda j: (0, 0, j))],
        out_shape=[jax.ShapeDtypeStruct((rows, ncols), out_dtype),
                   jax.ShapeDtypeStruct((rows, CONV_W - 1, ncols), F32)],
        compiler_params=_params("parallel"),
        name="conv_step_" + act,
    )(*args)


def _s5_prep_kernel(lr_ref, li_ref, ldt_ref, br_ref, bi_ref, pow_ref, bb_ref):
    lr, li = lr_ref[...], li_ref[...]
    dt = jnp.exp(ldt_ref[...])
    mag = jnp.exp(lr * dt)
    ar, ai = mag * jnp.cos(li * dt), mag * jnp.sin(li * dt)
    den = lr * lr + li * li
    cr = ((ar - 1.0) * lr + ai * li) / den
    ci = (ai * lr - (ar - 1.0) * li) / den
    br, bi = br_ref[...], bi_ref[...]
    bb_ref[:, 0:SSM_GROUP, :] = cr * br - ci * bi
    bb_ref[:, SSM_GROUP:2 * SSM_GROUP, :] = cr * bi + ci * br
    pr, pi = ar, ai
    for k in range(8):
        pow_ref[:, k:k + 1, :] = pr
        pow_ref[:, 8 + k:9 + k, :] = pi
        pr, pi = pr * ar - pi * ai, pr * ai + pi * ar


def s5_prepare(lam_re, lam_im, log_dt, b_re, b_im, c_re, c_im, d_skip):
    g, p = lam_re.shape
    i = SSM_GROUP
    nb, gb = g // GROUPS_PER_BLOCK, GROUPS_PER_BLOCK
    pw, bb = pl.pallas_call(
        _s5_prep_kernel,
        out_shape=[jax.ShapeDtypeStruct((g, 16, p), F32), jax.ShapeDtypeStruct((g, 2 * i, p), F32)],
        name="s5_prep",
    )(lam_re.reshape(g, 1, p), lam_im.reshape(g, 1, p), log_dt.reshape(g, 1, 1),
      b_re.transpose(0, 2, 1), b_im.transpose(0, 2, 1))

    def lanes(a):
        return a.reshape(nb, gb, 8, p).transpose(0, 2, 1, 3).reshape(nb, 8, gb * p)

    pr, pi = lanes(pw[:, 0:8]), lanes(pw[:, 8:16])
    rows = jnp.arange(8)[None, :, None]

    def masked(a, k):
        return jnp.where(rows >= k, a[:, k - 1:k, :], 0.0)

    def bcast(a, k):
        return jnp.broadcast_to(a[:, k - 1:k, :], a.shape)

    tab = jnp.concatenate([masked(pr, 1), masked(pi, 1), masked(pr, 2), masked(pi, 2),
                           masked(pr, 4), masked(pi, 4), pr, pi, bcast(pr, 8), bcast(pi, 8)], axis=1)
    eye = jnp.eye(gb, dtype=F32)

    def wb_part(a):
        a = a.reshape(nb, gb, i, p)
        return jnp.einsum('jaip,ab->jaibp', a, eye).reshape(nb, gb * i, gb * p)

    def wc_part(a):
        a = a.reshape(nb, gb, i, p)
        return jnp.einsum('jaip,ab->jbpai', a, eye).reshape(nb, gb * p, gb * i)

    wb = jnp.concatenate([wb_part(bb[:, :i]), wb_part(bb[:, i:])], axis=2).astype(BF16)
    wc = jnp.concatenate([wc_part(c_re.astype(F32)), wc_part(-c_im.astype(F32))], axis=1).astype(BF16)
    return tab, wb, wc, d_skip.reshape(nb, 1, gb * i)


_T_M1, _T_M2, _T_M4, _T_P, _T_A8 = 0, 16, 32, 48, 64


def _s5_seq_kernel(u_ref, wb_ref, wc_ref, tab_ref, d_ref, g_ref, st_ref, h_ref, carry_ref, *, tt):
    half = h_ref.shape[1] // 2
    t = pl.program_id(2)
    u = u_ref[...]
    h_ref[...] = jnp.dot(u.astype(BF16), wb_ref[...], preferred_element_type=F32)

    @pl.when(t == 0)
    def _():
        carry_ref[...] = jnp.zeros_like(carry_ref)

    def tab(off):
        return tab_ref[off:off + 8, :], tab_ref[off + 8:off + 16, :]

    def block(i, carry):
        cr, ci = carry
        r0 = pl.multiple_of(i * 8, 8)
        xr = h_ref[pl.ds(r0, 8), 0:half]
        xi = h_ref[pl.ds(r0, 8), half:2 * half]
        for k, off in ((1, _T_M1), (2, _T_M2), (4, _T_M4)):
            mr, mi = tab(off)
            sr, si = pltpu.roll(xr, k, axis=0), pltpu.roll(xi, k, axis=0)
            xr, xi = xr + (mr * sr - mi * si), xi + (mr * si + mi * sr)
        pr, pi = tab(_T_P)
        h_ref[pl.ds(r0, 8), 0:half] = xr + (pr * cr - pi * ci)
        h_ref[pl.ds(r0, 8), half:2 * half] = xi + (pr * ci + pi * cr)
        ar, ai = tab(_T_A8)
        lr = jnp.broadcast_to(xr[7:8, :], xr.shape)
        li = jnp.broadcast_to(xi[7:8, :], xi.shape)
        return lr + (ar * cr - ai * ci), li + (ar * ci + ai * cr)

    cr, ci = lax.fori_loop(0, tt // 8, block, (carry_ref[:, 0:half], carry_ref[:, half:2 * half]), unroll=2)
    carry_ref[:, 0:half] = cr
    carry_ref[:, half:2 * half] = ci
    st_ref[:, 0:half] = cr
    st_ref[:, half:2 * half] = ci
    y = jnp.dot(h_ref[...].astype(BF16), wc_ref[...], preferred_element_type=F32) + d_ref[...] * u
    g_ref[...] = jax.nn.gelu(y)


def s5_seq(z, prep, *, batch, tt=512):
    tab, wb, wc, d = prep
    nb, cin, cst = wb.shape
    m_rows = z.shape[0]
    tt = min(tt, m_rows // batch)
    nt = m_rows // batch // tt
    return pl.pallas_call(
        functools.partial(_s5_seq_kernel, tt=tt),
        grid=(nb, batch, nt),
        in_specs=[pl.BlockSpec((tt, cin), lambda j, b, t: (b * nt + t, j)),
                  pl.BlockSpec((None, cin, cst), lambda j, b, t: (j, 0, 0)),
                  pl.BlockSpec((None, cst, cin), lambda j, b, t: (j, 0, 0)),
                  pl.BlockSpec((None, tab.shape[1], cst // 2), lambda j, b, t: (j, 0, 0)),
                  pl.BlockSpec((None, 1, cin), lambda j, b, t: (j, 0, 0))],
        out_specs=[pl.BlockSpec((tt, cin), lambda j, b, t: (b * nt + t, j)),
                   pl.BlockSpec((None, None, 8, cst), lambda j, b, t: (j, b, 0, 0))],
        out_shape=[jax.ShapeDtypeStruct((m_rows, nb * cin), F32),
                   jax.ShapeDtypeStruct((nb, batch, 8, cst), F32)],
        scratch_shapes=[pltpu.VMEM((tt, cst), F32), pltpu.VMEM((8, cst), F32)],
        compiler_params=_params("parallel", "parallel", "arbitrary"),
        name="s5_seq",
    )(z, wb, wc, tab, d)


def _s5_step_kernel(u_ref, s_ref, wb_ref, wc_ref, tab_ref, d_ref, g_ref, st_ref):
    half = s_ref.shape[1] // 2
    u = u_ref[...]
    bu = jnp.dot(u.astype(BF16), wb_ref[...], preferred_element_type=F32)
    ar, ai = tab_ref[_T_P:_T_P + 1, :], tab_ref[_T_P + 8:_T_P + 9, :]
    sr, si = s_ref[:, 0:half], s_ref[:, half:2 * half]
    hr = ar * sr - ai * si + bu[:, 0:half]
    hi = ar * si + ai * sr + bu[:, half:2 * half]
    st_ref[:, 0:half] = hr
    st_ref[:, half:2 * half] = hi
    h = jnp.concatenate([hr, hi], axis=1)
    y = jnp.dot(h.astype(BF16), wc_ref[...], preferred_element_type=F32) + d_ref[...] * u
    g_ref[...] = jax.nn.gelu(y)


def s5_step(z, state, prep):
    tab, wb, wc, d = prep
    nb, cin, cst = wb.shape
    rows = z.shape[0]
    return pl.pallas_call(
        _s5_step_kernel,
        grid=(nb,),
        in_specs=[pl.BlockSpec((rows, cin), lambda j: (0, j)),
                  pl.BlockSpec((None, rows, cst), lambda j: (j, 0, 0)),
                  pl.BlockSpec((None, cin, cst), lambda j: (j, 0, 0)),
                  pl.BlockSpec((None, cst, cin), lambda j: (j, 0, 0)),
                  pl.BlockSpec((None, tab.shape[1], cst // 2), lambda j: (j, 0, 0)),
                  pl.BlockSpec((None, 1, cin), lambda j: (j, 0, 0))],
        out_specs=[pl.BlockSpec((rows, cin), lambda j: (0, j)),
                   pl.BlockSpec((None, rows, cst), lambda j: (j, 0, 0))],
        out_shape=[jax.ShapeDtypeStruct((rows, nb * cin), F32),
                   jax.ShapeDtypeStruct((nb, rows, cst), F32)],
        compiler_params=_params("parallel"),
        name="s5_step",
    )(z, state, wb, wc, tab, d)


def _state_to_blocks(s_re, s_im):
    b, g, p = s_re.shape
    nb = g // GROUPS_PER_BLOCK

    def f(s):
        return s.reshape(b, nb, GROUPS_PER_BLOCK * p).transpose(1, 0, 2)

    return jnp.concatenate([f(s_re), f(s_im)], axis=2)


def _blocks_to_state(st, p=SSM_STATE):
    nb, b, c = st.shape
    half = c // 2

    def f(s):
        return s.transpose(1, 0, 2).reshape(b, nb * half // p, p)

    return f(st[:, :, :half]), f(st[:, :, half:])


def _attn_seq_kernel(q_ref, kp_ref, kc_ref, vp_ref, vc_ref, sk_ref, o_ref):
    hd, blk = HEAD_DIM, WINDOW
    n = pl.program_id(1)
    shape = (KV_GROUP * blk, 2 * blk)
    qi = lax.broadcasted_iota(jnp.int32, shape, 0) % blk
    kj = lax.broadcasted_iota(jnp.int32, shape, 1)
    vis = (kj >= qi) & (kj <= qi + WINDOW) & ((kj >= blk) | (n > 0))
    for h in range(N_KV_HEADS):
        c0 = h * KV_GROUP * hd
        qa = jnp.concatenate([q_ref[:, c0 + g * hd:c0 + (g + 1) * hd] for g in range(KV_GROUP)], axis=0)
        kb = jnp.concatenate([kp_ref[:, h * hd:(h + 1) * hd], kc_ref[:, h * hd:(h + 1) * hd]], axis=0)
        vb = jnp.concatenate([vp_ref[:, h * hd:(h + 1) * hd], vc_ref[:, h * hd:(h + 1) * hd]], axis=0)
        s = lax.dot_general(qa.astype(BF16), kb.astype(BF16), (((1,), (1,)), ((), ())),
                            preferred_element_type=F32) * (hd ** -0.5)
        s = jnp.where(vis, s, NEG_INF)
        sk = sk_ref[h]
        m = jnp.maximum(jnp.max(s, axis=-1, keepdims=True), sk)
        p = jnp.exp(s - m)
        w = p / (jnp.sum(p, axis=-1, keepdims=True) + jnp.exp(sk - m))
        o = jnp.dot(w.astype(BF16), vb.astype(BF16), preferred_element_type=F32)
        for g in range(KV_GROUP):
            o_ref[:, c0 + g * hd:c0 + (g + 1) * hd] = o[g * blk:(g + 1) * blk].astype(o_ref.dtype)


def attn_seq(qkv, sinks, *, batch, out_dtype):
    m_rows = qkv.shape[0]
    nblk = m_rows // batch // WINDOW
    nq = N_KV_HEADS * KV_GROUP
    qw, kvw = nq * HEAD_DIM, N_KV_HEADS * HEAD_DIM
    kblk, vblk = qw // kvw, qw // kvw + 1
    sk = jnp.repeat(sinks.astype(F32).reshape(N_KV_HEADS, KV_GROUP), WINDOW, axis=1)[..., None]

    def cur(c):
        return pl.BlockSpec((WINDOW, kvw), lambda b, n: (b * nblk + n, c))

    def prev(c):
        return pl.BlockSpec((WINDOW, kvw), lambda b, n: (b * nblk + jnp.maximum(n - 1, 0), c))

    return pl.pallas_call(
        _attn_seq_kernel,
        grid=(batch, nblk),
        in_specs=[pl.BlockSpec((WINDOW, qw), lambda b, n: (b * nblk + n, 0)),
                  prev(kblk), cur(kblk), prev(vblk), cur(vblk),
                  pl.BlockSpec((N_KV_HEADS, KV_GROUP * WINDOW, 1), lambda b, n: (0, 0, 0))],
        out_specs=pl.BlockSpec((WINDOW, qw), lambda b, n: (b * nblk + n, 0)),
        out_shape=jax.ShapeDtypeStruct((m_rows, qw), out_dtype),
        compiler_params=_params("parallel", "parallel"),
        name="attn_seq",
    )(qkv, qkv, qkv, qkv, qkv, sk)


def _attn_step_kernel(q_ref, kn_ref, vn_ref, kc_ref, vc_ref, sk_ref, o_ref):
    hd = HEAD_DIM
    scale = hd ** -0.5
    for h in range(N_KV_HEADS):
        rows = slice(h * KV_GROUP, (h + 1) * KV_GROUP)
        qh = q_ref[rows, :].astype(BF16)
        kh = kc_ref[:, h * hd:(h + 1) * hd].astype(BF16)
        vh = vc_ref[:, h * hd:(h + 1) * hd].astype(BF16)
        kn = kn_ref[h:h + 1, :].astype(BF16).astype(F32)
        vn = vn_ref[h:h + 1, :].astype(BF16).astype(F32)
        sc = lax.dot_general(qh, kh, (((1,), (1,)), ((), ())), preferred_element_type=F32) * scale
        sn = jnp.sum(qh.astype(F32) * kn, axis=-1, keepdims=True) * scale
        sk = sk_ref[rows, :]
        m = jnp.maximum(jnp.maximum(jnp.max(sc, axis=-1, keepdims=True), sn), sk)
        pc, pn = jnp.exp(sc - m), jnp.exp(sn - m)
        den = jnp.sum(pc, axis=-1, keepdims=True) + pn + jnp.exp(sk - m)
        o = jnp.dot((pc / den).astype(BF16), vh, preferred_element_type=F32)
        o = o + (pn / den).astype(BF16).astype(F32) * vn
        o_ref[rows, :] = o.astype(o_ref.dtype)


def attn_step(qkv, k_buf, v_buf, sinks, *, out_dtype):
    rows = qkv.shape[0]
    nq = N_KV_HEADS * KV_GROUP
    wb = k_buf.shape[1]
    q3 = qkv[:, :nq * HEAD_DIM].reshape(rows, nq, HEAD_DIM)
    kn = qkv[:, nq * HEAD_DIM:(nq + N_KV_HEADS) * HEAD_DIM].reshape(rows, N_KV_HEADS, HEAD_DIM)
    vn = qkv[:, (nq + N_KV_HEADS) * HEAD_DIM:].reshape(rows, N_KV_HEADS, HEAD_DIM)
    kvw = N_KV_HEADS * HEAD_DIM
    o = pl.pallas_call(
        _attn_step_kernel,
        grid=(rows,),
        in_specs=[pl.BlockSpec((None, nq, HEAD_DIM), lambda b: (b, 0, 0)),
                  pl.BlockSpec((None, N_KV_HEADS, HEAD_DIM), lambda b: (b, 0, 0)),
                  pl.BlockSpec((None, N_KV_HEADS, HEAD_DIM), lambda b: (b, 0, 0)),
                  pl.BlockSpec((None, wb, kvw), lambda b: (b, 0, 0)),
                  pl.BlockSpec((None, wb, kvw), lambda b: (b, 0, 0)),
                  pl.BlockSpec((nq, 1), lambda b: (0, 0))],
        out_specs=pl.BlockSpec((None, nq, HEAD_DIM), lambda b: (b, 0, 0)),
        out_shape=jax.ShapeDtypeStruct((rows, nq, HEAD_DIM), out_dtype),
        compiler_params=_params("parallel"),
        name="attn_step",
    )(q3, kn, vn, k_buf.reshape(rows, wb, kvw), v_buf.reshape(rows, wb, kvw),
      sinks.astype(F32).reshape(nq, 1))
    new_k = jnp.concatenate([k_buf, kn[:, None]], axis=1)[:, 1:]
    new_v = jnp.concatenate([v_buf, vn[:, None]], axis=1)[:, 1:]
    return o.reshape(rows, nq * HEAD_DIM), new_k, new_v


def kernel(x_prompt, x_sample, state_ssm_re, state_ssm_im, state_sconv, cache_k, cache_v, state_ffn_conv, norm_mix_g, norm_ffn_g, norm_final_g, w_in_even, ssm_lambda_re, ssm_lambda_im, ssm_log_dt, ssm_b_re, ssm_b_im, ssm_c_re, ssm_c_im, ssm_d, w_glu, b_glu, sconv_w, w_out_even, w_qkv, b_qkv, attn_sinks, w_o, b_o, w_ffn_gate, w_ffn_up, ffn_conv_w, w_ffn_down):
    bp, lp, d_model = x_prompt.shape
    bs = x_sample.shape[0]
    d_ssm, d_conv = ssm_d.shape[1], sconv_w.shape[2]
    nq, nkv = N_KV_HEADS * KV_GROUP * HEAD_DIM, N_KV_HEADS * HEAD_DIM
    keep = min(WINDOW, lp)
    w_down_bf16 = w_ffn_down.astype(BF16)
    xp, xs = x_prompt.reshape(bp * lp, d_model), x_sample.reshape(bs, d_model)
    p_out = [[] for _ in range(6)]
    s_out = [[] for _ in range(6)]
    for l in range(DEPTH):
        j = l // 2
        hp, hs = rmsnorm(xp, norm_mix_g[l], BF16), rmsnorm(xs, norm_mix_g[l], BF16)
        if l % 2 == 0:
            prep = s5_prepare(ssm_lambda_re[j], ssm_lambda_im[j], ssm_log_dt[j], ssm_b_re[j], ssm_b_im[j],
                              ssm_c_re[j], ssm_c_im[j], ssm_d[j])
            zp, zs = matmul_dual(hp, hs, w_in_even[j])
            cv, cb, cc = d_ssm, d_ssm + d_conv, d_ssm + 2 * d_conv
            gp, stp = s5_seq(zp, prep, batch=bp)
            gs, sts = s5_step(zs, _state_to_blocks(state_ssm_re[j], state_ssm_im[j]), prep)
            for out, st in ((p_out, stp[:, :, 0, :]), (s_out, sts)):
                re, im = _blocks_to_state(st)
                out[0].append(re)
                out[1].append(im)
            bp_out, scp = conv_seq((zp, cc), (zp, cv), (zp, cb), sconv_w[j], batch=bp, ncols=d_conv,
                                   act="none", out_dtype=BF16)
            bs_out, scs = conv_step((zs, cc), (zs, cv), (zs, cb), sconv_w[j], state_sconv[j], ncols=d_conv,
                                    act="none", out_dtype=BF16)
            p_out[2].append(scp[:, 8 - (CONV_W - 1):, :])
            s_out[2].append(scs)
            ap, a_s = matmul_dual(gp, gs, w_glu[j], bias=b_glu[j], res=(gp, gs), mode="glu", out_dtype=BF16)
            xp, xs = matmul_dual(jnp.concatenate([ap, bp_out], axis=1), jnp.concatenate([a_s, bs_out], axis=1),
                                 w_out_even[j], res=(xp, xs), mode="res")
        else:
            qkvp, qkvs = matmul_dual(hp, hs, w_qkv[j], bias=b_qkv[j])
            op = attn_seq(qkvp, attn_sinks[j], batch=bp, out_dtype=BF16)
            kv = qkvp.reshape(bp, lp, -1)[:, lp - keep:, nq:]
            p_out[3].append(kv[..., :nkv].reshape(bp, keep, N_KV_HEADS, HEAD_DIM))
            p_out[4].append(kv[..., nkv:].reshape(bp, keep, N_KV_HEADS, HEAD_DIM))
            o_s, kk, vv = attn_step(qkvs, cache_k[j], cache_v[j], attn_sinks[j], out_dtype=BF16)
            s_out[3].append(kk)
            s_out[4].append(vv)
            xp, xs = matmul_dual(op, o_s, w_o[j], bias=b_o[j], res=(xp, xs), mode="res")
        hp, hs = rmsnorm(xp, norm_ffn_g[l], BF16), rmsnorm(xs, norm_ffn_g[l], BF16)
        ap, fcp, a_s, fcs = ffn_gate_up(hp, hs, w_ffn_gate[l], w_ffn_up[l], ffn_conv_w[l], state_ffn_conv[l],
                                        batch=bp)
        p_out[5].append(fcp[:, 8 - (CONV_W - 1):, :])
        s_out[5].append(fcs)
        xp = matmul(ap, w_down_bf16[l], res=xp, mode="res", tn=512, tk=w_down_bf16.shape[1] // 2)
        xs = matmul(a_s, w_down_bf16[l], res=xs, mode="res", tn=512, tk=w_down_bf16.shape[1] // 2)
    yp, ys = rmsnorm(xp, norm_final_g, F32), rmsnorm(xs, norm_final_g, F32)
    return (yp.reshape(bp, lp, d_model), ys.reshape(bs, 1, d_model), *[jnp.stack(t) for t in p_out],
            *[jnp.stack(t) for t in s_out])
```

```python
import functools

import jax
import jax.numpy as jnp
from jax import lax
from jax.experimental import pallas as pl
from jax.experimental.pallas import tpu as pltpu

F32 = jnp.float32
BF16 = jnp.bfloat16

DEPTH = 4
SSM_GROUP = 16
SSM_STATE = 64
GROUPS_PER_BLOCK = 8
CONV_W = 3
HEAD_DIM = 128
N_KV_HEADS = 8
KV_GROUP = 4
WINDOW = 128
RMS_EPS = 1e-5
NEG_INF = -1e30

VMEM_LIMIT_BYTES = 60 * 1024 * 1024


def _params(*sem):
    return pltpu.CompilerParams(dimension_semantics=sem, vmem_limit_bytes=VMEM_LIMIT_BYTES)


def _rmsnorm_kernel(x_ref, g_ref, o_ref):
    x = x_ref[...]
    y = x * lax.rsqrt(jnp.mean(x * x, axis=-1, keepdims=True) + RMS_EPS)
    o_ref[...] = (y * g_ref[...]).astype(o_ref.dtype)


def rmsnorm(x, g, out_dtype):
    m, d = x.shape
    tm = min(m, 256)
    return pl.pallas_call(
        _rmsnorm_kernel,
        grid=(m // tm,),
        in_specs=[pl.BlockSpec((tm, d), lambda i: (i, 0)), pl.BlockSpec((1, d), lambda i: (0, 0))],
        out_specs=pl.BlockSpec((tm, d), lambda i: (i, 0)),
        out_shape=jax.ShapeDtypeStruct((m, d), out_dtype),
        compiler_params=_params("parallel"),
        name="rmsnorm",
    )(x, g.reshape(1, d))


def _mm_kernel(*refs, nk, has_bias, mode):
    it = iter(refs)
    x_ref, w_ref = next(it), next(it)
    b_ref = next(it) if has_bias else None
    r_ref = next(it) if mode != "plain" else None
    o_ref = next(it)
    acc_ref = next(it) if nk > 1 else None

    def finish(acc):
        if has_bias:
            acc = acc + b_ref[...]
        if mode == "res":
            acc = r_ref[...] + acc
        elif mode == "glu":
            acc = r_ref[...] * jax.nn.sigmoid(acc)
        o_ref[...] = acc.astype(o_ref.dtype)

    part = jnp.dot(x_ref[...].astype(BF16), w_ref[...], preferred_element_type=F32)
    if nk == 1:
        finish(part)
    else:
        k = pl.program_id(2)

        @pl.when(k == 0)
        def _():
            acc_ref[...] = part

        @pl.when(k > 0)
        def _():
            acc_ref[...] += part

        @pl.when(k == nk - 1)
        def _():
            finish(acc_ref[...])


def matmul(x, w, layer, *, bias=None, res=None, mode="plain", out_dtype=F32, tm=1024, tn=1024, tk=None):
    m, kdim = x.shape
    n = w.shape[2]
    tm, tn = min(tm, m), min(tn, n)
    tk = kdim if tk is None else tk
    nk = kdim // tk
    grid = (m // tm, n // tn, nk)
    in_specs = [pl.BlockSpec((tm, tk), lambda i, j, k: (i, k)),
                pl.BlockSpec((None, tk, tn), lambda i, j, k: (layer, k, j))]
    args = [x, w]
    if bias is not None:
        in_specs.append(pl.BlockSpec((None, 1, tn), lambda i, j, k: (layer, 0, j)))
        args.append(bias.reshape(bias.shape[0], 1, n))
    if mode != "plain":
        in_specs.append(pl.BlockSpec((tm, tn), lambda i, j, k: (i, j)))
        args.append(res)
    return pl.pallas_call(
        functools.partial(_mm_kernel, nk=nk, has_bias=bias is not None, mode=mode),
        grid=grid,
        in_specs=in_specs,
        out_specs=pl.BlockSpec((tm, tn), lambda i, j, k: (i, j)),
        out_shape=jax.ShapeDtypeStruct((m, n), out_dtype),
        scratch_shapes=[pltpu.VMEM((tm, tn), F32)] if nk > 1 else [],
        compiler_params=_params("parallel", "parallel", "arbitrary"),
        name="matmul_" + mode,
    )(*args)


def _mm_dual_kernel(*refs, has_bias, mode):
    it = iter(refs)
    xp_ref, xs_ref, w_ref = next(it), next(it), next(it)
    b_ref = next(it) if has_bias else None
    rp_ref, rs_ref = (next(it), next(it)) if mode != "plain" else (None, None)
    op_ref, os_ref, wbf_ref = next(it), next(it), next(it)

    def run(x_ref, r_ref, o_ref):
        acc = jnp.dot(x_ref[...].astype(BF16), wbf_ref[...], preferred_element_type=F32)
        if has_bias:
            acc = acc + b_ref[...]
        if mode == "res":
            acc = r_ref[...] + acc
        elif mode == "glu":
            acc = r_ref[...] * jax.nn.sigmoid(acc)
        o_ref[...] = acc.astype(o_ref.dtype)

    @pl.when(pl.program_id(1) == 0)
    def _():
        wbf_ref[...] = w_ref[...].astype(BF16)
        run(xs_ref, rs_ref, os_ref)

    run(xp_ref, rp_ref, op_ref)


def matmul_dual(xp, xs, w, layer, *, bias=None, res=None, mode="plain", out_dtype=F32, tm=1024, tn=512):
    mp, kdim = xp.shape
    ms = xs.shape[0]
    n = w.shape[2]
    tm, tn = min(tm, mp), min(tn, n)
    in_specs = [pl.BlockSpec((tm, kdim), lambda j, i: (i, 0)),
                pl.BlockSpec((ms, kdim), lambda j, i: (0, 0)),
                pl.BlockSpec((None, kdim, tn), lambda j, i: (layer, 0, j))]
    args = [xp, xs, w]
    if bias is not None:
        in_specs.append(pl.BlockSpec((None, 1, tn), lambda j, i: (layer, 0, j)))
        args.append(bias.reshape(bias.shape[0], 1, n))
    if mode != "plain":
        in_specs += [pl.BlockSpec((tm, tn), lambda j, i: (i, j)), pl.BlockSpec((ms, tn), lambda j, i: (0, j))]
        args += list(res)
    return pl.pallas_call(
        functools.partial(_mm_dual_kernel, has_bias=bias is not None, mode=mode),
        grid=(n // tn, mp // tm),
        in_specs=in_specs,
        out_specs=[pl.BlockSpec((tm, tn), lambda j, i: (i, j)), pl.BlockSpec((ms, tn), lambda j, i: (0, j))],
        out_shape=[jax.ShapeDtypeStruct((mp, n), out_dtype), jax.ShapeDtypeStruct((ms, n), out_dtype)],
        scratch_shapes=[pltpu.VMEM((kdim, tn), BF16)],
        compiler_params=_params("parallel", "arbitrary"),
        name="matmul_dual_" + mode,
    )(*args)


def _conv3_rows(g, w_ref, carry_ref):
    c1 = carry_ref[7:8, :]
    c2 = carry_ref[6:7, :]
    row = lax.broadcasted_iota(jnp.int32, g.shape, 0)
    g1 = jnp.where(row == 0, c1, pltpu.roll(g, 1, axis=0))
    g2 = jnp.where(row == 0, c2, jnp.where(row == 1, c1, pltpu.roll(g, 2, axis=0)))
    carry_ref[...] = g[g.shape[0] - 8:, :]
    return w_ref[0:1, :] * g2 + w_ref[1:2, :] * g1 + w_ref[2:3, :] * g


def _ffn_gu_kernel(xp_ref, xs_ref, wg_ref, wu_ref, cw_ref, prev_ref, ap_ref, stp_ref, as_ref, sts_ref,
                   wgb_ref, wub_ref, carry_ref, *, tiles_per_seq, nchunk):
    i = pl.program_id(1)

    @pl.when(i == 0)
    def _():
        wgb_ref[...] = wg_ref[...].astype(BF16)
        wub_ref[...] = wu_ref[...].astype(BF16)
        xs = xs_ref[...]
        gs = jnp.dot(xs, wgb_ref[...], preferred_element_type=F32)
        us = jnp.dot(xs, wub_ref[...], preferred_element_type=F32)
        prev = prev_ref[...]
        x2, x1 = prev[:, 0, :], prev[:, 1, :]
        y = cw_ref[0:1, :] * x2 + cw_ref[1:2, :] * x1 + cw_ref[2:3, :] * gs
        as_ref[...] = (jax.nn.silu(y) * us).astype(as_ref.dtype)
        sts_ref[:, 0, :] = x1
        sts_ref[:, 1, :] = gs

    @pl.when(i % tiles_per_seq == 0)
    def _():
        carry_ref[...] = jnp.zeros_like(carry_ref)

    rc = xp_ref.shape[0] // nchunk
    for c in range(nchunk):
        x = xp_ref[c * rc:(c + 1) * rc, :]
        g = jnp.dot(x, wgb_ref[...], preferred_element_type=F32)
        u = jnp.dot(x, wub_ref[...], preferred_element_type=F32)
        y = _conv3_rows(g, cw_ref, carry_ref)
        ap_ref[c * rc:(c + 1) * rc, :] = (jax.nn.silu(y) * u).astype(ap_ref.dtype)

    @pl.when(i % tiles_per_seq == tiles_per_seq - 1)
    def _():
        stp_ref[...] = carry_ref[...]


def ffn_gate_up(xp, xs, wg, wu, cw, prev_s, layer, *, batch, tm=2048, tn=256, nchunk=4):
    mp, d = xp.shape
    ms = xs.shape[0]
    f = wg.shape[2]
    tm = min(tm, mp // batch)
    tiles_per_seq = mp // batch // tm
    return pl.pallas_call(
        functools.partial(_ffn_gu_kernel, tiles_per_seq=tiles_per_seq, nchunk=nchunk),
        grid=(f // tn, mp // tm),
        in_specs=[pl.BlockSpec((tm, d), lambda j, i: (i, 0)),
                  pl.BlockSpec((ms, d), lambda j, i: (0, 0)),
                  pl.BlockSpec((None, d, tn), lambda j, i: (layer, 0, j)),
                  pl.BlockSpec((None, d, tn), lambda j, i: (layer, 0, j)),
                  pl.BlockSpec((None, CONV_W, tn), lambda j, i: (layer, 0, j)),
                  pl.BlockSpec((None, ms, CONV_W - 1, tn), lambda j, i: (layer, 0, 0, j))],
        out_specs=[pl.BlockSpec((tm, tn), lambda j, i: (i, j)),
                   pl.BlockSpec((None, 8, tn), lambda j, i: (i // tiles_per_seq, 0, j)),
                   pl.BlockSpec((ms, tn), lambda j, i: (0, j)),
                   pl.BlockSpec((ms, CONV_W - 1, tn), lambda j, i: (0, 0, j))],
        out_shape=[jax.ShapeDtypeStruct((mp, f), BF16),
                   jax.ShapeDtypeStruct((batch, 8, f), F32),
                   jax.ShapeDtypeStruct((ms, f), BF16),
                   jax.ShapeDtypeStruct((ms, CONV_W - 1, f), F32)],
        scratch_shapes=[pltpu.VMEM((d, tn), BF16), pltpu.VMEM((d, tn), BF16), pltpu.VMEM((8, tn), F32)],
        compiler_params=_params("parallel", "arbitrary"),
        name="ffn_gate_up",
    )(xp, xs, wg, wu, cw, prev_s)


def _conv_seq_kernel(*refs, mul_in, act):
    it = iter(refs)
    p_ref = next(it)
    q_ref = next(it) if mul_in else None
    r_ref, w_ref, o_ref, st_ref, carry_ref = next(it), next(it), next(it), next(it), next(it)
    m = p_ref[...] * q_ref[...] if mul_in else p_ref[...]

    @pl.when(pl.program_id(2) == 0)
    def _():
        carry_ref[...] = jnp.zeros_like(carry_ref)

    y = _conv3_rows(m, w_ref, carry_ref)
    if act == "silu":
        y = jax.nn.silu(y)
    o_ref[...] = (r_ref[...] * y).astype(o_ref.dtype)
    st_ref[...] = carry_ref[...]


def conv_seq(p, q, r, w, *, batch, ncols, act, out_dtype, tt=512, tc=512):
    mul_in = q is not None
    m_rows = p[0].shape[0]
    tt = min(tt, m_rows // batch)
    nt = m_rows // batch // tt
    nc = ncols // tc

    def spec(c0):
        b0 = c0 // tc
        return pl.BlockSpec((tt, tc), lambda j, b, t: (b * nt + t, b0 + j))

    in_specs, args = [spec(p[1])], [p[0]]
    if mul_in:
        in_specs.append(spec(q[1]))
        args.append(q[0])
    in_specs += [spec(r[1]), pl.BlockSpec((CONV_W, tc), lambda j, b, t: (0, j))]
    args += [r[0], w]
    return pl.pallas_call(
        functools.partial(_conv_seq_kernel, mul_in=mul_in, act=act),
        grid=(nc, batch, nt),
        in_specs=in_specs,
        out_specs=[pl.BlockSpec((tt, tc), lambda j, b, t: (b * nt + t, j)),
                   pl.BlockSpec((None, 8, tc), lambda j, b, t: (b, 0, j))],
        out_shape=[jax.ShapeDtypeStruct((m_rows, ncols), out_dtype),
                   jax.ShapeDtypeStruct((batch, 8, ncols), F32)],
        scratch_shapes=[pltpu.VMEM((8, tc), F32)],
        compiler_params=_params("parallel", "parallel", "arbitrary"),
        name="conv_seq_" + act,
    )(*args)


def _conv_step_kernel(*refs, mul_in, act):
    it = iter(refs)
    p_ref = next(it)
    q_ref = next(it) if mul_in else None
    r_ref, w_ref, prev_ref, o_ref, st_ref = next(it), next(it), next(it), next(it), next(it)
    m = p_ref[...] * q_ref[...] if mul_in else p_ref[...]
    prev = prev_ref[...]
    x2, x1 = prev[:, 0, :], prev[:, 1, :]
    y = w_ref[0:1, :] * x2 + w_ref[1:2, :] * x1 + w_ref[2:3, :] * m
    if act == "silu":
        y = jax.nn.silu(y)
    o_ref[...] = (r_ref[...] * y).astype(o_ref.dtype)
    st_ref[:, 0, :] = x1
    st_ref[:, 1, :] = m


def conv_step(p, q, r, w, prev, *, ncols, act, out_dtype, tc=512):
    mul_in = q is not None
    rows = p[0].shape[0]
    nc = ncols // tc

    def spec(c0):
        b0 = c0 // tc
        return pl.BlockSpec((rows, tc), lambda j: (0, b0 + j))

    in_specs, args = [spec(p[1])], [p[0]]
    if mul_in:
        in_specs.append(spec(q[1]))
        args.append(q[0])
    in_specs += [spec(r[1]), pl.BlockSpec((CONV_W, tc), lambda j: (0, j)),
                 pl.BlockSpec((rows, CONV_W - 1, tc), lambda j: (0, 0, j))]
    args += [r[0], w, prev]
    return pl.pallas_call(
        functools.partial(_conv_step_kernel, mul_in=mul_in, act=act),
        grid=(nc,),
        in_specs=in_specs,
        out_specs=[pl.BlockSpec((rows, tc), lambda j: (0, j)),
                   pl.BlockSpec((rows, CONV_W - 1, tc), lambda j: (0, 0, j))],
        out_shape=[jax.ShapeDtypeStruct((rows, ncols), out_dtype),
                   jax.ShapeDtypeStruct((rows, CONV_W - 1, ncols), F32)],
        compiler_params=_params("parallel"),
        name="conv_step_" + act,
    )(*args)


def _s5_prep_kernel(lr_ref, li_ref, ldt_ref, br_ref, bi_ref, pow_ref, bb_ref):
    lr, li = lr_ref[...], li_ref[...]
    dt = jnp.exp(ldt_ref[...])
    mag = jnp.exp(lr * dt)
    ar, ai = mag * jnp.cos(li * dt), mag * jnp.sin(li * dt)
    den = lr * lr + li * li
    cr = ((ar - 1.0) * lr + ai * li) / den
    ci = (ai * lr - (ar - 1.0) * li) / den
    br, bi = br_ref[...], bi_ref[...]
    bb_ref[:, 0:SSM_GROUP, :] = cr * br - ci * bi
    bb_ref[:, SSM_GROUP:2 * SSM_GROUP, :] = cr * bi + ci * br
    pr, pi = ar, ai
    for k in range(8):
        pow_ref[:, k:k + 1, :] = pr
        pow_ref[:, 8 + k:9 + k, :] = pi
        pr, pi = pr * ar - pi * ai, pr * ai + pi * ar


def s5_prepare(lam_re, lam_im, log_dt, b_re, b_im, c_re, c_im, d_skip):
    g, p = lam_re.shape
    i = SSM_GROUP
    nb, gb = g // GROUPS_PER_BLOCK, GROUPS_PER_BLOCK
    pw, bb = pl.pallas_call(
        _s5_prep_kernel,
        out_shape=[jax.ShapeDtypeStruct((g, 16, p), F32), jax.ShapeDtypeStruct((g, 2 * i, p), F32)],
        name="s5_prep",
    )(lam_re.reshape(g, 1, p), lam_im.reshape(g, 1, p), log_dt.reshape(g, 1, 1),
      b_re.transpose(0, 2, 1), b_im.transpose(0, 2, 1))

    def lanes(a):
        return a.reshape(nb, gb, 8, p).transpose(0, 2, 1, 3).reshape(nb, 8, gb * p)

    pr, pi = lanes(pw[:, 0:8]), lanes(pw[:, 8:16])
    rows = jnp.arange(8)[None, :, None]

    def masked(a, k):
        return jnp.where(rows >= k, a[:, k - 1:k, :], 0.0)

    tab = jnp.concatenate([masked(pr, 1), masked(pi, 1), masked(pr, 2), masked(pi, 2),
                           masked(pr, 4), masked(pi, 4), pr, pi], axis=1)
    eye = jnp.eye(gb, dtype=F32)

    def wb_part(a):
        a = a.reshape(nb, gb, i, p)
        return jnp.einsum('jaip,ab->jaibp', a, eye).reshape(nb, gb * i, gb * p)

    def wc_part(a):
        a = a.reshape(nb, gb, i, p)
        return jnp.einsum('jaip,ab->jbpai', a, eye).reshape(nb, gb * p, gb * i)

    wb = jnp.concatenate([wb_part(bb[:, :i]), wb_part(bb[:, i:])], axis=2).astype(BF16)
    wc = jnp.concatenate([wc_part(c_re.astype(F32)), wc_part(-c_im.astype(F32))], axis=1).astype(BF16)
    return tab, wb, wc, d_skip.reshape(nb, 1, gb * i)


_T_M1, _T_M2, _T_M4, _T_P = 0, 16, 32, 48


def _s5_seq_kernel(u_ref, wb_ref, wc_ref, tab_ref, d_ref, g_ref, st_ref, h_ref, carry_ref, *, tt):
    half = h_ref.shape[1] // 2
    t = pl.program_id(2)
    u = u_ref[...]
    h_ref[...] = jnp.dot(u.astype(BF16), wb_ref[...], preferred_element_type=F32)

    @pl.when(t == 0)
    def _():
        carry_ref[...] = jnp.zeros_like(carry_ref)

    def tab(off):
        return tab_ref[off:off + 8, :], tab_ref[off + 8:off + 16, :]

    def block(i, carry):
        cr, ci = carry
        r0 = pl.multiple_of(i * 8, 8)
        xr = h_ref[pl.ds(r0, 8), 0:half]
        xi = h_ref[pl.ds(r0, 8), half:2 * half]
        for k, off in ((1, _T_M1), (2, _T_M2), (4, _T_M4)):
            mr, mi = tab(off)
            sr, si = pltpu.roll(xr, k, axis=0), pltpu.roll(xi, k, axis=0)
            xr, xi = xr + (mr * sr - mi * si), xi + (mr * si + mi * sr)
        pr, pi = tab(_T_P)
        hr = xr + (pr * cr - pi * ci)
        hi = xi + (pr * ci + pi * cr)
        h_ref[pl.ds(r0, 8), 0:half] = hr
        h_ref[pl.ds(r0, 8), half:2 * half] = hi
        return jnp.broadcast_to(hr[7:8, :], hr.shape), jnp.broadcast_to(hi[7:8, :], hi.shape)

    cr, ci = lax.fori_loop(0, tt // 8, block, (carry_ref[:, 0:half], carry_ref[:, half:2 * half]), unroll=2)
    carry_ref[:, 0:half] = cr
    carry_ref[:, half:2 * half] = ci
    st_ref[:, 0:half] = cr
    st_ref[:, half:2 * half] = ci
    y = jnp.dot(h_ref[...].astype(BF16), wc_ref[...], preferred_element_type=F32) + d_ref[...] * u
    g_ref[...] = jax.nn.gelu(y)


def s5_seq(z, prep, *, batch, tt=512):
    tab, wb, wc, d = prep
    nb, cin, cst = wb.shape
    m_rows = z.shape[0]
    tt = min(tt, m_rows // batch)
    nt = m_rows // batch // tt
    return pl.pallas_call(
        functools.partial(_s5_seq_kernel, tt=tt),
        grid=(nb, batch, nt),
        in_specs=[pl.BlockSpec((tt, cin), lambda j, b, t: (b * nt + t, j)),
                  pl.BlockSpec((None, cin, cst), lambda j, b, t: (j, 0, 0)),
                  pl.BlockSpec((None, cst, cin), lambda j, b, t: (j, 0, 0)),
                  pl.BlockSpec((None, tab.shape[1], cst // 2), lambda j, b, t: (j, 0, 0)),
                  pl.BlockSpec((None, 1, cin), lambda j, b, t: (j, 0, 0))],
        out_specs=[pl.BlockSpec((tt, cin), lambda j, b, t: (b * nt + t, j)),
                   pl.BlockSpec((None, None, 8, cst), lambda j, b, t: (j, b, 0, 0))],
        out_shape=[jax.ShapeDtypeStruct((m_rows, nb * cin), F32),
                   jax.ShapeDtypeStruct((nb, batch, 8, cst), F32)],
        scratch_shapes=[pltpu.VMEM((tt, cst), F32), pltpu.VMEM((8, cst), F32)],
        compiler_params=_params("parallel", "parallel", "arbitrary"),
        name="s5_seq",
    )(z, wb, wc, tab, d)


def _s5_step_kernel(u_ref, s_ref, wb_ref, wc_ref, tab_ref, d_ref, g_ref, st_ref):
    half = s_ref.shape[1] // 2
    u = u_ref[...]
    bu = jnp.dot(u.astype(BF16), wb_ref[...], preferred_element_type=F32)
    ar, ai = tab_ref[_T_P:_T_P + 1, :], tab_ref[_T_P + 8:_T_P + 9, :]
    sr, si = s_ref[:, 0:half], s_ref[:, half:2 * half]
    hr = ar * sr - ai * si + bu[:, 0:half]
    hi = ar * si + ai * sr + bu[:, half:2 * half]
    st_ref[:, 0:half] = hr
    st_ref[:, half:2 * half] = hi
    h = jnp.concatenate([hr, hi], axis=1)
    y = jnp.dot(h.astype(BF16), wc_ref[...], preferred_element_type=F32) + d_ref[...] * u
    g_ref[...] = jax.nn.gelu(y)


def s5_step(z, state, prep):
    tab, wb, wc, d = prep
    nb, cin, cst = wb.shape
    rows = z.shape[0]
    return pl.pallas_call(
        _s5_step_kernel,
        grid=(nb,),
        in_specs=[pl.BlockSpec((rows, cin), lambda j: (0, j)),
                  pl.BlockSpec((None, rows, cst), lambda j: (j, 0, 0)),
                  pl.BlockSpec((None, cin, cst), lambda j: (j, 0, 0)),
                  pl.BlockSpec((None, cst, cin), lambda j: (j, 0, 0)),
                  pl.BlockSpec((None, tab.shape[1], cst // 2), lambda j: (j, 0, 0)),
                  pl.BlockSpec((None, 1, cin), lambda j: (j, 0, 0))],
        out_specs=[pl.BlockSpec((rows, cin), lambda j: (0, j)),
                   pl.BlockSpec((None, rows, cst), lambda j: (j, 0, 0))],
        out_shape=[jax.ShapeDtypeStruct((rows, nb * cin), F32),
                   jax.ShapeDtypeStruct((nb, rows, cst), F32)],
        compiler_params=_params("parallel"),
        name="s5_step",
    )(z, state, wb, wc, tab, d)


def _state_to_blocks(s_re, s_im):
    b, g, p = s_re.shape
    nb = g // GROUPS_PER_BLOCK

    def f(s):
        return s.reshape(b, nb, GROUPS_PER_BLOCK * p).transpose(1, 0, 2)

    return jnp.concatenate([f(s_re), f(s_im)], axis=2)


def _blocks_to_state(st, p=SSM_STATE):
    nb, b, c = st.shape
    half = c // 2

    def f(s):
        return s.transpose(1, 0, 2).reshape(b, nb * half // p, p)

    return f(st[:, :, :half]), f(st[:, :, half:])


def _attn_seq_kernel(q_ref, kp_ref, kc_ref, vp_ref, vc_ref, sk_ref, o_ref):
    hd, blk = HEAD_DIM, WINDOW
    n = pl.program_id(1)
    shape = (KV_GROUP * blk, 2 * blk)
    qi = lax.broadcasted_iota(jnp.int32, shape, 0) % blk
    kj = lax.broadcasted_iota(jnp.int32, shape, 1)
    vis = (kj >= qi) & (kj <= qi + WINDOW) & ((kj >= blk) | (n > 0))
    for h in range(N_KV_HEADS):
        c0 = h * KV_GROUP * hd
        qa = jnp.concatenate([q_ref[:, c0 + g * hd:c0 + (g + 1) * hd] for g in range(KV_GROUP)], axis=0)
        kb = jnp.concatenate([kp_ref[:, h * hd:(h + 1) * hd], kc_ref[:, h * hd:(h + 1) * hd]], axis=0)
        vb = jnp.concatenate([vp_ref[:, h * hd:(h + 1) * hd], vc_ref[:, h * hd:(h + 1) * hd]], axis=0)
        s = lax.dot_general(qa.astype(BF16), kb.astype(BF16), (((1,), (1,)), ((), ())),
                            preferred_element_type=F32) * (hd ** -0.5)
        s = jnp.where(vis, s, NEG_INF)
        sk = sk_ref[h]
        m = jnp.maximum(jnp.max(s, axis=-1, keepdims=True), sk)
        p = jnp.exp(s - m)
        w = p / (jnp.sum(p, axis=-1, keepdims=True) + jnp.exp(sk - m))
        o = jnp.dot(w.astype(BF16), vb.astype(BF16), preferred_element_type=F32)
        for g in range(KV_GROUP):
            o_ref[:, c0 + g * hd:c0 + (g + 1) * hd] = o[g * blk:(g + 1) * blk].astype(o_ref.dtype)


def attn_seq(qkv, sinks, *, batch, out_dtype):
    m_rows = qkv.shape[0]
    nblk = m_rows // batch // WINDOW
    nq = N_KV_HEADS * KV_GROUP
    qw, kvw = nq * HEAD_DIM, N_KV_HEADS * HEAD_DIM
    kblk, vblk = qw // kvw, qw // kvw + 1
    sk = jnp.repeat(sinks.astype(F32).reshape(N_KV_HEADS, KV_GROUP), WINDOW, axis=1)[..., None]

    def cur(c):
        return pl.BlockSpec((WINDOW, kvw), lambda b, n: (b * nblk + n, c))

    def prev(c):
        return pl.BlockSpec((WINDOW, kvw), lambda b, n: (b * nblk + jnp.maximum(n - 1, 0), c))

    return pl.pallas_call(
        _attn_seq_kernel,
        grid=(batch, nblk),
        in_specs=[pl.BlockSpec((WINDOW, qw), lambda b, n: (b * nblk + n, 0)),
                  prev(kblk), cur(kblk), prev(vblk), cur(vblk),
                  pl.BlockSpec((N_KV_HEADS, KV_GROUP * WINDOW, 1), lambda b, n: (0, 0, 0))],
        out_specs=pl.BlockSpec((WINDOW, qw), lambda b, n: (b * nblk + n, 0)),
        out_shape=jax.ShapeDtypeStruct((m_rows, qw), out_dtype),
        compiler_params=_params("parallel", "parallel"),
        name="attn_seq",
    )(qkv, qkv, qkv, qkv, qkv, sk)


def _attn_step_kernel(q_ref, kn_ref, vn_ref, kc_ref, vc_ref, sk_ref, o_ref):
    hd = HEAD_DIM
    scale = hd ** -0.5
    for h in range(N_KV_HEADS):
        rows = slice(h * KV_GROUP, (h + 1) * KV_GROUP)
        qh = q_ref[rows, :].astype(BF16)
        kh = kc_ref[:, h * hd:(h + 1) * hd].astype(BF16)
        vh = vc_ref[:, h * hd:(h + 1) * hd].astype(BF16)
        kn = kn_ref[h:h + 1, :].astype(BF16).astype(F32)
        vn = vn_ref[h:h + 1, :].astype(BF16).astype(F32)
        sc = lax.dot_general(qh, kh, (((1,), (1,)), ((), ())), preferred_element_type=F32) * scale
        sn = jnp.sum(qh.astype(F32) * kn, axis=-1, keepdims=True) * scale
        sk = sk_ref[rows, :]
        m = jnp.maximum(jnp.maximum(jnp.max(sc, axis=-1, keepdims=True), sn), sk)
        pc, pn = jnp.exp(sc - m), jnp.exp(sn - m)
        den = jnp.sum(pc, axis=-1, keepdims=True) + pn + jnp.exp(sk - m)
        o = jnp.dot((pc / den).astype(BF16), vh, preferred_element_type=F32)
        o = o + (pn / den).astype(BF16).astype(F32) * vn
        o_ref[rows, :] = o.astype(o_ref.dtype)


def attn_step(qkv, k_buf, v_buf, sinks, *, out_dtype):
    rows = qkv.shape[0]
    nq = N_KV_HEADS * KV_GROUP
    wb = k_buf.shape[1]
    q3 = qkv[:, :nq * HEAD_DIM].reshape(rows, nq, HEAD_DIM)
    kn = qkv[:, nq * HEAD_DIM:(nq + N_KV_HEADS) * HEAD_DIM].reshape(rows, N_KV_HEADS, HEAD_DIM)
    vn = qkv[:, (nq + N_KV_HEADS) * HEAD_DIM:].reshape(rows, N_KV_HEADS, HEAD_DIM)
    kvw = N_KV_HEADS * HEAD_DIM
    o = pl.pallas_call(
        _attn_step_kernel,
        grid=(rows,),
        in_specs=[pl.BlockSpec((None, nq, HEAD_DIM), lambda b: (b, 0, 0)),
                  pl.BlockSpec((None, N_KV_HEADS, HEAD_DIM), lambda b: (b, 0, 0)),
                  pl.BlockSpec((None, N_KV_HEADS, HEAD_DIM), lambda b: (b, 0, 0)),
                  pl.BlockSpec((None, wb, kvw), lambda b: (b, 0, 0)),
                  pl.BlockSpec((None, wb, kvw), lambda b: (b, 0, 0)),
                  pl.BlockSpec((nq, 1), lambda b: (0, 0))],
        out_specs=pl.BlockSpec((None, nq, HEAD_DIM), lambda b: (b, 0, 0)),
        out_shape=jax.ShapeDtypeStruct((rows, nq, HEAD_DIM), out_dtype),
        compiler_params=_params("parallel"),
        name="attn_step",
    )(q3, kn, vn, k_buf.reshape(rows, wb, kvw), v_buf.reshape(rows, wb, kvw),
      sinks.astype(F32).reshape(nq, 1))
    new_k = jnp.concatenate([k_buf, kn[:, None]], axis=1)[:, 1:]
    new_v = jnp.concatenate([v_buf, vn[:, None]], axis=1)[:, 1:]
    return o.reshape(rows, nq * HEAD_DIM), new_k, new_v


def kernel(x_prompt, x_sample, state_ssm_re, state_ssm_im, state_sconv, cache_k, cache_v, state_ffn_conv, norm_mix_g, norm_ffn_g, norm_final_g, w_in_even, ssm_lambda_re, ssm_lambda_im, ssm_log_dt, ssm_b_re, ssm_b_im, ssm_c_re, ssm_c_im, ssm_d, w_glu, b_glu, sconv_w, w_out_even, w_qkv, b_qkv, attn_sinks, w_o, b_o, w_ffn_gate, w_ffn_up, ffn_conv_w, w_ffn_down):
    bp, lp, d_model = x_prompt.shape
    bs = x_sample.shape[0]
    d_ssm, d_conv = ssm_d.shape[1], sconv_w.shape[2]
    nq, nkv = N_KV_HEADS * KV_GROUP * HEAD_DIM, N_KV_HEADS * HEAD_DIM
    keep = min(WINDOW, lp)
    w_down_bf16 = w_ffn_down.astype(BF16)
    xp, xs = x_prompt.reshape(bp * lp, d_model), x_sample.reshape(bs, d_model)
    p_out = [[] for _ in range(6)]
    s_out = [[] for _ in range(6)]
    for l in range(DEPTH):
        j = l // 2
        hp, hs = rmsnorm(xp, norm_mix_g[l], BF16), rmsnorm(xs, norm_mix_g[l], BF16)
        if l % 2 == 0:
            prep = s5_prepare(ssm_lambda_re[j], ssm_lambda_im[j], ssm_log_dt[j], ssm_b_re[j], ssm_b_im[j],
                              ssm_c_re[j], ssm_c_im[j], ssm_d[j])
            zp, zs = matmul_dual(hp, hs, w_in_even, j)
            cv, cb, cc = d_ssm, d_ssm + d_conv, d_ssm + 2 * d_conv
            gp, stp = s5_seq(zp, prep, batch=bp)
            gs, sts = s5_step(zs, _state_to_blocks(state_ssm_re[j], state_ssm_im[j]), prep)
            for out, st in ((p_out, stp[:, :, 0, :]), (s_out, sts)):
                re, im = _blocks_to_state(st)
                out[0].append(re)
                out[1].append(im)
            bp_out, scp = conv_seq((zp, cc), (zp, cv), (zp, cb), sconv_w[j], batch=bp, ncols=d_conv,
                                   act="none", out_dtype=BF16)
            bs_out, scs = conv_step((zs, cc), (zs, cv), (zs, cb), sconv_w[j], state_sconv[j], ncols=d_conv,
                                    act="none", out_dtype=BF16)
            p_out[2].append(scp[:, 8 - (CONV_W - 1):, :])
            s_out[2].append(scs)
            ap, a_s = matmul_dual(gp, gs, w_glu, j, bias=b_glu, res=(gp, gs), mode="glu", out_dtype=BF16)
            xp, xs = matmul_dual(jnp.concatenate([ap, bp_out], axis=1), jnp.concatenate([a_s, bs_out], axis=1),
                                 w_out_even, j, res=(xp, xs), mode="res")
        else:
            qkvp, qkvs = matmul_dual(hp, hs, w_qkv, j, bias=b_qkv)
            op = attn_seq(qkvp, attn_sinks[j], batch=bp, out_dtype=BF16)
            kv = qkvp.reshape(bp, lp, -1)[:, lp - keep:, nq:]
            p_out[3].append(kv[..., :nkv].reshape(bp, keep, N_KV_HEADS, HEAD_DIM))
            p_out[4].append(kv[..., nkv:].reshape(bp, keep, N_KV_HEADS, HEAD_DIM))
            o_s, kk, vv = attn_step(qkvs, cache_k[j], cache_v[j], attn_sinks[j], out_dtype=BF16)
            s_out[3].append(kk)
            s_out[4].append(vv)
            xp, xs = matmul_dual(op, o_s, w_o, j, bias=b_o, res=(xp, xs), mode="res")
        hp, hs = rmsnorm(xp, norm_ffn_g[l], BF16), rmsnorm(xs, norm_ffn_g[l], BF16)
        ap, fcp, a_s, fcs = ffn_gate_up(hp, hs, w_ffn_gate, w_ffn_up, ffn_conv_w, state_ffn_conv, l, batch=bp)
        p_out[5].append(fcp[:, 8 - (CONV_W - 1):, :])
        s_out[5].append(fcs)
        xp = matmul(ap, w_down_bf16, l, res=xp, mode="res", tn=512, tk=w_down_bf16.shape[1] // 2)
        xs = matmul(a_s, w_down_bf16, l, res=xs, mode="res", tn=512, tk=w_down_bf16.shape[1] // 2)
    yp, ys = rmsnorm(xp, norm_final_g, F32), rmsnorm(xs, norm_final_g, F32)
    return (yp.reshape(bp, lp, d_model), ys.reshape(bs, 1, d_model), *[jnp.stack(t) for t in p_out],
            *[jnp.stack(t) for t in s_out])
```

```python
import functools

import jax
import jax.numpy as jnp
from jax import lax
from jax.experimental import pallas as pl
from jax.experimental.pallas import tpu as pltpu

F32 = jnp.float32
BF16 = jnp.bfloat16

DEPTH = 4
SSM_GROUP = 16
SSM_STATE = 64
GROUPS_PER_BLOCK = 8
S5_SUBSEQ = 8
CONV_W = 3
HEAD_DIM = 128
N_KV_HEADS = 8
KV_GROUP = 4
WINDOW = 128
RMS_EPS = 1e-5
NEG_INF = -1e30

VMEM_LIMIT_BYTES = 60 * 1024 * 1024


def _params(*sem):
    return pltpu.CompilerParams(dimension_semantics=sem, vmem_limit_bytes=VMEM_LIMIT_BYTES)


def _rmsnorm_kernel(x_ref, g_ref, o_ref):
    x = x_ref[...]
    y = x * lax.rsqrt(jnp.mean(x * x, axis=-1, keepdims=True) + RMS_EPS)
    o_ref[...] = (y * g_ref[...]).astype(o_ref.dtype)


def rmsnorm(x, g, out_dtype):
    m, d = x.shape
    tm = min(m, 256)
    return pl.pallas_call(
        _rmsnorm_kernel,
        grid=(m // tm,),
        in_specs=[pl.BlockSpec((tm, d), lambda i: (i, 0)), pl.BlockSpec((1, d), lambda i: (0, 0))],
        out_specs=pl.BlockSpec((tm, d), lambda i: (i, 0)),
        out_shape=jax.ShapeDtypeStruct((m, d), out_dtype),
        compiler_params=_params("parallel"),
        name="rmsnorm",
    )(x, g.reshape(1, d))


def _mm_dual_kernel(*refs, has_bias, mode, cast):
    it = iter(refs)
    xp_ref, xs_ref, w_ref = next(it), next(it), next(it)
    b_ref = next(it) if has_bias else None
    rp_ref, rs_ref = (next(it), next(it)) if mode != "plain" else (None, None)
    op_ref, os_ref = next(it), next(it)
    wbf_ref = next(it) if cast else w_ref

    def run(x_ref, r_ref, o_ref):
        acc = jnp.dot(x_ref[...].astype(BF16), wbf_ref[...], preferred_element_type=F32)
        if has_bias:
            acc = acc + b_ref[...]
        if mode == "res":
            acc = r_ref[...] + acc
        elif mode == "glu":
            acc = r_ref[...] * jax.nn.sigmoid(acc)
        o_ref[...] = acc.astype(o_ref.dtype)

    @pl.when(pl.program_id(1) == 0)
    def _():
        if cast:
            wbf_ref[...] = w_ref[...].astype(BF16)
        run(xs_ref, rs_ref, os_ref)

    run(xp_ref, rp_ref, op_ref)


def matmul_dual(xp, xs, w, layer, *, bias=None, res=None, mode="plain", out_dtype=F32, tm=1024, tn=512):
    mp, kdim = xp.shape
    ms = xs.shape[0]
    n = w.shape[2]
    tm, tn = min(tm, mp), min(tn, n)
    cast = w.dtype != BF16
    in_specs = [pl.BlockSpec((tm, kdim), lambda j, i: (i, 0)),
                pl.BlockSpec((ms, kdim), lambda j, i: (0, 0)),
                pl.BlockSpec((None, kdim, tn), lambda j, i: (layer, 0, j))]
    args = [xp, xs, w]
    if bias is not None:
        in_specs.append(pl.BlockSpec((None, 1, tn), lambda j, i: (layer, 0, j)))
        args.append(bias.reshape(bias.shape[0], 1, n))
    if mode != "plain":
        in_specs += [pl.BlockSpec((tm, tn), lambda j, i: (i, j)), pl.BlockSpec((ms, tn), lambda j, i: (0, j))]
        args += list(res)
    return pl.pallas_call(
        functools.partial(_mm_dual_kernel, has_bias=bias is not None, mode=mode, cast=cast),
        grid=(n // tn, mp // tm),
        in_specs=in_specs,
        out_specs=[pl.BlockSpec((tm, tn), lambda j, i: (i, j)), pl.BlockSpec((ms, tn), lambda j, i: (0, j))],
        out_shape=[jax.ShapeDtypeStruct((mp, n), out_dtype), jax.ShapeDtypeStruct((ms, n), out_dtype)],
        scratch_shapes=[pltpu.VMEM((kdim, tn), BF16)] if cast else [],
        compiler_params=_params("parallel", "arbitrary"),
        name="matmul_dual_" + mode,
    )(*args)


def _conv3_rows(g, w_ref, carry_ref):
    c1 = carry_ref[7:8, :]
    c2 = carry_ref[6:7, :]
    row = lax.broadcasted_iota(jnp.int32, g.shape, 0)
    g1 = jnp.where(row == 0, c1, pltpu.roll(g, 1, axis=0))
    g2 = jnp.where(row == 0, c2, jnp.where(row == 1, c1, pltpu.roll(g, 2, axis=0)))
    carry_ref[...] = g[g.shape[0] - 8:, :]
    return w_ref[0:1, :] * g2 + w_ref[1:2, :] * g1 + w_ref[2:3, :] * g


def _ffn_gu_kernel(xp_ref, xs_ref, wg_ref, wu_ref, cw_ref, prev_ref, ap_ref, stp_ref, as_ref, sts_ref,
                   wgb_ref, wub_ref, carry_ref, *, tiles_per_seq, nchunk):
    i = pl.program_id(1)

    @pl.when(i == 0)
    def _():
        wgb_ref[...] = wg_ref[...].astype(BF16)
        wub_ref[...] = wu_ref[...].astype(BF16)
        xs = xs_ref[...]
        gs = jnp.dot(xs, wgb_ref[...], preferred_element_type=F32)
        us = jnp.dot(xs, wub_ref[...], preferred_element_type=F32)
        prev = prev_ref[...]
        x2, x1 = prev[:, 0, :], prev[:, 1, :]
        y = cw_ref[0:1, :] * x2 + cw_ref[1:2, :] * x1 + cw_ref[2:3, :] * gs
        as_ref[...] = (jax.nn.silu(y) * us).astype(as_ref.dtype)
        sts_ref[:, 0, :] = x1
        sts_ref[:, 1, :] = gs

    @pl.when(i % tiles_per_seq == 0)
    def _():
        carry_ref[...] = jnp.zeros_like(carry_ref)

    rc = xp_ref.shape[0] // nchunk
    for c in range(nchunk):
        x = xp_ref[c * rc:(c + 1) * rc, :]
        g = jnp.dot(x, wgb_ref[...], preferred_element_type=F32)
        u = jnp.dot(x, wub_ref[...], preferred_element_type=F32)
        y = _conv3_rows(g, cw_ref, carry_ref)
        ap_ref[c * rc:(c + 1) * rc, :] = (jax.nn.silu(y) * u).astype(ap_ref.dtype)

    @pl.when(i % tiles_per_seq == tiles_per_seq - 1)
    def _():
        stp_ref[...] = carry_ref[...]


def ffn_gate_up(xp, xs, wg, wu, cw, prev_s, layer, *, batch, tm=2048, tn=256, nchunk=4):
    mp, d = xp.shape
    ms = xs.shape[0]
    f = wg.shape[2]
    tm = min(tm, mp // batch)
    tiles_per_seq = mp // batch // tm
    return pl.pallas_call(
        functools.partial(_ffn_gu_kernel, tiles_per_seq=tiles_per_seq, nchunk=nchunk),
        grid=(f // tn, mp // tm),
        in_specs=[pl.BlockSpec((tm, d), lambda j, i: (i, 0)),
                  pl.BlockSpec((ms, d), lambda j, i: (0, 0)),
                  pl.BlockSpec((None, d, tn), lambda j, i: (layer, 0, j)),
                  pl.BlockSpec((None, d, tn), lambda j, i: (layer, 0, j)),
                  pl.BlockSpec((None, CONV_W, tn), lambda j, i: (layer, 0, j)),
                  pl.BlockSpec((None, ms, CONV_W - 1, tn), lambda j, i: (layer, 0, 0, j))],
        out_specs=[pl.BlockSpec((tm, tn), lambda j, i: (i, j)),
                   pl.BlockSpec((None, 8, tn), lambda j, i: (i // tiles_per_seq, 0, j)),
                   pl.BlockSpec((ms, tn), lambda j, i: (0, j)),
                   pl.BlockSpec((ms, CONV_W - 1, tn), lambda j, i: (0, 0, j))],
        out_shape=[jax.ShapeDtypeStruct((mp, f), BF16),
                   jax.ShapeDtypeStruct((batch, 8, f), F32),
                   jax.ShapeDtypeStruct((ms, f), BF16),
                   jax.ShapeDtypeStruct((ms, CONV_W - 1, f), F32)],
        scratch_shapes=[pltpu.VMEM((d, tn), BF16), pltpu.VMEM((d, tn), BF16), pltpu.VMEM((8, tn), F32)],
        compiler_params=_params("parallel", "arbitrary"),
        name="ffn_gate_up",
    )(xp, xs, wg, wu, cw, prev_s)


def _conv_seq_kernel(*refs, mul_in, act):
    it = iter(refs)
    p_ref = next(it)
    q_ref = next(it) if mul_in else None
    r_ref, w_ref, o_ref, st_ref, carry_ref = next(it), next(it), next(it), next(it), next(it)
    m = p_ref[...] * q_ref[...] if mul_in else p_ref[...]

    @pl.when(pl.program_id(2) == 0)
    def _():
        carry_ref[...] = jnp.zeros_like(carry_ref)

    y = _conv3_rows(m, w_ref, carry_ref)
    if act == "silu":
        y = jax.nn.silu(y)
    o_ref[...] = (r_ref[...] * y).astype(o_ref.dtype)
    st_ref[...] = carry_ref[...]


def conv_seq(p, q, r, w, *, batch, ncols, act, out_dtype, tt=512, tc=512):
    mul_in = q is not None
    m_rows = p[0].shape[0]
    tt = min(tt, m_rows // batch)
    nt = m_rows // batch // tt
    nc = ncols // tc

    def spec(c0):
        b0 = c0 // tc
        return pl.BlockSpec((tt, tc), lambda j, b, t: (b * nt + t, b0 + j))

    in_specs, args = [spec(p[1])], [p[0]]
    if mul_in:
        in_specs.append(spec(q[1]))
        args.append(q[0])
    in_specs += [spec(r[1]), pl.BlockSpec((CONV_W, tc), lambda j, b, t: (0, j))]
    args += [r[0], w]
    return pl.pallas_call(
        functools.partial(_conv_seq_kernel, mul_in=mul_in, act=act),
        grid=(nc, batch, nt),
        in_specs=in_specs,
        out_specs=[pl.BlockSpec((tt, tc), lambda j, b, t: (b * nt + t, j)),
                   pl.BlockSpec((None, 8, tc), lambda j, b, t: (b, 0, j))],
        out_shape=[jax.ShapeDtypeStruct((m_rows, ncols), out_dtype),
                   jax.ShapeDtypeStruct((batch, 8, ncols), F32)],
        scratch_shapes=[pltpu.VMEM((8, tc), F32)],
        compiler_params=_params("parallel", "parallel", "arbitrary"),
        name="conv_seq_" + act,
    )(*args)


def _conv_step_kernel(*refs, mul_in, act):
    it = iter(refs)
    p_ref = next(it)
    q_ref = next(it) if mul_in else None
    r_ref, w_ref, prev_ref, o_ref, st_ref = next(it), next(it), next(it), next(it), next(it)
    m = p_ref[...] * q_ref[...] if mul_in else p_ref[...]
    prev = prev_ref[...]
    x2, x1 = prev[:, 0, :], prev[:, 1, :]
    y = w_ref[0:1, :] * x2 + w_ref[1:2, :] * x1 + w_ref[2:3, :] * m
    if act == "silu":
        y = jax.nn.silu(y)
    o_ref[...] = (r_ref[...] * y).astype(o_ref.dtype)
    st_ref[:, 0, :] = x1
    st_ref[:, 1, :] = m


def conv_step(p, q, r, w, prev, *, ncols, act, out_dtype, tc=512):
    mul_in = q is not None
    rows = p[0].shape[0]
    nc = ncols // tc

    def spec(c0):
        b0 = c0 // tc
        return pl.BlockSpec((rows, tc), lambda j: (0, b0 + j))

    in_specs, args = [spec(p[1])], [p[0]]
    if mul_in:
        in_specs.append(spec(q[1]))
        args.append(q[0])
    in_specs += [spec(r[1]), pl.BlockSpec((CONV_W, tc), lambda j: (0, j)),
                 pl.BlockSpec((rows, CONV_W - 1, tc), lambda j: (0, 0, j))]
    args += [r[0], w, prev]
    return pl.pallas_call(
        functools.partial(_conv_step_kernel, mul_in=mul_in, act=act),
        grid=(nc,),
        in_specs=in_specs,
        out_specs=[pl.BlockSpec((rows, tc), lambda j: (0, j)),
                   pl.BlockSpec((rows, CONV_W - 1, tc), lambda j: (0, 0, j))],
        out_shape=[jax.ShapeDtypeStruct((rows, ncols), out_dtype),
                   jax.ShapeDtypeStruct((rows, CONV_W - 1, ncols), F32)],
        compiler_params=_params("parallel"),
        name="conv_step_" + act,
    )(*args)


def _s5_prep_kernel(lr_ref, li_ref, ldt_ref, br_ref, bi_ref, pow_ref, bb_ref):
    lr, li = lr_ref[...], li_ref[...]
    dt = jnp.exp(ldt_ref[...])
    mag = jnp.exp(lr * dt)
    ar, ai = mag * jnp.cos(li * dt), mag * jnp.sin(li * dt)
    den = lr * lr + li * li
    cr = ((ar - 1.0) * lr + ai * li) / den
    ci = (ai * lr - (ar - 1.0) * li) / den
    br, bi = br_ref[...], bi_ref[...]
    bb_ref[:, 0:SSM_GROUP, :] = cr * br - ci * bi
    bb_ref[:, SSM_GROUP:2 * SSM_GROUP, :] = cr * bi + ci * br
    pr, pi = ar, ai
    for k in range(8):
        pow_ref[:, k:k + 1, :] = pr
        pow_ref[:, 8 + k:9 + k, :] = pi
        pr, pi = pr * ar - pi * ai, pr * ai + pi * ar


def s5_prepare(lam_re, lam_im, log_dt, b_re, b_im, c_re, c_im, d_skip):
    g, p = lam_re.shape
    i = SSM_GROUP
    nb, gb = g // GROUPS_PER_BLOCK, GROUPS_PER_BLOCK
    pw, bb = pl.pallas_call(
        _s5_prep_kernel,
        out_shape=[jax.ShapeDtypeStruct((g, 16, p), F32), jax.ShapeDtypeStruct((g, 2 * i, p), F32)],
        name="s5_prep",
    )(lam_re.reshape(g, 1, p), lam_im.reshape(g, 1, p), log_dt.reshape(g, 1, 1),
      b_re.transpose(0, 2, 1), b_im.transpose(0, 2, 1))

    def lanes(a):
        return a.reshape(nb, gb, 8, p).transpose(0, 2, 1, 3).reshape(nb, 8, gb * p)

    pr, pi = lanes(pw[:, 0:8]), lanes(pw[:, 8:16])

    def bcast(a, k):
        return jnp.broadcast_to(a[:, k - 1:k, :], a.shape)

    tab = jnp.concatenate([pr, pi, bcast(pr, 1), bcast(pi, 1), bcast(pr, 8), bcast(pi, 8)], axis=1)
    eye = jnp.eye(gb, dtype=F32)

    def wb_part(a):
        a = a.reshape(nb, gb, i, p)
        return jnp.einsum('jaip,ab->jaibp', a, eye).reshape(nb, gb * i, gb * p)

    def wc_part(a):
        a = a.reshape(nb, gb, i, p)
        return jnp.einsum('jaip,ab->jbpai', a, eye).reshape(nb, gb * p, gb * i)

    wb = jnp.concatenate([wb_part(bb[:, :i]), wb_part(bb[:, i:])], axis=2).astype(BF16)
    wc = jnp.concatenate([wc_part(c_re.astype(F32)), wc_part(-c_im.astype(F32))], axis=1).astype(BF16)
    return tab, wb, wc, d_skip.reshape(nb, 1, gb * i)


_T_P, _T_A1, _T_A8 = 0, 16, 32


def _s5_seq_kernel(u_ref, wb_ref, wc_ref, tab_ref, d_ref, g_ref, st_ref, h_ref, pw_ref, *, chunk):
    seq, width = h_ref.shape
    half, lj = width // 2, seq // S5_SUBSEQ
    re, im = slice(0, half), slice(half, width)

    def tab(off):
        return tab_ref[off:off + 8, :], tab_ref[off + 8:off + 16, :]

    @pl.when(pl.program_id(1) == 0)
    def _():
        a8r, a8i = tab(_T_A8)

        def fill(m, c):
            pr, pi = c
            r0 = pl.multiple_of(m * 8, 8)
            pw_ref[pl.ds(r0, 8), re] = pr
            pw_ref[pl.ds(r0, 8), im] = pi
            return pr * a8r - pi * a8i, pr * a8i + pi * a8r

        lax.fori_loop(0, lj // 8, fill, tab(_T_P))

    for c in range(seq // chunk):
        rows = slice(c * chunk, (c + 1) * chunk)
        h_ref[rows, :] = jnp.dot(u_ref[rows, :].astype(BF16), wb_ref[...], preferred_element_type=F32)

    a1r, a1i = tab(_T_A1)

    def pass1(j, c):
        hr, hi = c
        r0 = pl.multiple_of(j * 8, 8)
        hr, hi = (a1r * hr - a1i * hi) + h_ref[pl.ds(r0, 8), re], (a1r * hi + a1i * hr) + h_ref[pl.ds(r0, 8), im]
        h_ref[pl.ds(r0, 8), re] = hr
        h_ref[pl.ds(r0, 8), im] = hi
        return hr, hi

    zero = jnp.zeros((S5_SUBSEQ, half), F32)
    xr, xi = lax.fori_loop(0, lj, pass1, (zero, zero), unroll=4)

    row = lax.broadcasted_iota(jnp.int32, (S5_SUBSEQ, half), 0)
    qr, qi = pw_ref[lj - 1:lj, re], pw_ref[lj - 1:lj, im]
    for k in (1, 2, 4):
        sr = jnp.where(row >= k, pltpu.roll(xr, k, axis=0), 0.0)
        si = jnp.where(row >= k, pltpu.roll(xi, k, axis=0), 0.0)
        xr, xi = xr + (qr * sr - qi * si), xi + (qr * si + qi * sr)
        qr, qi = qr * qr - qi * qi, 2.0 * (qr * qi)
    st_ref[:, re] = jnp.broadcast_to(xr[S5_SUBSEQ - 1:, :], xr.shape)
    st_ref[:, im] = jnp.broadcast_to(xi[S5_SUBSEQ - 1:, :], xi.shape)
    er = jnp.where(row >= 1, pltpu.roll(xr, 1, axis=0), 0.0)
    ei = jnp.where(row >= 1, pltpu.roll(xi, 1, axis=0), 0.0)

    def pass2(j, _):
        r0 = pl.multiple_of(j * 8, 8)
        pr = jnp.broadcast_to(pw_ref[pl.ds(j, 1), re], er.shape)
        pi = jnp.broadcast_to(pw_ref[pl.ds(j, 1), im], er.shape)
        h_ref[pl.ds(r0, 8), re] += pr * er - pi * ei
        h_ref[pl.ds(r0, 8), im] += pr * ei + pi * er
        return 0

    lax.fori_loop(0, lj, pass2, 0, unroll=4)
    for c in range(seq // chunk):
        rows = slice(c * chunk, (c + 1) * chunk)
        y = jnp.dot(h_ref[rows, :].astype(BF16), wc_ref[...], preferred_element_type=F32)
        g_ref[rows, :] = jax.nn.gelu(y + d_ref[...] * u_ref[rows, :])


def s5_seq(z, prep, *, batch):
    tab, wb, wc, d = prep
    nb, cin, cst = wb.shape
    m_rows = z.shape[0]
    seq = m_rows // batch
    lj = seq // S5_SUBSEQ

    def to_kernel_order(a):
        return a.reshape(batch, S5_SUBSEQ, lj, -1).transpose(0, 2, 1, 3).reshape(m_rows, -1)

    def to_time_order(a):
        return a.reshape(batch, lj, S5_SUBSEQ, -1).transpose(0, 2, 1, 3).reshape(m_rows, -1)

    g, st = pl.pallas_call(
        functools.partial(_s5_seq_kernel, chunk=min(seq, 512)),
        grid=(nb, batch),
        in_specs=[pl.BlockSpec((seq, cin), lambda j, b: (b, j)),
                  pl.BlockSpec((None, cin, cst), lambda j, b: (j, 0, 0)),
                  pl.BlockSpec((None, cst, cin), lambda j, b: (j, 0, 0)),
                  pl.BlockSpec((None, tab.shape[1], cst // 2), lambda j, b: (j, 0, 0)),
                  pl.BlockSpec((None, 1, cin), lambda j, b: (j, 0, 0))],
        out_specs=[pl.BlockSpec((seq, cin), lambda j, b: (b, j)),
                   pl.BlockSpec((None, None, 8, cst), lambda j, b: (j, b, 0, 0))],
        out_shape=[jax.ShapeDtypeStruct((m_rows, nb * cin), F32),
                   jax.ShapeDtypeStruct((nb, batch, 8, cst), F32)],
        scratch_shapes=[pltpu.VMEM((seq, cst), F32), pltpu.VMEM((lj, cst), F32)],
        compiler_params=_params("parallel", "arbitrary"),
        name="s5_seq",
    )(to_kernel_order(z[:, :nb * cin]), wb, wc, tab, d)
    return to_time_order(g), st


def _s5_step_kernel(u_ref, s_ref, wb_ref, wc_ref, tab_ref, d_ref, g_ref, st_ref):
    half = s_ref.shape[1] // 2
    u = u_ref[...]
    bu = jnp.dot(u.astype(BF16), wb_ref[...], preferred_element_type=F32)
    ar, ai = tab_ref[_T_P:_T_P + 1, :], tab_ref[_T_P + 8:_T_P + 9, :]
    sr, si = s_ref[:, 0:half], s_ref[:, half:2 * half]
    hr = ar * sr - ai * si + bu[:, 0:half]
    hi = ar * si + ai * sr + bu[:, half:2 * half]
    st_ref[:, 0:half] = hr
    st_ref[:, half:2 * half] = hi
    h = jnp.concatenate([hr, hi], axis=1)
    y = jnp.dot(h.astype(BF16), wc_ref[...], preferred_element_type=F32) + d_ref[...] * u
    g_ref[...] = jax.nn.gelu(y)


def s5_step(z, state, prep):
    tab, wb, wc, d = prep
    nb, cin, cst = wb.shape
    rows = z.shape[0]
    return pl.pallas_call(
        _s5_step_kernel,
        grid=(nb,),
        in_specs=[pl.BlockSpec((rows, cin), lambda j: (0, j)),
                  pl.BlockSpec((None, rows, cst), lambda j: (j, 0, 0)),
                  pl.BlockSpec((None, cin, cst), lambda j: (j, 0, 0)),
                  pl.BlockSpec((None, cst, cin), lambda j: (j, 0, 0)),
                  pl.BlockSpec((None, tab.shape[1], cst // 2), lambda j: (j, 0, 0)),
                  pl.BlockSpec((None, 1, cin), lambda j: (j, 0, 0))],
        out_specs=[pl.BlockSpec((rows, cin), lambda j: (0, j)),
                   pl.BlockSpec((None, rows, cst), lambda j: (j, 0, 0))],
        out_shape=[jax.ShapeDtypeStruct((rows, nb * cin), F32),
                   jax.ShapeDtypeStruct((nb, rows, cst), F32)],
        compiler_params=_params("parallel"),
        name="s5_step",
    )(z, state, wb, wc, tab, d)


def _state_to_blocks(s_re, s_im):
    b, g, p = s_re.shape
    nb = g // GROUPS_PER_BLOCK

    def f(s):
        return s.reshape(b, nb, GROUPS_PER_BLOCK * p).transpose(1, 0, 2)

    return jnp.concatenate([f(s_re), f(s_im)], axis=2)


def _blocks_to_state(st, p=SSM_STATE):
    nb, b, c = st.shape
    half = c // 2

    def f(s):
        return s.transpose(1, 0, 2).reshape(b, nb * half // p, p)

    return f(st[:, :, :half]), f(st[:, :, half:])


def _attn_seq_kernel(q_ref, kp_ref, kc_ref, vp_ref, vc_ref, sk_ref, o_ref):
    hd, blk = HEAD_DIM, WINDOW
    n = pl.program_id(1)
    shape = (KV_GROUP * blk, 2 * blk)
    qi = lax.broadcasted_iota(jnp.int32, shape, 0) % blk
    kj = lax.broadcasted_iota(jnp.int32, shape, 1)
    vis = (kj >= qi) & (kj <= qi + WINDOW) & ((kj >= blk) | (n > 0))
    for h in range(N_KV_HEADS):
        c0 = h * KV_GROUP * hd
        qa = jnp.concatenate([q_ref[:, c0 + g * hd:c0 + (g + 1) * hd] for g in range(KV_GROUP)], axis=0)
        kb = jnp.concatenate([kp_ref[:, h * hd:(h + 1) * hd], kc_ref[:, h * hd:(h + 1) * hd]], axis=0)
        vb = jnp.concatenate([vp_ref[:, h * hd:(h + 1) * hd], vc_ref[:, h * hd:(h + 1) * hd]], axis=0)
        s = lax.dot_general(qa.astype(BF16), kb.astype(BF16), (((1,), (1,)), ((), ())),
                            preferred_element_type=F32) * (hd ** -0.5)
        s = jnp.where(vis, s, NEG_INF)
        sk = sk_ref[h]
        m = jnp.maximum(jnp.max(s, axis=-1, keepdims=True), sk)
        p = jnp.exp(s - m)
        w = p / (jnp.sum(p, axis=-1, keepdims=True) + jnp.exp(sk - m))
        o = jnp.dot(w.astype(BF16), vb.astype(BF16), preferred_element_type=F32)
        for g in range(KV_GROUP):
            o_ref[:, c0 + g * hd:c0 + (g + 1) * hd] = o[g * blk:(g + 1) * blk].astype(o_ref.dtype)


def attn_seq(qkv, sinks, *, batch, out_dtype):
    m_rows = qkv.shape[0]
    nblk = m_rows // batch // WINDOW
    nq = N_KV_HEADS * KV_GROUP
    qw, kvw = nq * HEAD_DIM, N_KV_HEADS * HEAD_DIM
    kblk, vblk = qw // kvw, qw // kvw + 1
    sk = jnp.repeat(sinks.astype(F32).reshape(N_KV_HEADS, KV_GROUP), WINDOW, axis=1)[..., None]

    def cur(c):
        return pl.BlockSpec((WINDOW, kvw), lambda b, n: (b * nblk + n, c))

    def prev(c):
        return pl.BlockSpec((WINDOW, kvw), lambda b, n: (b * nblk + jnp.maximum(n - 1, 0), c))

    return pl.pallas_call(
        _attn_seq_kernel,
        grid=(batch, nblk),
        in_specs=[pl.BlockSpec((WINDOW, qw), lambda b, n: (b * nblk + n, 0)),
                  prev(kblk), cur(kblk), prev(vblk), cur(vblk),
                  pl.BlockSpec((N_KV_HEADS, KV_GROUP * WINDOW, 1), lambda b, n: (0, 0, 0))],
        out_specs=pl.BlockSpec((WINDOW, qw), lambda b, n: (b * nblk + n, 0)),
        out_shape=jax.ShapeDtypeStruct((m_rows, qw), out_dtype),
        compiler_params=_params("parallel", "parallel"),
        name="attn_seq",
    )(qkv, qkv, qkv, qkv, qkv, sk)


def _attn_step_kernel(q_ref, kn_ref, vn_ref, kc_ref, vc_ref, sk_ref, o_ref):
    hd = HEAD_DIM
    scale = hd ** -0.5
    for h in range(N_KV_HEADS):
        rows = slice(h * KV_GROUP, (h + 1) * KV_GROUP)
        qh = q_ref[rows, :].astype(BF16)
        kh = kc_ref[:, h * hd:(h + 1) * hd].astype(BF16)
        vh = vc_ref[:, h * hd:(h + 1) * hd].astype(BF16)
        kn = kn_ref[h:h + 1, :].astype(BF16).astype(F32)
        vn = vn_ref[h:h + 1, :].astype(BF16).astype(F32)
        sc = lax.dot_general(qh, kh, (((1,), (1,)), ((), ())), preferred_element_type=F32) * scale
        sn = jnp.sum(qh.astype(F32) * kn, axis=-1, keepdims=True) * scale
        sk = sk_ref[rows, :]
        m = jnp.maximum(jnp.maximum(jnp.max(sc, axis=-1, keepdims=True), sn), sk)
        pc, pn = jnp.exp(sc - m), jnp.exp(sn - m)
        den = jnp.sum(pc, axis=-1, keepdims=True) + pn + jnp.exp(sk - m)
        o = jnp.dot((pc / den).astype(BF16), vh, preferred_element_type=F32)
        o = o + (pn / den).astype(BF16).astype(F32) * vn
        o_ref[rows, :] = o.astype(o_ref.dtype)


def attn_step(qkv, k_buf, v_buf, sinks, *, out_dtype):
    rows = qkv.shape[0]
    nq = N_KV_HEADS * KV_GROUP
    wb = k_buf.shape[1]
    q3 = qkv[:, :nq * HEAD_DIM].reshape(rows, nq, HEAD_DIM)
    kn = qkv[:, nq * HEAD_DIM:(nq + N_KV_HEADS) * HEAD_DIM].reshape(rows, N_KV_HEADS, HEAD_DIM)
    vn = qkv[:, (nq + N_KV_HEADS) * HEAD_DIM:].reshape(rows, N_KV_HEADS, HEAD_DIM)
    kvw = N_KV_HEADS * HEAD_DIM
    o = pl.pallas_call(
        _attn_step_kernel,
        grid=(rows,),
        in_specs=[pl.BlockSpec((None, nq, HEAD_DIM), lambda b: (b, 0, 0)),
                  pl.BlockSpec((None, N_KV_HEADS, HEAD_DIM), lambda b: (b, 0, 0)),
                  pl.BlockSpec((None, N_KV_HEADS, HEAD_DIM), lambda b: (b, 0, 0)),
                  pl.BlockSpec((None, wb, kvw), lambda b: (b, 0, 0)),
                  pl.BlockSpec((None, wb, kvw), lambda b: (b, 0, 0)),
                  pl.BlockSpec((nq, 1), lambda b: (0, 0))],
        out_specs=pl.BlockSpec((None, nq, HEAD_DIM), lambda b: (b, 0, 0)),
        out_shape=jax.ShapeDtypeStruct((rows, nq, HEAD_DIM), out_dtype),
        compiler_params=_params("parallel"),
        name="attn_step",
    )(q3, kn, vn, k_buf.reshape(rows, wb, kvw), v_buf.reshape(rows, wb, kvw),
      sinks.astype(F32).reshape(nq, 1))
    new_k = jnp.concatenate([k_buf, kn[:, None]], axis=1)[:, 1:]
    new_v = jnp.concatenate([v_buf, vn[:, None]], axis=1)[:, 1:]
    return o.reshape(rows, nq * HEAD_DIM), new_k, new_v


def kernel(x_prompt, x_sample, state_ssm_re, state_ssm_im, state_sconv, cache_k, cache_v, state_ffn_conv, norm_mix_g, norm_ffn_g, norm_final_g, w_in_even, ssm_lambda_re, ssm_lambda_im, ssm_log_dt, ssm_b_re, ssm_b_im, ssm_c_re, ssm_c_im, ssm_d, w_glu, b_glu, sconv_w, w_out_even, w_qkv, b_qkv, attn_sinks, w_o, b_o, w_ffn_gate, w_ffn_up, ffn_conv_w, w_ffn_down):
    bp, lp, d_model = x_prompt.shape
    bs = x_sample.shape[0]
    d_ssm, d_conv = ssm_d.shape[1], sconv_w.shape[2]
    nq, nkv = N_KV_HEADS * KV_GROUP * HEAD_DIM, N_KV_HEADS * HEAD_DIM
    keep = min(WINDOW, lp)
    w_down_bf16 = w_ffn_down.astype(BF16)
    xp, xs = x_prompt.reshape(bp * lp, d_model), x_sample.reshape(bs, d_model)
    p_out = [[] for _ in range(6)]
    s_out = [[] for _ in range(6)]
    for l in range(DEPTH):
        j = l // 2
        hp, hs = rmsnorm(xp, norm_mix_g[l], BF16), rmsnorm(xs, norm_mix_g[l], BF16)
        if l % 2 == 0:
            prep = s5_prepare(ssm_lambda_re[j], ssm_lambda_im[j], ssm_log_dt[j], ssm_b_re[j], ssm_b_im[j],
                              ssm_c_re[j], ssm_c_im[j], ssm_d[j])
            zp, zs = matmul_dual(hp, hs, w_in_even, j)
            cv, cb, cc = d_ssm, d_ssm + d_conv, d_ssm + 2 * d_conv
            gp, stp = s5_seq(zp, prep, batch=bp)
            gs, sts = s5_step(zs, _state_to_blocks(state_ssm_re[j], state_ssm_im[j]), prep)
            for out, st in ((p_out, stp[:, :, 0, :]), (s_out, sts)):
                re, im = _blocks_to_state(st)
                out[0].append(re)
                out[1].append(im)
            bp_out, scp = conv_seq((zp, cc), (zp, cv), (zp, cb), sconv_w[j], batch=bp, ncols=d_conv,
                                   act="none", out_dtype=BF16)
            bs_out, scs = conv_step((zs, cc), (zs, cv), (zs, cb), sconv_w[j], state_sconv[j], ncols=d_conv,
                                    act="none", out_dtype=BF16)
            p_out[2].append(scp[:, 8 - (CONV_W - 1):, :])
            s_out[2].append(scs)
            ap, a_s = matmul_dual(gp, gs, w_glu, j, bias=b_glu, res=(gp, gs), mode="glu", out_dtype=BF16)
            xp, xs = matmul_dual(jnp.concatenate([ap, bp_out], axis=1), jnp.concatenate([a_s, bs_out], axis=1),
                                 w_out_even, j, res=(xp, xs), mode="res")
        else:
            qkvp, qkvs = matmul_dual(hp, hs, w_qkv, j, bias=b_qkv)
            op = attn_seq(qkvp, attn_sinks[j], batch=bp, out_dtype=BF16)
            kv = qkvp.reshape(bp, lp, -1)[:, lp - keep:, nq:]
            p_out[3].append(kv[..., :nkv].reshape(bp, keep, N_KV_HEADS, HEAD_DIM))
            p_out[4].append(kv[..., nkv:].reshape(bp, keep, N_KV_HEADS, HEAD_DIM))
            o_s, kk, vv = attn_step(qkvs, cache_k[j], cache_v[j], attn_sinks[j], out_dtype=BF16)
            s_out[3].append(kk)
            s_out[4].append(vv)
            xp, xs = matmul_dual(op, o_s, w_o, j, bias=b_o, res=(xp, xs), mode="res")
        hp, hs = rmsnorm(xp, norm_ffn_g[l], BF16), rmsnorm(xs, norm_ffn_g[l], BF16)
        ap, fcp, a_s, fcs = ffn_gate_up(hp, hs, w_ffn_gate, w_ffn_up, ffn_conv_w, state_ffn_conv, l, batch=bp)
        p_out[5].append(fcp[:, 8 - (CONV_W - 1):, :])
        s_out[5].append(fcs)
        xp, xs = matmul_dual(ap, a_s, w_down_bf16, l, res=(xp, xs), mode="res", tm=512)
    yp, ys = rmsnorm(xp, norm_final_g, F32), rmsnorm(xs, norm_final_g, F32)
    return (yp.reshape(bp, lp, d_model), ys.reshape(bs, 1, d_model), *[jnp.stack(t) for t in p_out],
            *[jnp.stack(t) for t in s_out])
```

```python
import functools

import jax
import jax.numpy as jnp
from jax import lax
from jax.experimental import pallas as pl
from jax.experimental.pallas import tpu as pltpu

F32 = jnp.float32
BF16 = jnp.bfloat16

DEPTH = 4
SSM_GROUP = 16
SSM_STATE = 64
GROUPS_PER_BLOCK = 8
S5_SUBSEQ = 8
CONV_W = 3
HEAD_DIM = 128
N_KV_HEADS = 8
KV_GROUP = 4
WINDOW = 128
RMS_EPS = 1e-5
NEG_INF = -1e30

VMEM_LIMIT_BYTES = 62 * 1024 * 1024

TILES_K_MODEL = dict(tm=512, tn=1024)
TILES_K_FF = dict(tm=512, tn=512)


def _params(*sem):
    return pltpu.CompilerParams(dimension_semantics=sem, vmem_limit_bytes=VMEM_LIMIT_BYTES)


def _rmsnorm_kernel(x_ref, g_ref, o_ref):
    x = x_ref[...]
    y = x * lax.rsqrt(jnp.mean(x * x, axis=-1, keepdims=True) + RMS_EPS)
    o_ref[...] = (y * g_ref[...]).astype(o_ref.dtype)


def rmsnorm(x, g, out_dtype):
    m, d = x.shape
    tm = min(m, 256)
    return pl.pallas_call(
        _rmsnorm_kernel,
        grid=(m // tm,),
        in_specs=[pl.BlockSpec((tm, d), lambda i: (i, 0)), pl.BlockSpec((1, d), lambda i: (0, 0))],
        out_specs=pl.BlockSpec((tm, d), lambda i: (i, 0)),
        out_shape=jax.ShapeDtypeStruct((m, d), out_dtype),
        compiler_params=_params("parallel"),
        name="rmsnorm",
    )(x, g.reshape(1, d))


def _mm_dual_kernel(*refs, nx, has_bias, mode, cast):
    it = iter(refs)
    xp_refs = [next(it) for _ in range(nx)]
    xs_refs = [next(it) for _ in range(nx)]
    w_ref = next(it)
    b_ref = next(it) if has_bias else None
    rp_ref, rs_ref = (next(it), next(it)) if mode != "plain" else (None, None)
    op_ref, os_ref = next(it), next(it)
    wbf_ref = next(it) if cast else w_ref

    def run(x_refs, r_ref, o_ref):
        acc, k0 = None, 0
        for x_ref in x_refs:
            k1 = k0 + x_ref.shape[1]
            part = jnp.dot(x_ref[...].astype(BF16), wbf_ref[k0:k1, :], preferred_element_type=F32)
            acc = part if acc is None else acc + part
            k0 = k1
        if has_bias:
            acc = acc + b_ref[...]
        if mode == "res":
            acc = r_ref[...] + acc
        elif mode == "glu":
            acc = r_ref[...] * jax.nn.sigmoid(acc)
        o_ref[...] = acc.astype(o_ref.dtype)

    @pl.when(pl.program_id(1) == 0)
    def _():
        if cast:
            wbf_ref[...] = w_ref[...].astype(BF16)
        run(xs_refs, rs_ref, os_ref)

    run(xp_refs, rp_ref, op_ref)


def matmul_dual(xp, xs, w, layer, *, bias=None, res=None, mode="plain", out_dtype=F32, tm=1024, tn=512):
    xps = xp if isinstance(xp, tuple) else (xp,)
    xss = xs if isinstance(xs, tuple) else (xs,)
    mp, ms = xps[0].shape[0], xss[0].shape[0]
    kdim, n = w.shape[1], w.shape[2]
    tm, tn = min(tm, mp), min(tn, n)
    cast = w.dtype != BF16
    in_specs = ([pl.BlockSpec((tm, x.shape[1]), lambda j, i: (i, 0)) for x in xps]
                + [pl.BlockSpec((ms, x.shape[1]), lambda j, i: (0, 0)) for x in xss]
                + [pl.BlockSpec((None, kdim, tn), lambda j, i: (layer, 0, j))])
    args = [*xps, *xss, w]
    if bias is not None:
        in_specs.append(pl.BlockSpec((None, 1, tn), lambda j, i: (layer, 0, j)))
        args.append(bias.reshape(bias.shape[0], 1, n))
    if mode != "plain":
        in_specs += [pl.BlockSpec((tm, tn), lambda j, i: (i, j)), pl.BlockSpec((ms, tn), lambda j, i: (0, j))]
        args += list(res)
    return pl.pallas_call(
        functools.partial(_mm_dual_kernel, nx=len(xps), has_bias=bias is not None, mode=mode, cast=cast),
        grid=(n // tn, mp // tm),
        in_specs=in_specs,
        out_specs=[pl.BlockSpec((tm, tn), lambda j, i: (i, j)), pl.BlockSpec((ms, tn), lambda j, i: (0, j))],
        out_shape=[jax.ShapeDtypeStruct((mp, n), out_dtype), jax.ShapeDtypeStruct((ms, n), out_dtype)],
        scratch_shapes=[pltpu.VMEM((kdim, tn), BF16)] if cast else [],
        compiler_params=_params("parallel", "arbitrary"),
        name="matmul_dual_" + mode,
    )(*args)


def _conv3_rows(g, w_ref, carry_ref):
    c1 = carry_ref[7:8, :]
    c2 = carry_ref[6:7, :]
    row = lax.broadcasted_iota(jnp.int32, g.shape, 0)
    g1 = jnp.where(row == 0, c1, pltpu.roll(g, 1, axis=0))
    g2 = jnp.where(row == 0, c2, jnp.where(row == 1, c1, pltpu.roll(g, 2, axis=0)))
    carry_ref[...] = g[g.shape[0] - 8:, :]
    return w_ref[0:1, :] * g2 + w_ref[1:2, :] * g1 + w_ref[2:3, :] * g


def _ffn_gu_kernel(xp_ref, xs_ref, wg_ref, wu_ref, cw_ref, prev_ref, ap_ref, stp_ref, as_ref, sts_ref,
                   wgb_ref, wub_ref, carry_ref, *, tiles_per_seq, nchunk):
    i = pl.program_id(1)

    @pl.when(i == 0)
    def _():
        wgb_ref[...] = wg_ref[...].astype(BF16)
        wub_ref[...] = wu_ref[...].astype(BF16)
        xs = xs_ref[...]
        gs = jnp.dot(xs, wgb_ref[...], preferred_element_type=F32)
        us = jnp.dot(xs, wub_ref[...], preferred_element_type=F32)
        prev = prev_ref[...]
        x2, x1 = prev[:, 0, :], prev[:, 1, :]
        y = cw_ref[0:1, :] * x2 + cw_ref[1:2, :] * x1 + cw_ref[2:3, :] * gs
        as_ref[...] = (jax.nn.silu(y) * us).astype(as_ref.dtype)
        sts_ref[:, 0, :] = x1
        sts_ref[:, 1, :] = gs

    @pl.when(i % tiles_per_seq == 0)
    def _():
        carry_ref[...] = jnp.zeros_like(carry_ref)

    rc = xp_ref.shape[0] // nchunk
    for c in range(nchunk):
        x = xp_ref[c * rc:(c + 1) * rc, :]
        g = jnp.dot(x, wgb_ref[...], preferred_element_type=F32)
        u = jnp.dot(x, wub_ref[...], preferred_element_type=F32)
        y = _conv3_rows(g, cw_ref, carry_ref)
        ap_ref[c * rc:(c + 1) * rc, :] = (jax.nn.silu(y) * u).astype(ap_ref.dtype)

    @pl.when(i % tiles_per_seq == tiles_per_seq - 1)
    def _():
        stp_ref[...] = carry_ref[...]


def ffn_gate_up(xp, xs, wg, wu, cw, prev_s, layer, *, batch, tm=2048, tn=256, nchunk=4):
    mp, d = xp.shape
    ms = xs.shape[0]
    f = wg.shape[2]
    tm = min(tm, mp // batch)
    tiles_per_seq = mp // batch // tm
    return pl.pallas_call(
        functools.partial(_ffn_gu_kernel, tiles_per_seq=tiles_per_seq, nchunk=nchunk),
        grid=(f // tn, mp // tm),
        in_specs=[pl.BlockSpec((tm, d), lambda j, i: (i, 0)),
                  pl.BlockSpec((ms, d), lambda j, i: (0, 0)),
                  pl.BlockSpec((None, d, tn), lambda j, i: (layer, 0, j)),
                  pl.BlockSpec((None, d, tn), lambda j, i: (layer, 0, j)),
                  pl.BlockSpec((None, CONV_W, tn), lambda j, i: (layer, 0, j)),
                  pl.BlockSpec((None, ms, CONV_W - 1, tn), lambda j, i: (layer, 0, 0, j))],
        out_specs=[pl.BlockSpec((tm, tn), lambda j, i: (i, j)),
                   pl.BlockSpec((None, 8, tn), lambda j, i: (i // tiles_per_seq, 0, j)),
                   pl.BlockSpec((ms, tn), lambda j, i: (0, j)),
                   pl.BlockSpec((ms, CONV_W - 1, tn), lambda j, i: (0, 0, j))],
        out_shape=[jax.ShapeDtypeStruct((mp, f), BF16),
                   jax.ShapeDtypeStruct((batch, 8, f), F32),
                   jax.ShapeDtypeStruct((ms, f), BF16),
                   jax.ShapeDtypeStruct((ms, CONV_W - 1, f), F32)],
        scratch_shapes=[pltpu.VMEM((d, tn), BF16), pltpu.VMEM((d, tn), BF16), pltpu.VMEM((8, tn), F32)],
        compiler_params=_params("parallel", "arbitrary"),
        name="ffn_gate_up",
    )(xp, xs, wg, wu, cw, prev_s)


def _conv_seq_kernel(*refs, mul_in, act):
    it = iter(refs)
    p_ref = next(it)
    q_ref = next(it) if mul_in else None
    r_ref, w_ref, o_ref, st_ref, carry_ref = next(it), next(it), next(it), next(it), next(it)
    m = p_ref[...] * q_ref[...] if mul_in else p_ref[...]

    @pl.when(pl.program_id(2) == 0)
    def _():
        carry_ref[...] = jnp.zeros_like(carry_ref)

    y = _conv3_rows(m, w_ref, carry_ref)
    if act == "silu":
        y = jax.nn.silu(y)
    o_ref[...] = (r_ref[...] * y).astype(o_ref.dtype)
    st_ref[...] = carry_ref[...]


def conv_seq(p, q, r, w, *, batch, ncols, act, out_dtype, tt=512, tc=512):
    mul_in = q is not None
    m_rows = p[0].shape[0]
    tt = min(tt, m_rows // batch)
    nt = m_rows // batch // tt
    nc = ncols // tc

    def spec(c0):
        b0 = c0 // tc
        return pl.BlockSpec((tt, tc), lambda j, b, t: (b * nt + t, b0 + j))

    in_specs, args = [spec(p[1])], [p[0]]
    if mul_in:
        in_specs.append(spec(q[1]))
        args.append(q[0])
    in_specs += [spec(r[1]), pl.BlockSpec((CONV_W, tc), lambda j, b, t: (0, j))]
    args += [r[0], w]
    return pl.pallas_call(
        functools.partial(_conv_seq_kernel, mul_in=mul_in, act=act),
        grid=(nc, batch, nt),
        in_specs=in_specs,
        out_specs=[pl.BlockSpec((tt, tc), lambda j, b, t: (b * nt + t, j)),
                   pl.BlockSpec((None, 8, tc), lambda j, b, t: (b, 0, j))],
        out_shape=[jax.ShapeDtypeStruct((m_rows, ncols), out_dtype),
                   jax.ShapeDtypeStruct((batch, 8, ncols), F32)],
        scratch_shapes=[pltpu.VMEM((8, tc), F32)],
        compiler_params=_params("parallel", "parallel", "arbitrary"),
        name="conv_seq_" + act,
    )(*args)


def _conv_step_kernel(*refs, mul_in, act):
    it = iter(refs)
    p_ref = next(it)
    q_ref = next(it) if mul_in else None
    r_ref, w_ref, prev_ref, o_ref, st_ref = next(it), next(it), next(it), next(it), next(it)
    m = p_ref[...] * q_ref[...] if mul_in else p_ref[...]
    prev = prev_ref[...]
    x2, x1 = prev[:, 0, :], prev[:, 1, :]
    y = w_ref[0:1, :] * x2 + w_ref[1:2, :] * x1 + w_ref[2:3, :] * m
    if act == "silu":
        y = jax.nn.silu(y)
    o_ref[...] = (r_ref[...] * y).astype(o_ref.dtype)
    st_ref[:, 0, :] = x1
    st_ref[:, 1, :] = m


def conv_step(p, q, r, w, prev, *, ncols, act, out_dtype, tc=512):
    mul_in = q is not None
    rows = p[0].shape[0]
    nc = ncols // tc

    def spec(c0):
        b0 = c0 // tc
        return pl.BlockSpec((rows, tc), lambda j: (0, b0 + j))

    in_specs, args = [spec(p[1])], [p[0]]
    if mul_in:
        in_specs.append(spec(q[1]))
        args.append(q[0])
    in_specs += [spec(r[1]), pl.BlockSpec((CONV_W, tc), lambda j: (0, j)),
                 pl.BlockSpec((rows, CONV_W - 1, tc), lambda j: (0, 0, j))]
    args += [r[0], w, prev]
    return pl.pallas_call(
        functools.partial(_conv_step_kernel, mul_in=mul_in, act=act),
        grid=(nc,),
        in_specs=in_specs,
        out_specs=[pl.BlockSpec((rows, tc), lambda j: (0, j)),
                   pl.BlockSpec((rows, CONV_W - 1, tc), lambda j: (0, 0, j))],
        out_shape=[jax.ShapeDtypeStruct((rows, ncols), out_dtype),
                   jax.ShapeDtypeStruct((rows, CONV_W - 1, ncols), F32)],
        compiler_params=_params("parallel"),
        name="conv_step_" + act,
    )(*args)


def _s5_prep_kernel(lr_ref, li_ref, ldt_ref, br_ref, bi_ref, pow_ref, bb_ref):
    lr, li = lr_ref[...], li_ref[...]
    dt = jnp.exp(ldt_ref[...])
    mag = jnp.exp(lr * dt)
    ar, ai = mag * jnp.cos(li * dt), mag * jnp.sin(li * dt)
    den = lr * lr + li * li
    cr = ((ar - 1.0) * lr + ai * li) / den
    ci = (ai * lr - (ar - 1.0) * li) / den
    br, bi = br_ref[...], bi_ref[...]
    bb_ref[:, 0:SSM_GROUP, :] = cr * br - ci * bi
    bb_ref[:, SSM_GROUP:2 * SSM_GROUP, :] = cr * bi + ci * br
    pr, pi = ar, ai
    for k in range(8):
        pow_ref[:, k:k + 1, :] = pr
        pow_ref[:, 8 + k:9 + k, :] = pi
        pr, pi = pr * ar - pi * ai, pr * ai + pi * ar


def s5_prepare(lam_re, lam_im, log_dt, b_re, b_im, c_re, c_im, d_skip):
    g, p = lam_re.shape
    i = SSM_GROUP
    nb, gb = g // GROUPS_PER_BLOCK, GROUPS_PER_BLOCK
    pw, bb = pl.pallas_call(
        _s5_prep_kernel,
        out_shape=[jax.ShapeDtypeStruct((g, 16, p), F32), jax.ShapeDtypeStruct((g, 2 * i, p), F32)],
        name="s5_prep",
    )(lam_re.reshape(g, 1, p), lam_im.reshape(g, 1, p), log_dt.reshape(g, 1, 1),
      b_re.transpose(0, 2, 1), b_im.transpose(0, 2, 1))

    def lanes(a):
        return a.reshape(nb, gb, 8, p).transpose(0, 2, 1, 3).reshape(nb, 8, gb * p)

    pr, pi = lanes(pw[:, 0:8]), lanes(pw[:, 8:16])

    def bcast(a, k):
        return jnp.broadcast_to(a[:, k - 1:k, :], a.shape)

    tab = jnp.concatenate([pr, pi, bcast(pr, 1), bcast(pi, 1), bcast(pr, 8), bcast(pi, 8)], axis=1)
    eye = jnp.eye(gb, dtype=F32)

    def wb_part(a):
        a = a.reshape(nb, gb, i, p)
        return jnp.einsum('jaip,ab->jaibp', a, eye).reshape(nb, gb * i, gb * p)

    def wc_part(a):
        a = a.reshape(nb, gb, i, p)
        return jnp.einsum('jaip,ab->jbpai', a, eye).reshape(nb, gb * p, gb * i)

    wb = jnp.concatenate([wb_part(bb[:, :i]), wb_part(bb[:, i:])], axis=2).astype(BF16)
    wc = jnp.concatenate([wc_part(c_re.astype(F32)), wc_part(-c_im.astype(F32))], axis=1).astype(BF16)
    return tab, wb, wc, d_skip.reshape(nb, 1, gb * i)


_T_P, _T_A1, _T_A8 = 0, 16, 32


def _s5_seq_kernel(u_ref, wb_ref, wc_ref, tab_ref, d_ref, g_ref, st_ref, h_ref, pw_ref, *, chunk):
    seq, width = h_ref.shape
    half, lj = width // 2, seq // S5_SUBSEQ
    re, im = slice(0, half), slice(half, width)

    def tab(off):
        return tab_ref[off:off + 8, :], tab_ref[off + 8:off + 16, :]

    @pl.when(pl.program_id(1) == 0)
    def _():
        a8r, a8i = tab(_T_A8)

        def fill(m, c):
            pr, pi = c
            r0 = pl.multiple_of(m * 8, 8)
            pw_ref[pl.ds(r0, 8), re] = pr
            pw_ref[pl.ds(r0, 8), im] = pi
            return pr * a8r - pi * a8i, pr * a8i + pi * a8r

        lax.fori_loop(0, lj // 8, fill, tab(_T_P))

    for c in range(seq // chunk):
        rows = slice(c * chunk, (c + 1) * chunk)
        h_ref[rows, :] = jnp.dot(u_ref[rows, :].astype(BF16), wb_ref[...], preferred_element_type=F32)

    a1r, a1i = tab(_T_A1)

    def pass1(j, c):
        hr, hi = c
        r0 = pl.multiple_of(j * 8, 8)
        hr, hi = (a1r * hr - a1i * hi) + h_ref[pl.ds(r0, 8), re], (a1r * hi + a1i * hr) + h_ref[pl.ds(r0, 8), im]
        h_ref[pl.ds(r0, 8), re] = hr
        h_ref[pl.ds(r0, 8), im] = hi
        return hr, hi

    zero = jnp.zeros((S5_SUBSEQ, half), F32)
    xr, xi = lax.fori_loop(0, lj, pass1, (zero, zero), unroll=4)

    row = lax.broadcasted_iota(jnp.int32, (S5_SUBSEQ, half), 0)
    qr, qi = pw_ref[lj - 1:lj, re], pw_ref[lj - 1:lj, im]
    for k in (1, 2, 4):
        sr = jnp.where(row >= k, pltpu.roll(xr, k, axis=0), 0.0)
        si = jnp.where(row >= k, pltpu.roll(xi, k, axis=0), 0.0)
        xr, xi = xr + (qr * sr - qi * si), xi + (qr * si + qi * sr)
        qr, qi = qr * qr - qi * qi, 2.0 * (qr * qi)
    st_ref[:, re] = jnp.broadcast_to(xr[S5_SUBSEQ - 1:, :], xr.shape)
    st_ref[:, im] = jnp.broadcast_to(xi[S5_SUBSEQ - 1:, :], xi.shape)
    er = jnp.where(row >= 1, pltpu.roll(xr, 1, axis=0), 0.0)
    ei = jnp.where(row >= 1, pltpu.roll(xi, 1, axis=0), 0.0)

    def pass2(j, _):
        r0 = pl.multiple_of(j * 8, 8)
        pr = jnp.broadcast_to(pw_ref[pl.ds(j, 1), re], er.shape)
        pi = jnp.broadcast_to(pw_ref[pl.ds(j, 1), im], er.shape)
        h_ref[pl.ds(r0, 8), re] += pr * er - pi * ei
        h_ref[pl.ds(r0, 8), im] += pr * ei + pi * er
        return 0

    lax.fori_loop(0, lj, pass2, 0, unroll=4)
    for c in range(seq // chunk):
        rows = slice(c * chunk, (c + 1) * chunk)
        y = jnp.dot(h_ref[rows, :].astype(BF16), wc_ref[...], preferred_element_type=F32)
        g_ref[rows, :] = jax.nn.gelu(y + d_ref[...] * u_ref[rows, :])


def s5_seq(z, prep, *, batch):
    tab, wb, wc, d = prep
    nb, cin, cst = wb.shape
    m_rows = z.shape[0]
    seq = m_rows // batch
    lj = seq // S5_SUBSEQ

    def to_kernel_order(a):
        return a.reshape(batch, S5_SUBSEQ, lj, -1).transpose(0, 2, 1, 3).reshape(m_rows, -1)

    def to_time_order(a):
        return a.reshape(batch, lj, S5_SUBSEQ, -1).transpose(0, 2, 1, 3).reshape(m_rows, -1)

    g, st = pl.pallas_call(
        functools.partial(_s5_seq_kernel, chunk=min(seq, 512)),
        grid=(nb, batch),
        in_specs=[pl.BlockSpec((seq, cin), lambda j, b: (b, j)),
                  pl.BlockSpec((None, cin, cst), lambda j, b: (j, 0, 0)),
                  pl.BlockSpec((None, cst, cin), lambda j, b: (j, 0, 0)),
                  pl.BlockSpec((None, tab.shape[1], cst // 2), lambda j, b: (j, 0, 0)),
                  pl.BlockSpec((None, 1, cin), lambda j, b: (j, 0, 0))],
        out_specs=[pl.BlockSpec((seq, cin), lambda j, b: (b, j)),
                   pl.BlockSpec((None, None, 8, cst), lambda j, b: (j, b, 0, 0))],
        out_shape=[jax.ShapeDtypeStruct((m_rows, nb * cin), F32),
                   jax.ShapeDtypeStruct((nb, batch, 8, cst), F32)],
        scratch_shapes=[pltpu.VMEM((seq, cst), F32), pltpu.VMEM((lj, cst), F32)],
        compiler_params=_params("parallel", "arbitrary"),
        name="s5_seq",
    )(to_kernel_order(z[:, :nb * cin]), wb, wc, tab, d)
    return to_time_order(g), st


def _s5_step_kernel(u_ref, s_ref, wb_ref, wc_ref, tab_ref, d_ref, g_ref, st_ref):
    half = s_ref.shape[1] // 2
    u = u_ref[...]
    bu = jnp.dot(u.astype(BF16), wb_ref[...], preferred_element_type=F32)
    ar, ai = tab_ref[_T_P:_T_P + 1, :], tab_ref[_T_P + 8:_T_P + 9, :]
    sr, si = s_ref[:, 0:half], s_ref[:, half:2 * half]
    hr = ar * sr - ai * si + bu[:, 0:half]
    hi = ar * si + ai * sr + bu[:, half:2 * half]
    st_ref[:, 0:half] = hr
    st_ref[:, half:2 * half] = hi
    h = jnp.concatenate([hr, hi], axis=1)
    y = jnp.dot(h.astype(BF16), wc_ref[...], preferred_element_type=F32) + d_ref[...] * u
    g_ref[...] = jax.nn.gelu(y)


def s5_step(z, state, prep):
    tab, wb, wc, d = prep
    nb, cin, cst = wb.shape
    rows = z.shape[0]
    return pl.pallas_call(
        _s5_step_kernel,
        grid=(nb,),
        in_specs=[pl.BlockSpec((rows, cin), lambda j: (0, j)),
                  pl.BlockSpec((None, rows, cst), lambda j: (j, 0, 0)),
                  pl.BlockSpec((None, cin, cst), lambda j: (j, 0, 0)),
                  pl.BlockSpec((None, cst, cin), lambda j: (j, 0, 0)),
                  pl.BlockSpec((None, tab.shape[1], cst // 2), lambda j: (j, 0, 0)),
                  pl.BlockSpec((None, 1, cin), lambda j: (j, 0, 0))],
        out_specs=[pl.BlockSpec((rows, cin), lambda j: (0, j)),
                   pl.BlockSpec((None, rows, cst), lambda j: (j, 0, 0))],
        out_shape=[jax.ShapeDtypeStruct((rows, nb * cin), F32),
                   jax.ShapeDtypeStruct((nb, rows, cst), F32)],
        compiler_params=_params("parallel"),
        name="s5_step",
    )(z, state, wb, wc, tab, d)


def _state_to_blocks(s_re, s_im):
    b, g, p = s_re.shape
    nb = g // GROUPS_PER_BLOCK

    def f(s):
        return s.reshape(b, nb, GROUPS_PER_BLOCK * p).transpose(1, 0, 2)

    return jnp.concatenate([f(s_re), f(s_im)], axis=2)


def _blocks_to_state(st, p=SSM_STATE):
    nb, b, c = st.shape
    half = c // 2

    def f(s):
        return s.transpose(1, 0, 2).reshape(b, nb * half // p, p)

    return f(st[:, :, :half]), f(st[:, :, half:])


def _attn_seq_kernel(q_ref, kp_ref, kc_ref, vp_ref, vc_ref, sk_ref, o_ref):
    hd, blk = HEAD_DIM, WINDOW
    n = pl.program_id(1)
    shape = (KV_GROUP * blk, 2 * blk)
    qi = lax.broadcasted_iota(jnp.int32, shape, 0) % blk
    kj = lax.broadcasted_iota(jnp.int32, shape, 1)
    vis = (kj >= qi) & (kj <= qi + WINDOW) & ((kj >= blk) | (n > 0))
    for h in range(N_KV_HEADS):
        c0 = h * KV_GROUP * hd
        qa = jnp.concatenate([q_ref[:, c0 + g * hd:c0 + (g + 1) * hd] for g in range(KV_GROUP)], axis=0)
        kb = jnp.concatenate([kp_ref[:, h * hd:(h + 1) * hd], kc_ref[:, h * hd:(h + 1) * hd]], axis=0)
        vb = jnp.concatenate([vp_ref[:, h * hd:(h + 1) * hd], vc_ref[:, h * hd:(h + 1) * hd]], axis=0)
        s = lax.dot_general(qa.astype(BF16), kb.astype(BF16), (((1,), (1,)), ((), ())),
                            preferred_element_type=F32) * (hd ** -0.5)
        s = jnp.where(vis, s, NEG_INF)
        sk = sk_ref[h]
        m = jnp.maximum(jnp.max(s, axis=-1, keepdims=True), sk)
        p = jnp.exp(s - m)
        w = p * (1.0 / (jnp.sum(p, axis=-1, keepdims=True) + jnp.exp(sk - m)))
        o = jnp.dot(w.astype(BF16), vb.astype(BF16), preferred_element_type=F32)
        for g in range(KV_GROUP):
            o_ref[:, c0 + g * hd:c0 + (g + 1) * hd] = o[g * blk:(g + 1) * blk].astype(o_ref.dtype)


def attn_seq(qkv, sinks, *, batch, out_dtype):
    m_rows = qkv.shape[0]
    nblk = m_rows // batch // WINDOW
    nq = N_KV_HEADS * KV_GROUP
    qw, kvw = nq * HEAD_DIM, N_KV_HEADS * HEAD_DIM
    kblk, vblk = qw // kvw, qw // kvw + 1
    sk = jnp.repeat(sinks.astype(F32).reshape(N_KV_HEADS, KV_GROUP), WINDOW, axis=1)[..., None]

    def cur(c):
        return pl.BlockSpec((WINDOW, kvw), lambda b, n: (b * nblk + n, c))

    def prev(c):
        return pl.BlockSpec((WINDOW, kvw), lambda b, n: (b * nblk + jnp.maximum(n - 1, 0), c))

    return pl.pallas_call(
        _attn_seq_kernel,
        grid=(batch, nblk),
        in_specs=[pl.BlockSpec((WINDOW, qw), lambda b, n: (b * nblk + n, 0)),
                  prev(kblk), cur(kblk), prev(vblk), cur(vblk),
                  pl.BlockSpec((N_KV_HEADS, KV_GROUP * WINDOW, 1), lambda b, n: (0, 0, 0))],
        out_specs=pl.BlockSpec((WINDOW, qw), lambda b, n: (b * nblk + n, 0)),
        out_shape=jax.ShapeDtypeStruct((m_rows, qw), out_dtype),
        compiler_params=_params("parallel", "parallel"),
        name="attn_seq",
    )(qkv, qkv, qkv, qkv, qkv, sk)


def _attn_step_kernel(q_ref, kn_ref, vn_ref, kc_ref, vc_ref, sk_ref, o_ref):
    hd = HEAD_DIM
    scale = hd ** -0.5
    for h in range(N_KV_HEADS):
        rows = slice(h * KV_GROUP, (h + 1) * KV_GROUP)
        qh = q_ref[rows, :].astype(BF16)
        kh = kc_ref[:, h * hd:(h + 1) * hd].astype(BF16)
        vh = vc_ref[:, h * hd:(h + 1) * hd].astype(BF16)
        kn = kn_ref[h:h + 1, :].astype(BF16).astype(F32)
        vn = vn_ref[h:h + 1, :].astype(BF16).astype(F32)
        sc = lax.dot_general(qh, kh, (((1,), (1,)), ((), ())), preferred_element_type=F32) * scale
        sn = jnp.sum(qh.astype(F32) * kn, axis=-1, keepdims=True) * scale
        sk = sk_ref[rows, :]
        m = jnp.maximum(jnp.maximum(jnp.max(sc, axis=-1, keepdims=True), sn), sk)
        pc, pn = jnp.exp(sc - m), jnp.exp(sn - m)
        den = jnp.sum(pc, axis=-1, keepdims=True) + pn + jnp.exp(sk - m)
        o = jnp.dot((pc / den).astype(BF16), vh, preferred_element_type=F32)
        o = o + (pn / den).astype(BF16).astype(F32) * vn
        o_ref[rows, :] = o.astype(o_ref.dtype)


def attn_step(qkv, cache_k, cache_v, layer, sinks, *, out_dtype):
    rows = qkv.shape[0]
    nq = N_KV_HEADS * KV_GROUP
    layers, wb = cache_k.shape[0], cache_k.shape[2]
    q3 = qkv[:, :nq * HEAD_DIM].reshape(rows, nq, HEAD_DIM)
    kn = qkv[:, nq * HEAD_DIM:(nq + N_KV_HEADS) * HEAD_DIM].reshape(rows, N_KV_HEADS, HEAD_DIM)
    vn = qkv[:, (nq + N_KV_HEADS) * HEAD_DIM:].reshape(rows, N_KV_HEADS, HEAD_DIM)
    kvw = N_KV_HEADS * HEAD_DIM
    o = pl.pallas_call(
        _attn_step_kernel,
        grid=(rows,),
        in_specs=[pl.BlockSpec((None, nq, HEAD_DIM), lambda b: (b, 0, 0)),
                  pl.BlockSpec((None, N_KV_HEADS, HEAD_DIM), lambda b: (b, 0, 0)),
                  pl.BlockSpec((None, N_KV_HEADS, HEAD_DIM), lambda b: (b, 0, 0)),
                  pl.BlockSpec((None, None, wb, kvw), lambda b: (layer, b, 0, 0)),
                  pl.BlockSpec((None, None, wb, kvw), lambda b: (layer, b, 0, 0)),
                  pl.BlockSpec((nq, 1), lambda b: (0, 0))],
        out_specs=pl.BlockSpec((None, nq, HEAD_DIM), lambda b: (b, 0, 0)),
        out_shape=jax.ShapeDtypeStruct((rows, nq, HEAD_DIM), out_dtype),
        compiler_params=_params("parallel"),
        name="attn_step",
    )(q3, kn, vn, cache_k.reshape(layers, rows, wb, kvw), cache_v.reshape(layers, rows, wb, kvw),
      sinks.astype(F32).reshape(nq, 1))
    new_k = jnp.concatenate([cache_k[layer, :, 1:], kn[:, None]], axis=1)
    new_v = jnp.concatenate([cache_v[layer, :, 1:], vn[:, None]], axis=1)
    return o.reshape(rows, nq * HEAD_DIM), new_k, new_v


def kernel(x_prompt, x_sample, state_ssm_re, state_ssm_im, state_sconv, cache_k, cache_v, state_ffn_conv, norm_mix_g, norm_ffn_g, norm_final_g, w_in_even, ssm_lambda_re, ssm_lambda_im, ssm_log_dt, ssm_b_re, ssm_b_im, ssm_c_re, ssm_c_im, ssm_d, w_glu, b_glu, sconv_w, w_out_even, w_qkv, b_qkv, attn_sinks, w_o, b_o, w_ffn_gate, w_ffn_up, ffn_conv_w, w_ffn_down):
    bp, lp, d_model = x_prompt.shape
    bs = x_sample.shape[0]
    d_ssm, d_conv = ssm_d.shape[1], sconv_w.shape[2]
    nq, nkv = N_KV_HEADS * KV_GROUP * HEAD_DIM, N_KV_HEADS * HEAD_DIM
    keep = min(WINDOW, lp)
    w_down_bf16 = w_ffn_down.astype(BF16)
    xp, xs = x_prompt.reshape(bp * lp, d_model), x_sample.reshape(bs, d_model)
    p_out = [[] for _ in range(6)]
    s_out = [[] for _ in range(6)]
    for l in range(DEPTH):
        j = l // 2
        hp, hs = rmsnorm(xp, norm_mix_g[l], BF16), rmsnorm(xs, norm_mix_g[l], BF16)
        if l % 2 == 0:
            prep = s5_prepare(ssm_lambda_re[j], ssm_lambda_im[j], ssm_log_dt[j], ssm_b_re[j], ssm_b_im[j],
                              ssm_c_re[j], ssm_c_im[j], ssm_d[j])
            zp, zs = matmul_dual(hp, hs, w_in_even, j, **TILES_K_MODEL)
            cv, cb, cc = d_ssm, d_ssm + d_conv, d_ssm + 2 * d_conv
            gp, stp = s5_seq(zp, prep, batch=bp)
            gs, sts = s5_step(zs, _state_to_blocks(state_ssm_re[j], state_ssm_im[j]), prep)
            for out, st in ((p_out, stp[:, :, 0, :]), (s_out, sts)):
                re, im = _blocks_to_state(st)
                out[0].append(re)
                out[1].append(im)
            bp_out, scp = conv_seq((zp, cc), (zp, cv), (zp, cb), sconv_w[j], batch=bp, ncols=d_conv,
                                   act="none", out_dtype=BF16)
            bs_out, scs = conv_step((zs, cc), (zs, cv), (zs, cb), sconv_w[j], state_sconv[j], ncols=d_conv,
                                    act="none", out_dtype=BF16)
            p_out[2].append(scp[:, 8 - (CONV_W - 1):, :])
            s_out[2].append(scs)
            ap, a_s = matmul_dual(gp, gs, w_glu, j, bias=b_glu, res=(gp, gs), mode="glu", out_dtype=BF16)
            xp, xs = matmul_dual((ap, bp_out), (a_s, bs_out), w_out_even, j, res=(xp, xs), mode="res",
                                 **TILES_K_MODEL)
        else:
            qkvp, qkvs = matmul_dual(hp, hs, w_qkv, j, bias=b_qkv, **TILES_K_MODEL)
            op = attn_seq(qkvp, attn_sinks[j], batch=bp, out_dtype=BF16)
            kv = qkvp.reshape(bp, lp, -1)[:, lp - keep:, nq:]
            p_out[3].append(kv[..., :nkv].reshape(bp, keep, N_KV_HEADS, HEAD_DIM))
            p_out[4].append(kv[..., nkv:].reshape(bp, keep, N_KV_HEADS, HEAD_DIM))
            o_s, kk, vv = attn_step(qkvs, cache_k, cache_v, j, attn_sinks[j], out_dtype=BF16)
            s_out[3].append(kk)
            s_out[4].append(vv)
            xp, xs = matmul_dual(op, o_s, w_o, j, bias=b_o, res=(xp, xs), mode="res", **TILES_K_MODEL)
        hp, hs = rmsnorm(xp, norm_ffn_g[l], BF16), rmsnorm(xs, norm_ffn_g[l], BF16)
        ap, fcp, a_s, fcs = ffn_gate_up(hp, hs, w_ffn_gate, w_ffn_up, ffn_conv_w, state_ffn_conv, l, batch=bp)
        p_out[5].append(fcp[:, 8 - (CONV_W - 1):, :])
        s_out[5].append(fcs)
        xp, xs = matmul_dual(ap, a_s, w_down_bf16, l, res=(xp, xs), mode="res", **TILES_K_FF)
    yp, ys = rmsnorm(xp, norm_final_g, F32), rmsnorm(xs, norm_final_g, F32)
    return (yp.reshape(bp, lp, d_model), ys.reshape(bs, 1, d_model), *[jnp.stack(t) for t in p_out],
            *[jnp.stack(t) for t in s_out])
```

```python
import functools

import jax
import jax.numpy as jnp
from jax import lax
from jax.experimental import pallas as pl
from jax.experimental.pallas import tpu as pltpu

F32 = jnp.float32
BF16 = jnp.bfloat16

DEPTH = 4
SSM_GROUP = 16
SSM_STATE = 64
GROUPS_PER_BLOCK = 8
S5_SUBSEQ = 8
CONV_W = 3
HEAD_DIM = 128
N_KV_HEADS = 8
KV_GROUP = 4
WINDOW = 128
RMS_EPS = 1e-5
NEG_INF = -1e30

VMEM_LIMIT_BYTES = 62 * 1024 * 1024

TILES_K_MODEL = dict(tm=512, tn=1024)
TILES_K_FF = dict(tm=512, tn=512)


def _params(*sem):
    return pltpu.CompilerParams(dimension_semantics=sem, vmem_limit_bytes=VMEM_LIMIT_BYTES)


def _rmsnorm_kernel(x_ref, g_ref, o_ref):
    x = x_ref[...]
    y = x * lax.rsqrt(jnp.mean(x * x, axis=-1, keepdims=True) + RMS_EPS)
    o_ref[...] = (y * g_ref[...]).astype(o_ref.dtype)


def rmsnorm(x, g, out_dtype):
    m, d = x.shape
    tm = min(m, 512)
    return pl.pallas_call(
        _rmsnorm_kernel,
        grid=(m // tm,),
        in_specs=[pl.BlockSpec((tm, d), lambda i: (i, 0)), pl.BlockSpec((1, d), lambda i: (0, 0))],
        out_specs=pl.BlockSpec((tm, d), lambda i: (i, 0)),
        out_shape=jax.ShapeDtypeStruct((m, d), out_dtype),
        compiler_params=_params("parallel"),
        name="rmsnorm",
    )(x, g.reshape(1, d))


def _mm_dual_kernel(*refs, nx, has_bias, mode, cast):
    it = iter(refs)
    xp_refs = [next(it) for _ in range(nx)]
    xs_refs = [next(it) for _ in range(nx)]
    w_ref = next(it)
    b_ref = next(it) if has_bias else None
    rp_ref, rs_ref = (next(it), next(it)) if mode != "plain" else (None, None)
    op_ref, os_ref = next(it), next(it)
    wbf_ref = next(it) if cast else w_ref

    def run(x_refs, r_ref, o_ref):
        acc, k0 = None, 0
        for x_ref in x_refs:
            k1 = k0 + x_ref.shape[1]
            part = jnp.dot(x_ref[...].astype(BF16), wbf_ref[k0:k1, :], preferred_element_type=F32)
            acc = part if acc is None else acc + part
            k0 = k1
        if has_bias:
            acc = acc + b_ref[...]
        if mode == "res":
            acc = r_ref[...] + acc
        elif mode == "glu":
            acc = r_ref[...] * jax.nn.sigmoid(acc)
        o_ref[...] = acc.astype(o_ref.dtype)

    @pl.when(pl.program_id(1) == 0)
    def _():
        if cast:
            wbf_ref[...] = w_ref[...].astype(BF16)
        run(xs_refs, rs_ref, os_ref)

    run(xp_refs, rp_ref, op_ref)


def matmul_dual(xp, xs, w, layer, *, bias=None, res=None, mode="plain", out_dtype=F32, tm=1024, tn=512):
    xps = xp if isinstance(xp, tuple) else (xp,)
    xss = xs if isinstance(xs, tuple) else (xs,)
    mp, ms = xps[0].shape[0], xss[0].shape[0]
    kdim, n = w.shape[1], w.shape[2]
    tm, tn = min(tm, mp), min(tn, n)
    cast = w.dtype != BF16
    in_specs = ([pl.BlockSpec((tm, x.shape[1]), lambda j, i: (i, 0)) for x in xps]
                + [pl.BlockSpec((ms, x.shape[1]), lambda j, i: (0, 0)) for x in xss]
                + [pl.BlockSpec((None, kdim, tn), lambda j, i: (layer, 0, j))])
    args = [*xps, *xss, w]
    if bias is not None:
        in_specs.append(pl.BlockSpec((None, 1, tn), lambda j, i: (layer, 0, j)))
        args.append(bias.reshape(bias.shape[0], 1, n))
    if mode != "plain":
        in_specs += [pl.BlockSpec((tm, tn), lambda j, i: (i, j)), pl.BlockSpec((ms, tn), lambda j, i: (0, j))]
        args += list(res)
    return pl.pallas_call(
        functools.partial(_mm_dual_kernel, nx=len(xps), has_bias=bias is not None, mode=mode, cast=cast),
        grid=(n // tn, mp // tm),
        in_specs=in_specs,
        out_specs=[pl.BlockSpec((tm, tn), lambda j, i: (i, j)), pl.BlockSpec((ms, tn), lambda j, i: (0, j))],
        out_shape=[jax.ShapeDtypeStruct((mp, n), out_dtype), jax.ShapeDtypeStruct((ms, n), out_dtype)],
        scratch_shapes=[pltpu.VMEM((kdim, tn), BF16)] if cast else [],
        compiler_params=_params("parallel", "arbitrary"),
        name="matmul_dual_" + mode,
    )(*args)


def _conv3_rows(g, w_ref, carry_ref):
    c1 = carry_ref[7:8, :]
    c2 = carry_ref[6:7, :]
    row = lax.broadcasted_iota(jnp.int32, g.shape, 0)
    g1 = jnp.where(row == 0, c1, pltpu.roll(g, 1, axis=0))
    g2 = jnp.where(row == 0, c2, jnp.where(row == 1, c1, pltpu.roll(g, 2, axis=0)))
    carry_ref[...] = g[g.shape[0] - 8:, :]
    return w_ref[0:1, :] * g2 + w_ref[1:2, :] * g1 + w_ref[2:3, :] * g


def _ffn_gu_kernel(xp_ref, xs_ref, wg_ref, wu_ref, cw_ref, prev_ref, ap_ref, stp_ref, as_ref, sts_ref,
                   wgb_ref, wub_ref, carry_ref, *, tiles_per_seq, nchunk):
    i = pl.program_id(1)

    @pl.when(i == 0)
    def _():
        wgb_ref[...] = wg_ref[...].astype(BF16)
        wub_ref[...] = wu_ref[...].astype(BF16)
        xs = xs_ref[...]
        gs = jnp.dot(xs, wgb_ref[...], preferred_element_type=F32)
        us = jnp.dot(xs, wub_ref[...], preferred_element_type=F32)
        prev = prev_ref[...]
        x2, x1 = prev[:, 0, :], prev[:, 1, :]
        y = cw_ref[0:1, :] * x2 + cw_ref[1:2, :] * x1 + cw_ref[2:3, :] * gs
        as_ref[...] = (jax.nn.silu(y) * us).astype(as_ref.dtype)
        sts_ref[:, 0, :] = x1
        sts_ref[:, 1, :] = gs

    @pl.when(i % tiles_per_seq == 0)
    def _():
        carry_ref[...] = jnp.zeros_like(carry_ref)

    rc = xp_ref.shape[0] // nchunk
    for c in range(nchunk):
        x = xp_ref[c * rc:(c + 1) * rc, :]
        g = jnp.dot(x, wgb_ref[...], preferred_element_type=F32)
        u = jnp.dot(x, wub_ref[...], preferred_element_type=F32)
        y = _conv3_rows(g, cw_ref, carry_ref)
        ap_ref[c * rc:(c + 1) * rc, :] = (jax.nn.silu(y) * u).astype(ap_ref.dtype)

    @pl.when(i % tiles_per_seq == tiles_per_seq - 1)
    def _():
        stp_ref[...] = carry_ref[...]


def ffn_gate_up(xp, xs, wg, wu, cw, prev_s, layer, *, batch, tm=2048, tn=256, nchunk=4):
    mp, d = xp.shape
    ms = xs.shape[0]
    f = wg.shape[2]
    tm = min(tm, mp // batch)
    tiles_per_seq = mp // batch // tm
    return pl.pallas_call(
        functools.partial(_ffn_gu_kernel, tiles_per_seq=tiles_per_seq, nchunk=nchunk),
        grid=(f // tn, mp // tm),
        in_specs=[pl.BlockSpec((tm, d), lambda j, i: (i, 0)),
                  pl.BlockSpec((ms, d), lambda j, i: (0, 0)),
                  pl.BlockSpec((None, d, tn), lambda j, i: (layer, 0, j)),
                  pl.BlockSpec((None, d, tn), lambda j, i: (layer, 0, j)),
                  pl.BlockSpec((None, CONV_W, tn), lambda j, i: (layer, 0, j)),
                  pl.BlockSpec((None, ms, CONV_W - 1, tn), lambda j, i: (layer, 0, 0, j))],
        out_specs=[pl.BlockSpec((tm, tn), lambda j, i: (i, j)),
                   pl.BlockSpec((None, 8, tn), lambda j, i: (i // tiles_per_seq, 0, j)),
                   pl.BlockSpec((ms, tn), lambda j, i: (0, j)),
                   pl.BlockSpec((ms, CONV_W - 1, tn), lambda j, i: (0, 0, j))],
        out_shape=[jax.ShapeDtypeStruct((mp, f), BF16),
                   jax.ShapeDtypeStruct((batch, 8, f), F32),
                   jax.ShapeDtypeStruct((ms, f), BF16),
                   jax.ShapeDtypeStruct((ms, CONV_W - 1, f), F32)],
        scratch_shapes=[pltpu.VMEM((d, tn), BF16), pltpu.VMEM((d, tn), BF16), pltpu.VMEM((8, tn), F32)],
        compiler_params=_params("parallel", "arbitrary"),
        name="ffn_gate_up",
    )(xp, xs, wg, wu, cw, prev_s)


def _conv_seq_kernel(*refs, mul_in, act):
    it = iter(refs)
    p_ref = next(it)
    q_ref = next(it) if mul_in else None
    r_ref, w_ref, o_ref, st_ref, carry_ref = next(it), next(it), next(it), next(it), next(it)
    m = p_ref[...] * q_ref[...] if mul_in else p_ref[...]

    @pl.when(pl.program_id(2) == 0)
    def _():
        carry_ref[...] = jnp.zeros_like(carry_ref)

    y = _conv3_rows(m, w_ref, carry_ref)
    if act == "silu":
        y = jax.nn.silu(y)
    o_ref[...] = (r_ref[...] * y).astype(o_ref.dtype)
    st_ref[...] = carry_ref[...]


def conv_seq(p, q, r, w, *, batch, ncols, act, out_dtype, tt=512, tc=512):
    mul_in = q is not None
    m_rows = p[0].shape[0]
    tt = min(tt, m_rows // batch)
    nt = m_rows // batch // tt
    nc = ncols // tc

    def spec(c0):
        b0 = c0 // tc
        return pl.BlockSpec((tt, tc), lambda j, b, t: (b * nt + t, b0 + j))

    in_specs, args = [spec(p[1])], [p[0]]
    if mul_in:
        in_specs.append(spec(q[1]))
        args.append(q[0])
    in_specs += [spec(r[1]), pl.BlockSpec((CONV_W, tc), lambda j, b, t: (0, j))]
    args += [r[0], w]
    return pl.pallas_call(
        functools.partial(_conv_seq_kernel, mul_in=mul_in, act=act),
        grid=(nc, batch, nt),
        in_specs=in_specs,
        out_specs=[pl.BlockSpec((tt, tc), lambda j, b, t: (b * nt + t, j)),
                   pl.BlockSpec((None, 8, tc), lambda j, b, t: (b, 0, j))],
        out_shape=[jax.ShapeDtypeStruct((m_rows, ncols), out_dtype),
                   jax.ShapeDtypeStruct((batch, 8, ncols), F32)],
        scratch_shapes=[pltpu.VMEM((8, tc), F32)],
        compiler_params=_params("parallel", "parallel", "arbitrary"),
        name="conv_seq_" + act,
    )(*args)


def _conv_step_kernel(*refs, mul_in, act):
    it = iter(refs)
    p_ref = next(it)
    q_ref = next(it) if mul_in else None
    r_ref, w_ref, prev_ref, o_ref, st_ref = next(it), next(it), next(it), next(it), next(it)
    m = p_ref[...] * q_ref[...] if mul_in else p_ref[...]
    prev = prev_ref[...]
    x2, x1 = prev[:, 0, :], prev[:, 1, :]
    y = w_ref[0:1, :] * x2 + w_ref[1:2, :] * x1 + w_ref[2:3, :] * m
    if act == "silu":
        y = jax.nn.silu(y)
    o_ref[...] = (r_ref[...] * y).astype(o_ref.dtype)
    st_ref[:, 0, :] = x1
    st_ref[:, 1, :] = m


def conv_step(p, q, r, w, prev, *, ncols, act, out_dtype, tc=512):
    mul_in = q is not None
    rows = p[0].shape[0]
    nc = ncols // tc

    def spec(c0):
        b0 = c0 // tc
        return pl.BlockSpec((rows, tc), lambda j: (0, b0 + j))

    in_specs, args = [spec(p[1])], [p[0]]
    if mul_in:
        in_specs.append(spec(q[1]))
        args.append(q[0])
    in_specs += [spec(r[1]), pl.BlockSpec((CONV_W, tc), lambda j: (0, j)),
                 pl.BlockSpec((rows, CONV_W - 1, tc), lambda j: (0, 0, j))]
    args += [r[0], w, prev]
    return pl.pallas_call(
        functools.partial(_conv_step_kernel, mul_in=mul_in, act=act),
        grid=(nc,),
        in_specs=in_specs,
        out_specs=[pl.BlockSpec((rows, tc), lambda j: (0, j)),
                   pl.BlockSpec((rows, CONV_W - 1, tc), lambda j: (0, 0, j))],
        out_shape=[jax.ShapeDtypeStruct((rows, ncols), out_dtype),
                   jax.ShapeDtypeStruct((rows, CONV_W - 1, ncols), F32)],
        compiler_params=_params("parallel"),
        name="conv_step_" + act,
    )(*args)


def _s5_prep_kernel(lr_ref, li_ref, ldt_ref, br_ref, bi_ref, pow_ref, bb_ref):
    lr, li = lr_ref[...], li_ref[...]
    dt = jnp.exp(ldt_ref[...])
    mag = jnp.exp(lr * dt)
    ar, ai = mag * jnp.cos(li * dt), mag * jnp.sin(li * dt)
    den = lr * lr + li * li
    cr = ((ar - 1.0) * lr + ai * li) / den
    ci = (ai * lr - (ar - 1.0) * li) / den
    br, bi = br_ref[...], bi_ref[...]
    bb_ref[:, 0:SSM_GROUP, :] = cr * br - ci * bi
    bb_ref[:, SSM_GROUP:2 * SSM_GROUP, :] = cr * bi + ci * br
    pr, pi = ar, ai
    for k in range(8):
        pow_ref[:, k:k + 1, :] = pr
        pow_ref[:, 8 + k:9 + k, :] = pi
        pr, pi = pr * ar - pi * ai, pr * ai + pi * ar


def s5_prepare(lam_re, lam_im, log_dt, b_re, b_im, c_re, c_im, d_skip):
    g, p = lam_re.shape
    i = SSM_GROUP
    nb, gb = g // GROUPS_PER_BLOCK, GROUPS_PER_BLOCK
    pw, bb = pl.pallas_call(
        _s5_prep_kernel,
        out_shape=[jax.ShapeDtypeStruct((g, 16, p), F32), jax.ShapeDtypeStruct((g, 2 * i, p), F32)],
        name="s5_prep",
    )(lam_re.reshape(g, 1, p), lam_im.reshape(g, 1, p), log_dt.reshape(g, 1, 1),
      b_re.transpose(0, 2, 1), b_im.transpose(0, 2, 1))

    def lanes(a):
        return a.reshape(nb, gb, 8, p).transpose(0, 2, 1, 3).reshape(nb, 8, gb * p)

    pr, pi = lanes(pw[:, 0:8]), lanes(pw[:, 8:16])

    def bcast(a, k):
        return jnp.broadcast_to(a[:, k - 1:k, :], a.shape)

    tab = jnp.concatenate([pr, pi, bcast(pr, 1), bcast(pi, 1), bcast(pr, 8), bcast(pi, 8)], axis=1)
    eye = jnp.eye(gb, dtype=F32)

    def wb_part(a):
        a = a.reshape(nb, gb, i, p)
        return jnp.einsum('jaip,ab->jaibp', a, eye).reshape(nb, gb * i, gb * p)

    def wc_part(a):
        a = a.reshape(nb, gb, i, p)
        return jnp.einsum('jaip,ab->jbpai', a, eye).reshape(nb, gb * p, gb * i)

    wb = jnp.concatenate([wb_part(bb[:, :i]), wb_part(bb[:, i:])], axis=2).astype(BF16)
    wc = jnp.concatenate([wc_part(c_re.astype(F32)), wc_part(-c_im.astype(F32))], axis=1).astype(BF16)
    return tab, wb, wc, d_skip.reshape(nb, 1, gb * i)


_T_P, _T_A1, _T_A8 = 0, 16, 32


def _s5_seq_kernel(u_ref, wb_ref, wc_ref, tab_ref, d_ref, g_ref, st_ref, h_ref, pw_ref, *, chunk):
    seq, width = h_ref.shape
    half, lj = width // 2, seq // S5_SUBSEQ
    re, im = slice(0, half), slice(half, width)

    def tab(off):
        return tab_ref[off:off + 8, :], tab_ref[off + 8:off + 16, :]

    @pl.when(pl.program_id(1) == 0)
    def _():
        a8r, a8i = tab(_T_A8)

        def fill(m, c):
            pr, pi = c
            r0 = pl.multiple_of(m * 8, 8)
            pw_ref[pl.ds(r0, 8), re] = pr
            pw_ref[pl.ds(r0, 8), im] = pi
            return pr * a8r - pi * a8i, pr * a8i + pi * a8r

        lax.fori_loop(0, lj // 8, fill, tab(_T_P))

    for c in range(seq // chunk):
        rows = slice(c * chunk, (c + 1) * chunk)
        h_ref[rows, :] = jnp.dot(u_ref[rows, :].astype(BF16), wb_ref[...], preferred_element_type=F32)

    a1r, a1i = tab(_T_A1)

    def pass1(j, c):
        hr, hi = c
        r0 = pl.multiple_of(j * 8, 8)
        hr, hi = (a1r * hr - a1i * hi) + h_ref[pl.ds(r0, 8), re], (a1r * hi + a1i * hr) + h_ref[pl.ds(r0, 8), im]
        h_ref[pl.ds(r0, 8), re] = hr
        h_ref[pl.ds(r0, 8), im] = hi
        return hr, hi

    zero = jnp.zeros((S5_SUBSEQ, half), F32)
    xr, xi = lax.fori_loop(0, lj, pass1, (zero, zero), unroll=4)

    row = lax.broadcasted_iota(jnp.int32, (S5_SUBSEQ, half), 0)
    qr, qi = pw_ref[lj - 1:lj, re], pw_ref[lj - 1:lj, im]
    for k in (1, 2, 4):
        sr = jnp.where(row >= k, pltpu.roll(xr, k, axis=0), 0.0)
        si = jnp.where(row >= k, pltpu.roll(xi, k, axis=0), 0.0)
        xr, xi = xr + (qr * sr - qi * si), xi + (qr * si + qi * sr)
        qr, qi = qr * qr - qi * qi, 2.0 * (qr * qi)
    st_ref[:, re] = jnp.broadcast_to(xr[S5_SUBSEQ - 1:, :], xr.shape)
    st_ref[:, im] = jnp.broadcast_to(xi[S5_SUBSEQ - 1:, :], xi.shape)
    er = jnp.where(row >= 1, pltpu.roll(xr, 1, axis=0), 0.0)
    ei = jnp.where(row >= 1, pltpu.roll(xi, 1, axis=0), 0.0)

    def pass2(j, _):
        r0 = pl.multiple_of(j * 8, 8)
        pr = jnp.broadcast_to(pw_ref[pl.ds(j, 1), re], er.shape)
        pi = jnp.broadcast_to(pw_ref[pl.ds(j, 1), im], er.shape)
        h_ref[pl.ds(r0, 8), re] += pr * er - pi * ei
        h_ref[pl.ds(r0, 8), im] += pr * ei + pi * er
        return 0

    lax.fori_loop(0, lj, pass2, 0, unroll=4)
    for c in range(seq // chunk):
        rows = slice(c * chunk, (c + 1) * chunk)
        y = jnp.dot(h_ref[rows, :].astype(BF16), wc_ref[...], preferred_element_type=F32)
        g_ref[rows, :] = jax.nn.gelu(y + d_ref[...] * u_ref[rows, :])


def s5_seq(z, prep, *, batch):
    tab, wb, wc, d = prep
    nb, cin, cst = wb.shape
    m_rows = z.shape[0]
    seq = m_rows // batch
    lj = seq // S5_SUBSEQ

    def to_kernel_order(a):
        return a.reshape(batch, S5_SUBSEQ, lj, -1).transpose(0, 2, 1, 3).reshape(m_rows, -1)

    def to_time_order(a):
        return a.reshape(batch, lj, S5_SUBSEQ, -1).transpose(0, 2, 1, 3).reshape(m_rows, -1)

    g, st = pl.pallas_call(
        functools.partial(_s5_seq_kernel, chunk=min(seq, 512)),
        grid=(nb, batch),
        in_specs=[pl.BlockSpec((seq, cin), lambda j, b: (b, j)),
                  pl.BlockSpec((None, cin, cst), lambda j, b: (j, 0, 0)),
                  pl.BlockSpec((None, cst, cin), lambda j, b: (j, 0, 0)),
                  pl.BlockSpec((None, tab.shape[1], cst // 2), lambda j, b: (j, 0, 0)),
                  pl.BlockSpec((None, 1, cin), lambda j, b: (j, 0, 0))],
        out_specs=[pl.BlockSpec((seq, cin), lambda j, b: (b, j)),
                   pl.BlockSpec((None, None, 8, cst), lambda j, b: (j, b, 0, 0))],
        out_shape=[jax.ShapeDtypeStruct((m_rows, nb * cin), F32),
                   jax.ShapeDtypeStruct((nb, batch, 8, cst), F32)],
        scratch_shapes=[pltpu.VMEM((seq, cst), F32), pltpu.VMEM((lj, cst), F32)],
        compiler_params=_params("parallel", "arbitrary"),
        name="s5_seq",
    )(to_kernel_order(z[:, :nb * cin]), wb, wc, tab, d)
    return to_time_order(g), st


def _s5_step_kernel(u_ref, s_ref, wb_ref, wc_ref, tab_ref, d_ref, g_ref, st_ref):
    half = s_ref.shape[1] // 2
    u = u_ref[...]
    bu = jnp.dot(u.astype(BF16), wb_ref[...], preferred_element_type=F32)
    ar, ai = tab_ref[_T_P:_T_P + 1, :], tab_ref[_T_P + 8:_T_P + 9, :]
    sr, si = s_ref[:, 0:half], s_ref[:, half:2 * half]
    hr = ar * sr - ai * si + bu[:, 0:half]
    hi = ar * si + ai * sr + bu[:, half:2 * half]
    st_ref[:, 0:half] = hr
    st_ref[:, half:2 * half] = hi
    h = jnp.concatenate([hr, hi], axis=1)
    y = jnp.dot(h.astype(BF16), wc_ref[...], preferred_element_type=F32) + d_ref[...] * u
    g_ref[...] = jax.nn.gelu(y)


def s5_step(z, state, prep):
    tab, wb, wc, d = prep
    nb, cin, cst = wb.shape
    rows = z.shape[0]
    return pl.pallas_call(
        _s5_step_kernel,
        grid=(nb,),
        in_specs=[pl.BlockSpec((rows, cin), lambda j: (0, j)),
                  pl.BlockSpec((None, rows, cst), lambda j: (j, 0, 0)),
                  pl.BlockSpec((None, cin, cst), lambda j: (j, 0, 0)),
                  pl.BlockSpec((None, cst, cin), lambda j: (j, 0, 0)),
                  pl.BlockSpec((None, tab.shape[1], cst // 2), lambda j: (j, 0, 0)),
                  pl.BlockSpec((None, 1, cin), lambda j: (j, 0, 0))],
        out_specs=[pl.BlockSpec((rows, cin), lambda j: (0, j)),
                   pl.BlockSpec((None, rows, cst), lambda j: (j, 0, 0))],
        out_shape=[jax.ShapeDtypeStruct((rows, nb * cin), F32),
                   jax.ShapeDtypeStruct((nb, rows, cst), F32)],
        compiler_params=_params("parallel"),
        name="s5_step",
    )(z, state, wb, wc, tab, d)


def _state_to_blocks(s_re, s_im):
    b, g, p = s_re.shape
    nb = g // GROUPS_PER_BLOCK

    def f(s):
        return s.reshape(b, nb, GROUPS_PER_BLOCK * p).transpose(1, 0, 2)

    return jnp.concatenate([f(s_re), f(s_im)], axis=2)


def _blocks_to_state(st, p=SSM_STATE):
    nb, b, c = st.shape
    half = c // 2

    def f(s):
        return s.transpose(1, 0, 2).reshape(b, nb * half // p, p)

    return f(st[:, :, :half]), f(st[:, :, half:])


def _attn_seq_kernel(q_ref, kp_ref, kc_ref, vp_ref, vc_ref, sk_ref, o_ref):
    hd, blk = HEAD_DIM, WINDOW
    n = pl.program_id(1)
    shape = (KV_GROUP * blk, 2 * blk)
    qi = lax.broadcasted_iota(jnp.int32, shape, 0) % blk
    kj = lax.broadcasted_iota(jnp.int32, shape, 1)
    vis = (kj >= qi) & (kj <= qi + WINDOW) & ((kj >= blk) | (n > 0))
    for h in range(N_KV_HEADS):
        c0 = h * KV_GROUP * hd
        qa = jnp.concatenate([q_ref[:, c0 + g * hd:c0 + (g + 1) * hd] for g in range(KV_GROUP)], axis=0)
        kb = jnp.concatenate([kp_ref[:, h * hd:(h + 1) * hd], kc_ref[:, h * hd:(h + 1) * hd]], axis=0)
        vb = jnp.concatenate([vp_ref[:, h * hd:(h + 1) * hd], vc_ref[:, h * hd:(h + 1) * hd]], axis=0)
        s = lax.dot_general(qa.astype(BF16), kb.astype(BF16), (((1,), (1,)), ((), ())),
                            preferred_element_type=F32) * (hd ** -0.5)
        s = jnp.where(vis, s, NEG_INF)
        sk = sk_ref[h]
        m = jnp.maximum(jnp.max(s, axis=-1, keepdims=True), sk)
        p = jnp.exp(s - m)
        w = p * (1.0 / (jnp.sum(p, axis=-1, keepdims=True) + jnp.exp(sk - m)))
        o = jnp.dot(w.astype(BF16), vb.astype(BF16), preferred_element_type=F32)
        for g in range(KV_GROUP):
            o_ref[:, c0 + g * hd:c0 + (g + 1) * hd] = o[g * blk:(g + 1) * blk].astype(o_ref.dtype)


def attn_seq(qkv, sinks, *, batch, out_dtype):
    m_rows = qkv.shape[0]
    nblk = m_rows // batch // WINDOW
    nq = N_KV_HEADS * KV_GROUP
    qw, kvw = nq * HEAD_DIM, N_KV_HEADS * HEAD_DIM
    kblk, vblk = qw // kvw, qw // kvw + 1
    sk = jnp.repeat(sinks.astype(F32).reshape(N_KV_HEADS, KV_GROUP), WINDOW, axis=1)[..., None]

    def cur(c):
        return pl.BlockSpec((WINDOW, kvw), lambda b, n: (b * nblk + n, c))

    def prev(c):
        return pl.BlockSpec((WINDOW, kvw), lambda b, n: (b * nblk + jnp.maximum(n - 1, 0), c))

    return pl.pallas_call(
        _attn_seq_kernel,
        grid=(batch, nblk),
        in_specs=[pl.BlockSpec((WINDOW, qw), lambda b, n: (b * nblk + n, 0)),
                  prev(kblk), cur(kblk), prev(vblk), cur(vblk),
                  pl.BlockSpec((N_KV_HEADS, KV_GROUP * WINDOW, 1), lambda b, n: (0, 0, 0))],
        out_specs=pl.BlockSpec((WINDOW, qw), lambda b, n: (b * nblk + n, 0)),
        out_shape=jax.ShapeDtypeStruct((m_rows, qw), out_dtype),
        compiler_params=_params("parallel", "parallel"),
        name="attn_seq",
    )(qkv, qkv, qkv, qkv, qkv, sk)


def _attn_step_kernel(q_ref, kn_ref, vn_ref, kc_ref, vc_ref, sk_ref, o_ref):
    scale = HEAD_DIM ** -0.5
    q = q_ref[...].astype(BF16)
    s = lax.dot_general(q, kc_ref[...].astype(BF16), (((1,), (1,)), ((), ())),
                        preferred_element_type=F32) * scale
    own = (lax.broadcasted_iota(jnp.int32, s.shape, 1) % N_KV_HEADS
           == lax.broadcasted_iota(jnp.int32, s.shape, 0) // KV_GROUP)
    s = jnp.where(own, s, NEG_INF)
    sn = jnp.sum(q.astype(F32) * kn_ref[...].astype(BF16).astype(F32), axis=-1, keepdims=True) * scale
    sk = sk_ref[...]
    m = jnp.maximum(jnp.maximum(jnp.max(s, axis=-1, keepdims=True), sn), sk)
    p, pn = jnp.exp(s - m), jnp.exp(sn - m)
    r = 1.0 / (jnp.sum(p, axis=-1, keepdims=True) + pn + jnp.exp(sk - m))
    o = jnp.dot((p * r).astype(BF16), vc_ref[...].astype(BF16), preferred_element_type=F32)
    o = o + (pn * r).astype(BF16).astype(F32) * vn_ref[...].astype(BF16).astype(F32)
    o_ref[...] = o.astype(o_ref.dtype)


def attn_step(qkv, cache_k, cache_v, layer, sinks, *, out_dtype):
    rows = qkv.shape[0]
    nq = N_KV_HEADS * KV_GROUP
    layers, wb = cache_k.shape[0], cache_k.shape[2]
    q3 = qkv[:, :nq * HEAD_DIM].reshape(rows, nq, HEAD_DIM)
    kn = qkv[:, nq * HEAD_DIM:(nq + N_KV_HEADS) * HEAD_DIM].reshape(rows, N_KV_HEADS, HEAD_DIM)
    vn = qkv[:, (nq + N_KV_HEADS) * HEAD_DIM:].reshape(rows, N_KV_HEADS, HEAD_DIM)
    nkeys = wb * N_KV_HEADS
    head_spec = pl.BlockSpec((None, nq, HEAD_DIM), lambda b: (b, 0, 0))
    cache_spec = pl.BlockSpec((None, None, nkeys, HEAD_DIM), lambda b: (layer, b, 0, 0))
    o = pl.pallas_call(
        _attn_step_kernel,
        grid=(rows,),
        in_specs=[head_spec, head_spec, head_spec, cache_spec, cache_spec,
                  pl.BlockSpec((nq, 1), lambda b: (0, 0))],
        out_specs=head_spec,
        out_shape=jax.ShapeDtypeStruct((rows, nq, HEAD_DIM), out_dtype),
        compiler_params=_params("parallel"),
        name="attn_step",
    )(q3, jnp.repeat(kn, KV_GROUP, axis=1), jnp.repeat(vn, KV_GROUP, axis=1),
      cache_k.reshape(layers, rows, nkeys, HEAD_DIM), cache_v.reshape(layers, rows, nkeys, HEAD_DIM),
      sinks.astype(F32).reshape(nq, 1))
    new_k = jnp.concatenate([cache_k[layer, :, 1:], kn[:, None]], axis=1)
    new_v = jnp.concatenate([cache_v[layer, :, 1:], vn[:, None]], axis=1)
    return o.reshape(rows, nq * HEAD_DIM), new_k, new_v


def kernel(x_prompt, x_sample, state_ssm_re, state_ssm_im, state_sconv, cache_k, cache_v, state_ffn_conv, norm_mix_g, norm_ffn_g, norm_final_g, w_in_even, ssm_lambda_re, ssm_lambda_im, ssm_log_dt, ssm_b_re, ssm_b_im, ssm_c_re, ssm_c_im, ssm_d, w_glu, b_glu, sconv_w, w_out_even, w_qkv, b_qkv, attn_sinks, w_o, b_o, w_ffn_gate, w_ffn_up, ffn_conv_w, w_ffn_down):
    bp, lp, d_model = x_prompt.shape
    bs = x_sample.shape[0]
    d_ssm, d_conv = ssm_d.shape[1], sconv_w.shape[2]
    nq, nkv = N_KV_HEADS * KV_GROUP * HEAD_DIM, N_KV_HEADS * HEAD_DIM
    keep = min(WINDOW, lp)
    w_down_bf16 = w_ffn_down.astype(BF16)
    xp, xs = x_prompt.reshape(bp * lp, d_model), x_sample.reshape(bs, d_model)
    p_out = [[] for _ in range(6)]
    s_out = [[] for _ in range(6)]
    for l in range(DEPTH):
        j = l // 2
        hp, hs = rmsnorm(xp, norm_mix_g[l], BF16), rmsnorm(xs, norm_mix_g[l], BF16)
        if l % 2 == 0:
            prep = s5_prepare(ssm_lambda_re[j], ssm_lambda_im[j], ssm_log_dt[j], ssm_b_re[j], ssm_b_im[j],
                              ssm_c_re[j], ssm_c_im[j], ssm_d[j])
            zp, zs = matmul_dual(hp, hs, w_in_even, j, **TILES_K_MODEL)
            cv, cb, cc = d_ssm, d_ssm + d_conv, d_ssm + 2 * d_conv
            gp, stp = s5_seq(zp, prep, batch=bp)
            gs, sts = s5_step(zs, _state_to_blocks(state_ssm_re[j], state_ssm_im[j]), prep)
            for out, st in ((p_out, stp[:, :, 0, :]), (s_out, sts)):
                re, im = _blocks_to_state(st)
                out[0].append(re)
                out[1].append(im)
            bp_out, scp = conv_seq((zp, cc), (zp, cv), (zp, cb), sconv_w[j], batch=bp, ncols=d_conv,
                                   act="none", out_dtype=BF16)
            bs_out, scs = conv_step((zs, cc), (zs, cv), (zs, cb), sconv_w[j], state_sconv[j], ncols=d_conv,
                                    act="none", out_dtype=BF16)
            p_out[2].append(scp[:, 8 - (CONV_W - 1):, :])
            s_out[2].append(scs)
            ap, a_s = matmul_dual(gp, gs, w_glu, j, bias=b_glu, res=(gp, gs), mode="glu", out_dtype=BF16)
            xp, xs = matmul_dual((ap, bp_out), (a_s, bs_out), w_out_even, j, res=(xp, xs), mode="res",
                                 **TILES_K_MODEL)
        else:
            qkvp, qkvs = matmul_dual(hp, hs, w_qkv, j, bias=b_qkv, **TILES_K_MODEL)
            op = attn_seq(qkvp, attn_sinks[j], batch=bp, out_dtype=BF16)
            kv = qkvp.reshape(bp, lp, -1)[:, lp - keep:, nq:]
            p_out[3].append(kv[..., :nkv].reshape(bp, keep, N_KV_HEADS, HEAD_DIM))
            p_out[4].append(kv[..., nkv:].reshape(bp, keep, N_KV_HEADS, HEAD_DIM))
            o_s, kk, vv = attn_step(qkvs, cache_k, cache_v, j, attn_sinks[j], out_dtype=BF16)
            s_out[3].append(kk)
            s_out[4].append(vv)
            xp, xs = matmul_dual(op, o_s, w_o, j, bias=b_o, res=(xp, xs), mode="res", **TILES_K_MODEL)
        hp, hs = rmsnorm(xp, norm_ffn_g[l], BF16), rmsnorm(xs, norm_ffn_g[l], BF16)
        ap, fcp, a_s, fcs = ffn_gate_up(hp, hs, w_ffn_gate, w_ffn_up, ffn_conv_w, state_ffn_conv, l, batch=bp)
        p_out[5].append(fcp[:, 8 - (CONV_W - 1):, :])
        s_out[5].append(fcs)
        xp, xs = matmul_dual(ap, a_s, w_down_bf16, l, res=(xp, xs), mode="res", **TILES_K_FF)
    yp, ys = rmsnorm(xp, norm_final_g, F32), rmsnorm(xs, norm_final_g, F32)
    return (yp.reshape(bp, lp, d_model), ys.reshape(bs, 1, d_model), *[jnp.stack(t) for t in p_out],
            *[jnp.stack(t) for t in s_out])
```

```python
import functools

import jax
import jax.numpy as jnp
from jax import lax
from jax.experimental import pallas as pl
from jax.experimental.pallas import tpu as pltpu

F32 = jnp.float32
BF16 = jnp.bfloat16

DEPTH = 4
SSM_GROUP = 16
SSM_STATE = 64
GROUPS_PER_BLOCK = 8
S5_SUBSEQ = 8
CONV_W = 3
HEAD_DIM = 128
N_KV_HEADS = 8
KV_GROUP = 4
WINDOW = 128
RMS_EPS = 1e-5
NEG_INF = -1e30

VMEM_LIMIT_BYTES = 62 * 1024 * 1024

TILES_K_MODEL = dict(tm=512, tn=1024)
TILES_K_FF = dict(tm=512, tn=512)


def _params(*sem):
    return pltpu.CompilerParams(dimension_semantics=sem, vmem_limit_bytes=VMEM_LIMIT_BYTES)


def _rmsnorm_kernel(x_ref, g_ref, o_ref):
    x = x_ref[...]
    y = x * lax.rsqrt(jnp.mean(x * x, axis=-1, keepdims=True) + RMS_EPS)
    o_ref[...] = (y * g_ref[...]).astype(o_ref.dtype)


def rmsnorm(x, g, out_dtype):
    m, d = x.shape
    tm = min(m, 512)
    return pl.pallas_call(
        _rmsnorm_kernel,
        grid=(m // tm,),
        in_specs=[pl.BlockSpec((tm, d), lambda i: (i, 0)), pl.BlockSpec((1, d), lambda i: (0, 0))],
        out_specs=pl.BlockSpec((tm, d), lambda i: (i, 0)),
        out_shape=jax.ShapeDtypeStruct((m, d), out_dtype),
        compiler_params=_params("parallel"),
        name="rmsnorm",
    )(x, g.reshape(1, d))


def _mm_dual_kernel(*refs, nx, has_bias, mode, cast):
    it = iter(refs)
    xp_refs = [next(it) for _ in range(nx)]
    xs_refs = [next(it) for _ in range(nx)]
    w_ref = next(it)
    b_ref = next(it) if has_bias else None
    rp_ref, rs_ref = (next(it), next(it)) if mode != "plain" else (None, None)
    op_ref, os_ref = next(it), next(it)
    wbf_ref = next(it) if cast else w_ref

    def run(x_refs, r_ref, o_ref):
        acc, k0 = None, 0
        for x_ref in x_refs:
            k1 = k0 + x_ref.shape[1]
            part = jnp.dot(x_ref[...].astype(BF16), wbf_ref[k0:k1, :], preferred_element_type=F32)
            acc = part if acc is None else acc + part
            k0 = k1
        if has_bias:
            acc = acc + b_ref[...]
        if mode == "res":
            acc = r_ref[...] + acc
        elif mode == "glu":
            acc = r_ref[...] * jax.nn.sigmoid(acc)
        o_ref[...] = acc.astype(o_ref.dtype)

    @pl.when(pl.program_id(1) == 0)
    def _():
        if cast:
            wbf_ref[...] = w_ref[...].astype(BF16)
        run(xs_refs, rs_ref, os_ref)

    run(xp_refs, rp_ref, op_ref)


def matmul_dual(xp, xs, w, layer, *, bias=None, res=None, mode="plain", out_dtype=F32, tm=1024, tn=512):
    xps = xp if isinstance(xp, tuple) else (xp,)
    xss = xs if isinstance(xs, tuple) else (xs,)
    mp, ms = xps[0].shape[0], xss[0].shape[0]
    kdim, n = w.shape[1], w.shape[2]
    tm, tn = min(tm, mp), min(tn, n)
    cast = w.dtype != BF16
    in_specs = ([pl.BlockSpec((tm, x.shape[1]), lambda j, i: (i, 0)) for x in xps]
                + [pl.BlockSpec((ms, x.shape[1]), lambda j, i: (0, 0)) for x in xss]
                + [pl.BlockSpec((None, kdim, tn), lambda j, i: (layer, 0, j))])
    args = [*xps, *xss, w]
    if bias is not None:
        in_specs.append(pl.BlockSpec((None, 1, tn), lambda j, i: (layer, 0, j)))
        args.append(bias.reshape(bias.shape[0], 1, n))
    if mode != "plain":
        in_specs += [pl.BlockSpec((tm, tn), lambda j, i: (i, j)), pl.BlockSpec((ms, tn), lambda j, i: (0, j))]
        args += list(res)
    return pl.pallas_call(
        functools.partial(_mm_dual_kernel, nx=len(xps), has_bias=bias is not None, mode=mode, cast=cast),
        grid=(n // tn, mp // tm),
        in_specs=in_specs,
        out_specs=[pl.BlockSpec((tm, tn), lambda j, i: (i, j)), pl.BlockSpec((ms, tn), lambda j, i: (0, j))],
        out_shape=[jax.ShapeDtypeStruct((mp, n), out_dtype), jax.ShapeDtypeStruct((ms, n), out_dtype)],
        scratch_shapes=[pltpu.VMEM((kdim, tn), BF16)] if cast else [],
        compiler_params=_params("parallel", "arbitrary"),
        name="matmul_dual_" + mode,
    )(*args)


def _conv3_rows(g, w_ref, carry_ref):
    c1 = carry_ref[7:8, :]
    c2 = carry_ref[6:7, :]
    row = lax.broadcasted_iota(jnp.int32, g.shape, 0)
    g1 = jnp.where(row == 0, c1, pltpu.roll(g, 1, axis=0))
    g2 = jnp.where(row == 0, c2, jnp.where(row == 1, c1, pltpu.roll(g, 2, axis=0)))
    carry_ref[...] = g[g.shape[0] - 8:, :]
    return w_ref[0:1, :] * g2 + w_ref[1:2, :] * g1 + w_ref[2:3, :] * g


def _ffn_gu_kernel(xp_ref, xs_ref, wg_ref, wu_ref, cw_ref, prev_ref, ap_ref, stp_ref, as_ref, sts_ref,
                   wgb_ref, wub_ref, carry_ref, *, tiles_per_seq, nchunk):
    i = pl.program_id(1)

    @pl.when(i == 0)
    def _():
        wgb_ref[...] = wg_ref[...].astype(BF16)
        wub_ref[...] = wu_ref[...].astype(BF16)
        xs = xs_ref[...]
        gs = jnp.dot(xs, wgb_ref[...], preferred_element_type=F32)
        us = jnp.dot(xs, wub_ref[...], preferred_element_type=F32)
        prev = prev_ref[...]
        x2, x1 = prev[:, 0, :], prev[:, 1, :]
        y = cw_ref[0:1, :] * x2 + cw_ref[1:2, :] * x1 + cw_ref[2:3, :] * gs
        as_ref[...] = (jax.nn.silu(y) * us).astype(as_ref.dtype)
        sts_ref[:, 0, :] = x1
        sts_ref[:, 1, :] = gs

    @pl.when(i % tiles_per_seq == 0)
    def _():
        carry_ref[...] = jnp.zeros_like(carry_ref)

    rc = xp_ref.shape[0] // nchunk
    for c in range(nchunk):
        x = xp_ref[c * rc:(c + 1) * rc, :]
        g = jnp.dot(x, wgb_ref[...], preferred_element_type=F32)
        u = jnp.dot(x, wub_ref[...], preferred_element_type=F32)
        y = _conv3_rows(g, cw_ref, carry_ref)
        ap_ref[c * rc:(c + 1) * rc, :] = (jax.nn.silu(y) * u).astype(ap_ref.dtype)

    @pl.when(i % tiles_per_seq == tiles_per_seq - 1)
    def _():
        stp_ref[...] = carry_ref[...]


def ffn_gate_up(xp, xs, wg, wu, cw, prev_s, layer, *, batch, tm=2048, tn=256, nchunk=4):
    mp, d = xp.shape
    ms = xs.shape[0]
    f = wg.shape[2]
    tm = min(tm, mp // batch)
    tiles_per_seq = mp // batch // tm
    return pl.pallas_call(
        functools.partial(_ffn_gu_kernel, tiles_per_seq=tiles_per_seq, nchunk=nchunk),
        grid=(f // tn, mp // tm),
        in_specs=[pl.BlockSpec((tm, d), lambda j, i: (i, 0)),
                  pl.BlockSpec((ms, d), lambda j, i: (0, 0)),
                  pl.BlockSpec((None, d, tn), lambda j, i: (layer, 0, j)),
                  pl.BlockSpec((None, d, tn), lambda j, i: (layer, 0, j)),
                  pl.BlockSpec((None, CONV_W, tn), lambda j, i: (layer, 0, j)),
                  pl.BlockSpec((None, ms, CONV_W - 1, tn), lambda j, i: (layer, 0, 0, j))],
        out_specs=[pl.BlockSpec((tm, tn), lambda j, i: (i, j)),
                   pl.BlockSpec((None, 8, tn), lambda j, i: (i // tiles_per_seq, 0, j)),
                   pl.BlockSpec((ms, tn), lambda j, i: (0, j)),
                   pl.BlockSpec((ms, CONV_W - 1, tn), lambda j, i: (0, 0, j))],
        out_shape=[jax.ShapeDtypeStruct((mp, f), BF16),
                   jax.ShapeDtypeStruct((batch, 8, f), F32),
                   jax.ShapeDtypeStruct((ms, f), BF16),
                   jax.ShapeDtypeStruct((ms, CONV_W - 1, f), F32)],
        scratch_shapes=[pltpu.VMEM((d, tn), BF16), pltpu.VMEM((d, tn), BF16), pltpu.VMEM((8, tn), F32)],
        compiler_params=_params("parallel", "arbitrary"),
        name="ffn_gate_up",
    )(xp, xs, wg, wu, cw, prev_s)


def _conv_seq_kernel(*refs, mul_in, act):
    it = iter(refs)
    p_ref = next(it)
    q_ref = next(it) if mul_in else None
    r_ref, w_ref, o_ref, st_ref, carry_ref = next(it), next(it), next(it), next(it), next(it)
    m = p_ref[...] * q_ref[...] if mul_in else p_ref[...]

    @pl.when(pl.program_id(2) == 0)
    def _():
        carry_ref[...] = jnp.zeros_like(carry_ref)

    y = _conv3_rows(m, w_ref, carry_ref)
    if act == "silu":
        y = jax.nn.silu(y)
    o_ref[...] = (r_ref[...] * y).astype(o_ref.dtype)
    st_ref[...] = carry_ref[...]


def conv_seq(p, q, r, w, *, batch, ncols, act, out_dtype, tt=512, tc=512):
    mul_in = q is not None
    m_rows = p[0].shape[0]
    tt = min(tt, m_rows // batch)
    nt = m_rows // batch // tt
    nc = ncols // tc

    def spec(c0):
        b0 = c0 // tc
        return pl.BlockSpec((tt, tc), lambda j, b, t: (b * nt + t, b0 + j))

    in_specs, args = [spec(p[1])], [p[0]]
    if mul_in:
        in_specs.append(spec(q[1]))
        args.append(q[0])
    in_specs += [spec(r[1]), pl.BlockSpec((CONV_W, tc), lambda j, b, t: (0, j))]
    args += [r[0], w]
    return pl.pallas_call(
        functools.partial(_conv_seq_kernel, mul_in=mul_in, act=act),
        grid=(nc, batch, nt),
        in_specs=in_specs,
        out_specs=[pl.BlockSpec((tt, tc), lambda j, b, t: (b * nt + t, j)),
                   pl.BlockSpec((None, 8, tc), lambda j, b, t: (b, 0, j))],
        out_shape=[jax.ShapeDtypeStruct((m_rows, ncols), out_dtype),
                   jax.ShapeDtypeStruct((batch, 8, ncols), F32)],
        scratch_shapes=[pltpu.VMEM((8, tc), F32)],
        compiler_params=_params("parallel", "parallel", "arbitrary"),
        name="conv_seq_" + act,
    )(*args)


def _conv_step_kernel(*refs, mul_in, act):
    it = iter(refs)
    p_ref = next(it)
    q_ref = next(it) if mul_in else None
    r_ref, w_ref, prev_ref, o_ref, st_ref = next(it), next(it), next(it), next(it), next(it)
    m = p_ref[...] * q_ref[...] if mul_in else p_ref[...]
    prev = prev_ref[...]
    x2, x1 = prev[:, 0, :], prev[:, 1, :]
    y = w_ref[0:1, :] * x2 + w_ref[1:2, :] * x1 + w_ref[2:3, :] * m
    if act == "silu":
        y = jax.nn.silu(y)
    o_ref[...] = (r_ref[...] * y).astype(o_ref.dtype)
    st_ref[:, 0, :] = x1
    st_ref[:, 1, :] = m


def conv_step(p, q, r, w, prev, *, ncols, act, out_dtype, tc=512):
    mul_in = q is not None
    rows = p[0].shape[0]
    nc = ncols // tc

    def spec(c0):
        b0 = c0 // tc
        return pl.BlockSpec((rows, tc), lambda j: (0, b0 + j))

    in_specs, args = [spec(p[1])], [p[0]]
    if mul_in:
        in_specs.append(spec(q[1]))
        args.append(q[0])
    in_specs += [spec(r[1]), pl.BlockSpec((CONV_W, tc), lambda j: (0, j)),
                 pl.BlockSpec((rows, CONV_W - 1, tc), lambda j: (0, 0, j))]
    args += [r[0], w, prev]
    return pl.pallas_call(
        functools.partial(_conv_step_kernel, mul_in=mul_in, act=act),
        grid=(nc,),
        in_specs=in_specs,
        out_specs=[pl.BlockSpec((rows, tc), lambda j: (0, j)),
                   pl.BlockSpec((rows, CONV_W - 1, tc), lambda j: (0, 0, j))],
        out_shape=[jax.ShapeDtypeStruct((rows, ncols), out_dtype),
                   jax.ShapeDtypeStruct((rows, CONV_W - 1, ncols), F32)],
        compiler_params=_params("parallel"),
        name="conv_step_" + act,
    )(*args)


def _cast_slab_specs(w, layer, nslabs):
    rows, cols = w.shape[1:]
    slab = rows // nslabs
    assert slab * nslabs == rows and slab % 16 == 0
    return (pl.BlockSpec((None, slab, cols), lambda o, i: (layer, o, 0)),
            pl.BlockSpec((slab, cols), lambda o, i: (o, 0)),
            jax.ShapeDtypeStruct((rows, cols), BF16))


def _cast_slab(src_ref, dst_ref):
    @pl.when(pl.program_id(1) == 0)
    def _():
        dst_ref[...] = src_ref[...].astype(BF16)


def _s5_prep_kernel(lr_ref, li_ref, ldt_ref, br_ref, bi_ref, pow_ref, bb_ref):
    lr, li = lr_ref[...], li_ref[...]
    dt = jnp.exp(ldt_ref[...])
    mag = jnp.exp(lr * dt)
    ar, ai = mag * jnp.cos(li * dt), mag * jnp.sin(li * dt)
    den = lr * lr + li * li
    cr = ((ar - 1.0) * lr + ai * li) / den
    ci = (ai * lr - (ar - 1.0) * li) / den
    br, bi = br_ref[...], bi_ref[...]
    bb_ref[:, 0:SSM_GROUP, :] = cr * br - ci * bi
    bb_ref[:, SSM_GROUP:2 * SSM_GROUP, :] = cr * bi + ci * br
    pr, pi = ar, ai
    for k in range(8):
        pow_ref[:, k:k + 1, :] = pr
        pow_ref[:, 8 + k:9 + k, :] = pi
        pr, pi = pr * ar - pi * ai, pr * ai + pi * ar


def s5_prepare(lam_re, lam_im, log_dt, b_re, b_im, c_re, c_im, d_skip):
    g, p = lam_re.shape
    i = SSM_GROUP
    nb, gb = g // GROUPS_PER_BLOCK, GROUPS_PER_BLOCK
    pw, bb = pl.pallas_call(
        _s5_prep_kernel,
        out_shape=[jax.ShapeDtypeStruct((g, 16, p), F32), jax.ShapeDtypeStruct((g, 2 * i, p), F32)],
        name="s5_prep",
    )(lam_re.reshape(g, 1, p), lam_im.reshape(g, 1, p), log_dt.reshape(g, 1, 1),
      b_re.transpose(0, 2, 1), b_im.transpose(0, 2, 1))

    def lanes(a):
        return a.reshape(nb, gb, 8, p).transpose(0, 2, 1, 3).reshape(nb, 8, gb * p)

    pr, pi = lanes(pw[:, 0:8]), lanes(pw[:, 8:16])

    def bcast(a, k):
        return jnp.broadcast_to(a[:, k - 1:k, :], a.shape)

    tab = jnp.concatenate([pr, pi, bcast(pr, 1), bcast(pi, 1), bcast(pr, 8), bcast(pi, 8)], axis=1)
    eye = jnp.eye(gb, dtype=F32)

    def wb_part(a):
        a = a.reshape(nb, gb, i, p)
        return jnp.einsum('jaip,ab->jaibp', a, eye).reshape(nb, gb * i, gb * p)

    def wc_part(a):
        a = a.reshape(nb, gb, i, p)
        return jnp.einsum('jaip,ab->jbpai', a, eye).reshape(nb, gb * p, gb * i)

    wb = jnp.concatenate([wb_part(bb[:, :i]), wb_part(bb[:, i:])], axis=2).astype(BF16)
    wc = jnp.concatenate([wc_part(c_re.astype(F32)), wc_part(-c_im.astype(F32))], axis=1).astype(BF16)
    return tab, wb, wc, d_skip.reshape(nb, 1, gb * i)


_T_P, _T_A1, _T_A8 = 0, 16, 32


def _s5_seq_kernel(u_ref, wb_ref, wc_ref, tab_ref, d_ref, wsrc_ref, g_ref, st_ref, wdst_ref, h_ref, pw_ref, *,
                   chunk):
    _cast_slab(wsrc_ref, wdst_ref)
    seq, width = h_ref.shape
    half, lj = width // 2, seq // S5_SUBSEQ
    re, im = slice(0, half), slice(half, width)

    def tab(off):
        return tab_ref[off:off + 8, :], tab_ref[off + 8:off + 16, :]

    @pl.when(pl.program_id(1) == 0)
    def _():
        a8r, a8i = tab(_T_A8)

        def fill(m, c):
            pr, pi = c
            r0 = pl.multiple_of(m * 8, 8)
            pw_ref[pl.ds(r0, 8), re] = pr
            pw_ref[pl.ds(r0, 8), im] = pi
            return pr * a8r - pi * a8i, pr * a8i + pi * a8r

        lax.fori_loop(0, lj // 8, fill, tab(_T_P))

    for c in range(seq // chunk):
        rows = slice(c * chunk, (c + 1) * chunk)
        h_ref[rows, :] = jnp.dot(u_ref[rows, :].astype(BF16), wb_ref[...], preferred_element_type=F32)

    a1r, a1i = tab(_T_A1)

    def pass1(j, c):
        hr, hi = c
        r0 = pl.multiple_of(j * 8, 8)
        hr, hi = (a1r * hr - a1i * hi) + h_ref[pl.ds(r0, 8), re], (a1r * hi + a1i * hr) + h_ref[pl.ds(r0, 8), im]
        h_ref[pl.ds(r0, 8), re] = hr
        h_ref[pl.ds(r0, 8), im] = hi
        return hr, hi

    zero = jnp.zeros((S5_SUBSEQ, half), F32)
    xr, xi = lax.fori_loop(0, lj, pass1, (zero, zero), unroll=4)

    row = lax.broadcasted_iota(jnp.int32, (S5_SUBSEQ, half), 0)
    qr, qi = pw_ref[lj - 1:lj, re], pw_ref[lj - 1:lj, im]
    for k in (1, 2, 4):
        sr = jnp.where(row >= k, pltpu.roll(xr, k, axis=0), 0.0)
        si = jnp.where(row >= k, pltpu.roll(xi, k, axis=0), 0.0)
        xr, xi = xr + (qr * sr - qi * si), xi + (qr * si + qi * sr)
        qr, qi = qr * qr - qi * qi, 2.0 * (qr * qi)
    st_ref[:, re] = jnp.broadcast_to(xr[S5_SUBSEQ - 1:, :], xr.shape)
    st_ref[:, im] = jnp.broadcast_to(xi[S5_SUBSEQ - 1:, :], xi.shape)
    er = jnp.where(row >= 1, pltpu.roll(xr, 1, axis=0), 0.0)
    ei = jnp.where(row >= 1, pltpu.roll(xi, 1, axis=0), 0.0)

    def pass2(j, _):
        r0 = pl.multiple_of(j * 8, 8)
        pr = jnp.broadcast_to(pw_ref[pl.ds(j, 1), re], er.shape)
        pi = jnp.broadcast_to(pw_ref[pl.ds(j, 1), im], er.shape)
        h_ref[pl.ds(r0, 8), re] += pr * er - pi * ei
        h_ref[pl.ds(r0, 8), im] += pr * ei + pi * er
        return 0

    lax.fori_loop(0, lj, pass2, 0, unroll=4)
    for c in range(seq // chunk):
        rows = slice(c * chunk, (c + 1) * chunk)
        y = jnp.dot(h_ref[rows, :].astype(BF16), wc_ref[...], preferred_element_type=F32)
        g_ref[rows, :] = jax.nn.gelu(y + d_ref[...] * u_ref[rows, :])


def s5_seq(z, prep, w_cast, layer, *, batch):
    tab, wb, wc, d = prep
    nb, cin, cst = wb.shape
    cast_in, cast_out, cast_shape = _cast_slab_specs(w_cast, layer, nb)
    m_rows = z.shape[0]
    seq = m_rows // batch
    lj = seq // S5_SUBSEQ

    def to_kernel_order(a):
        return a.reshape(batch, S5_SUBSEQ, lj, -1).transpose(0, 2, 1, 3).reshape(m_rows, -1)

    def to_time_order(a):
        return a.reshape(batch, lj, S5_SUBSEQ, -1).transpose(0, 2, 1, 3).reshape(m_rows, -1)

    g, st, w_bf16 = pl.pallas_call(
        functools.partial(_s5_seq_kernel, chunk=min(seq, 512)),
        grid=(nb, batch),
        in_specs=[pl.BlockSpec((seq, cin), lambda j, b: (b, j)),
                  pl.BlockSpec((None, cin, cst), lambda j, b: (j, 0, 0)),
                  pl.BlockSpec((None, cst, cin), lambda j, b: (j, 0, 0)),
                  pl.BlockSpec((None, tab.shape[1], cst // 2), lambda j, b: (j, 0, 0)),
                  pl.BlockSpec((None, 1, cin), lambda j, b: (j, 0, 0)), cast_in],
        out_specs=[pl.BlockSpec((seq, cin), lambda j, b: (b, j)),
                   pl.BlockSpec((None, None, 8, cst), lambda j, b: (j, b, 0, 0)), cast_out],
        out_shape=[jax.ShapeDtypeStruct((m_rows, nb * cin), F32),
                   jax.ShapeDtypeStruct((nb, batch, 8, cst), F32), cast_shape],
        scratch_shapes=[pltpu.VMEM((seq, cst), F32), pltpu.VMEM((lj, cst), F32)],
        compiler_params=_params("parallel", "arbitrary"),
        name="s5_seq",
    )(to_kernel_order(z[:, :nb * cin]), wb, wc, tab, d, w_cast)
    return to_time_order(g), st, w_bf16


def _s5_step_kernel(u_ref, s_ref, wb_ref, wc_ref, tab_ref, d_ref, g_ref, st_ref):
    half = s_ref.shape[1] // 2
    u = u_ref[...]
    bu = jnp.dot(u.astype(BF16), wb_ref[...], preferred_element_type=F32)
    ar, ai = tab_ref[_T_P:_T_P + 1, :], tab_ref[_T_P + 8:_T_P + 9, :]
    sr, si = s_ref[:, 0:half], s_ref[:, half:2 * half]
    hr = ar * sr - ai * si + bu[:, 0:half]
    hi = ar * si + ai * sr + bu[:, half:2 * half]
    st_ref[:, 0:half] = hr
    st_ref[:, half:2 * half] = hi
    h = jnp.concatenate([hr, hi], axis=1)
    y = jnp.dot(h.astype(BF16), wc_ref[...], preferred_element_type=F32) + d_ref[...] * u
    g_ref[...] = jax.nn.gelu(y)


def s5_step(z, state, prep):
    tab, wb, wc, d = prep
    nb, cin, cst = wb.shape
    rows = z.shape[0]
    return pl.pallas_call(
        _s5_step_kernel,
        grid=(nb,),
        in_specs=[pl.BlockSpec((rows, cin), lambda j: (0, j)),
                  pl.BlockSpec((None, rows, cst), lambda j: (j, 0, 0)),
                  pl.BlockSpec((None, cin, cst), lambda j: (j, 0, 0)),
                  pl.BlockSpec((None, cst, cin), lambda j: (j, 0, 0)),
                  pl.BlockSpec((None, tab.shape[1], cst // 2), lambda j: (j, 0, 0)),
                  pl.BlockSpec((None, 1, cin), lambda j: (j, 0, 0))],
        out_specs=[pl.BlockSpec((rows, cin), lambda j: (0, j)),
                   pl.BlockSpec((None, rows, cst), lambda j: (j, 0, 0))],
        out_shape=[jax.ShapeDtypeStruct((rows, nb * cin), F32),
                   jax.ShapeDtypeStruct((nb, rows, cst), F32)],
        compiler_params=_params("parallel"),
        name="s5_step",
    )(z, state, wb, wc, tab, d)


def _state_to_blocks(s_re, s_im):
    b, g, p = s_re.shape
    nb = g // GROUPS_PER_BLOCK

    def f(s):
        return s.reshape(b, nb, GROUPS_PER_BLOCK * p).transpose(1, 0, 2)

    return jnp.concatenate([f(s_re), f(s_im)], axis=2)


def _blocks_to_state(st, p=SSM_STATE):
    nb, b, c = st.shape
    half = c // 2

    def f(s):
        return s.transpose(1, 0, 2).reshape(b, nb * half // p, p)

    return f(st[:, :, :half]), f(st[:, :, half:])


def _attn_seq_kernel(q_ref, kp_ref, kc_ref, vp_ref, vc_ref, sk_ref, wsrc_ref, o_ref, wdst_ref):
    _cast_slab(wsrc_ref, wdst_ref)
    hd, blk = HEAD_DIM, WINDOW
    n = pl.program_id(0)
    shape = (KV_GROUP * blk, 2 * blk)
    qi = lax.broadcasted_iota(jnp.int32, shape, 0) % blk
    kj = lax.broadcasted_iota(jnp.int32, shape, 1)
    vis = (kj >= qi) & (kj <= qi + WINDOW) & ((kj >= blk) | (n > 0))
    for h in range(N_KV_HEADS):
        c0 = h * KV_GROUP * hd
        qa = jnp.concatenate([q_ref[:, c0 + g * hd:c0 + (g + 1) * hd] for g in range(KV_GROUP)], axis=0)
        kb = jnp.concatenate([kp_ref[:, h * hd:(h + 1) * hd], kc_ref[:, h * hd:(h + 1) * hd]], axis=0)
        vb = jnp.concatenate([vp_ref[:, h * hd:(h + 1) * hd], vc_ref[:, h * hd:(h + 1) * hd]], axis=0)
        s = lax.dot_general(qa.astype(BF16), kb.astype(BF16), (((1,), (1,)), ((), ())),
                            preferred_element_type=F32) * (hd ** -0.5)
        s = jnp.where(vis, s, NEG_INF)
        sk = sk_ref[h]
        m = jnp.maximum(jnp.max(s, axis=-1, keepdims=True), sk)
        p = jnp.exp(s - m)
        w = p * (1.0 / (jnp.sum(p, axis=-1, keepdims=True) + jnp.exp(sk - m)))
        o = jnp.dot(w.astype(BF16), vb.astype(BF16), preferred_element_type=F32)
        for g in range(KV_GROUP):
            o_ref[:, c0 + g * hd:c0 + (g + 1) * hd] = o[g * blk:(g + 1) * blk].astype(o_ref.dtype)


def attn_seq(qkv, sinks, w_cast, layer, *, batch, out_dtype):
    m_rows = qkv.shape[0]
    nblk = m_rows // batch // WINDOW
    nq = N_KV_HEADS * KV_GROUP
    qw, kvw = nq * HEAD_DIM, N_KV_HEADS * HEAD_DIM
    kblk, vblk = qw // kvw, qw // kvw + 1
    sk = jnp.repeat(sinks.astype(F32).reshape(N_KV_HEADS, KV_GROUP), WINDOW, axis=1)[..., None]

    cast_in, cast_out, cast_shape = _cast_slab_specs(w_cast, layer, nblk)

    def cur(c):
        return pl.BlockSpec((WINDOW, kvw), lambda n, b: (b * nblk + n, c))

    def prev(c):
        return pl.BlockSpec((WINDOW, kvw), lambda n, b: (b * nblk + jnp.maximum(n - 1, 0), c))

    return pl.pallas_call(
        _attn_seq_kernel,
        grid=(nblk, batch),
        in_specs=[pl.BlockSpec((WINDOW, qw), lambda n, b: (b * nblk + n, 0)),
                  prev(kblk), cur(kblk), prev(vblk), cur(vblk),
                  pl.BlockSpec((N_KV_HEADS, KV_GROUP * WINDOW, 1), lambda n, b: (0, 0, 0)), cast_in],
        out_specs=[pl.BlockSpec((WINDOW, qw), lambda n, b: (b * nblk + n, 0)), cast_out],
        out_shape=[jax.ShapeDtypeStruct((m_rows, qw), out_dtype), cast_shape],
        compiler_params=_params("parallel", "arbitrary"),
        name="attn_seq",
    )(qkv, qkv, qkv, qkv, qkv, sk, w_cast)


def _attn_step_kernel(q_ref, kn_ref, vn_ref, kc_ref, vc_ref, sk_ref, o_ref):
    scale = HEAD_DIM ** -0.5
    q = q_ref[...].astype(BF16)
    s = lax.dot_general(q, kc_ref[...].astype(BF16), (((1,), (1,)), ((), ())),
                        preferred_element_type=F32) * scale
    own = (lax.broadcasted_iota(jnp.int32, s.shape, 1) % N_KV_HEADS
           == lax.broadcasted_iota(jnp.int32, s.shape, 0) // KV_GROUP)
    s = jnp.where(own, s, NEG_INF)
    sn = jnp.sum(q.astype(F32) * kn_ref[...].astype(BF16).astype(F32), axis=-1, keepdims=True) * scale
    sk = sk_ref[...]
    m = jnp.maximum(jnp.maximum(jnp.max(s, axis=-1, keepdims=True), sn), sk)
    p, pn = jnp.exp(s - m), jnp.exp(sn - m)
    r = 1.0 / (jnp.sum(p, axis=-1, keepdims=True) + pn + jnp.exp(sk - m))
    o = jnp.dot((p * r).astype(BF16), vc_ref[...].astype(BF16), preferred_element_type=F32)
    o = o + (pn * r).astype(BF16).astype(F32) * vn_ref[...].astype(BF16).astype(F32)
    o_ref[...] = o.astype(o_ref.dtype)


def attn_step(qkv, cache_k, cache_v, layer, sinks, *, out_dtype):
    rows = qkv.shape[0]
    nq = N_KV_HEADS * KV_GROUP
    layers, wb = cache_k.shape[0], cache_k.shape[2]
    q3 = qkv[:, :nq * HEAD_DIM].reshape(rows, nq, HEAD_DIM)
    kn = qkv[:, nq * HEAD_DIM:(nq + N_KV_HEADS) * HEAD_DIM].reshape(rows, N_KV_HEADS, HEAD_DIM)
    vn = qkv[:, (nq + N_KV_HEADS) * HEAD_DIM:].reshape(rows, N_KV_HEADS, HEAD_DIM)
    nkeys = wb * N_KV_HEADS
    head_spec = pl.BlockSpec((None, nq, HEAD_DIM), lambda b: (b, 0, 0))
    cache_spec = pl.BlockSpec((None, None, nkeys, HEAD_DIM), lambda b: (layer, b, 0, 0))
    o = pl.pallas_call(
        _attn_step_kernel,
        grid=(rows,),
        in_specs=[head_spec, head_spec, head_spec, cache_spec, cache_spec,
                  pl.BlockSpec((nq, 1), lambda b: (0, 0))],
        out_specs=head_spec,
        out_shape=jax.ShapeDtypeStruct((rows, nq, HEAD_DIM), out_dtype),
        compiler_params=_params("parallel"),
        name="attn_step",
    )(q3, jnp.repeat(kn, KV_GROUP, axis=1), jnp.repeat(vn, KV_GROUP, axis=1),
      cache_k.reshape(layers, rows, nkeys, HEAD_DIM), cache_v.reshape(layers, rows, nkeys, HEAD_DIM),
      sinks.astype(F32).reshape(nq, 1))
    new_k = jnp.concatenate([cache_k[layer, :, 1:], kn[:, None]], axis=1)
    new_v = jnp.concatenate([cache_v[layer, :, 1:], vn[:, None]], axis=1)
    return o.reshape(rows, nq * HEAD_DIM), new_k, new_v


def kernel(x_prompt, x_sample, state_ssm_re, state_ssm_im, state_sconv, cache_k, cache_v, state_ffn_conv, norm_mix_g, norm_ffn_g, norm_final_g, w_in_even, ssm_lambda_re, ssm_lambda_im, ssm_log_dt, ssm_b_re, ssm_b_im, ssm_c_re, ssm_c_im, ssm_d, w_glu, b_glu, sconv_w, w_out_even, w_qkv, b_qkv, attn_sinks, w_o, b_o, w_ffn_gate, w_ffn_up, ffn_conv_w, w_ffn_down):
    bp, lp, d_model = x_prompt.shape
    bs = x_sample.shape[0]
    d_ssm, d_conv = ssm_d.shape[1], sconv_w.shape[2]
    nq, nkv = N_KV_HEADS * KV_GROUP * HEAD_DIM, N_KV_HEADS * HEAD_DIM
    keep = min(WINDOW, lp)
    xp, xs = x_prompt.reshape(bp * lp, d_model), x_sample.reshape(bs, d_model)
    p_out = [[] for _ in range(6)]
    s_out = [[] for _ in range(6)]
    for l in range(DEPTH):
        j = l // 2
        hp, hs = rmsnorm(xp, norm_mix_g[l], BF16), rmsnorm(xs, norm_mix_g[l], BF16)
        if l % 2 == 0:
            prep = s5_prepare(ssm_lambda_re[j], ssm_lambda_im[j], ssm_log_dt[j], ssm_b_re[j], ssm_b_im[j],
                              ssm_c_re[j], ssm_c_im[j], ssm_d[j])
            zp, zs = matmul_dual(hp, hs, w_in_even, j, **TILES_K_MODEL)
            cv, cb, cc = d_ssm, d_ssm + d_conv, d_ssm + 2 * d_conv
            gp, stp, w_down = s5_seq(zp, prep, w_ffn_down, l, batch=bp)
            gs, sts = s5_step(zs, _state_to_blocks(state_ssm_re[j], state_ssm_im[j]), prep)
            for out, st in ((p_out, stp[:, :, 0, :]), (s_out, sts)):
                re, im = _blocks_to_state(st)
                out[0].append(re)
                out[1].append(im)
            bp_out, scp = conv_seq((zp, cc), (zp, cv), (zp, cb), sconv_w[j], batch=bp, ncols=d_conv,
                                   act="none", out_dtype=BF16)
            bs_out, scs = conv_step((zs, cc), (zs, cv), (zs, cb), sconv_w[j], state_sconv[j], ncols=d_conv,
                                    act="none", out_dtype=BF16)
            p_out[2].append(scp[:, 8 - (CONV_W - 1):, :])
            s_out[2].append(scs)
            ap, a_s = matmul_dual(gp, gs, w_glu, j, bias=b_glu, res=(gp, gs), mode="glu", out_dtype=BF16)
            xp, xs = matmul_dual((ap, bp_out), (a_s, bs_out), w_out_even, j, res=(xp, xs), mode="res",
                                 **TILES_K_MODEL)
        else:
            qkvp, qkvs = matmul_dual(hp, hs, w_qkv, j, bias=b_qkv, **TILES_K_MODEL)
            op, w_down = attn_seq(qkvp, attn_sinks[j], w_ffn_down, l, batch=bp, out_dtype=BF16)
            kv = qkvp.reshape(bp, lp, -1)[:, lp - keep:, nq:]
            p_out[3].append(kv[..., :nkv].reshape(bp, keep, N_KV_HEADS, HEAD_DIM))
            p_out[4].append(kv[..., nkv:].reshape(bp, keep, N_KV_HEADS, HEAD_DIM))
            o_s, kk, vv = attn_step(qkvs, cache_k, cache_v, j, attn_sinks[j], out_dtype=BF16)
            s_out[3].append(kk)
            s_out[4].append(vv)
            xp, xs = matmul_dual(op, o_s, w_o, j, bias=b_o, res=(xp, xs), mode="res", **TILES_K_MODEL)
        hp, hs = rmsnorm(xp, norm_ffn_g[l], BF16), rmsnorm(xs, norm_ffn_g[l], BF16)
        ap, fcp, a_s, fcs = ffn_gate_up(hp, hs, w_ffn_gate, w_ffn_up, ffn_conv_w, state_ffn_conv, l, batch=bp)
        p_out[5].append(fcp[:, 8 - (CONV_W - 1):, :])
        s_out[5].append(fcs)
        xp, xs = matmul_dual(ap, a_s, w_down[None], 0, res=(xp, xs), mode="res", **TILES_K_FF)
    yp, ys = rmsnorm(xp, norm_final_g, F32), rmsnorm(xs, norm_final_g, F32)
    return (yp.reshape(bp, lp, d_model), ys.reshape(bs, 1, d_model), *[jnp.stack(t) for t in p_out],
            *[jnp.stack(t) for t in s_out])
```

```python
import functools

import jax
import jax.numpy as jnp
from jax import lax
from jax.experimental import pallas as pl
from jax.experimental.pallas import tpu as pltpu

F32 = jnp.float32
BF16 = jnp.bfloat16

DEPTH = 4
SSM_GROUP = 16
SSM_STATE = 64
GROUPS_PER_BLOCK = 8
S5_SUBSEQ = 8
CONV_W = 3
HEAD_DIM = 128
N_KV_HEADS = 8
KV_GROUP = 4
WINDOW = 128
RMS_EPS = 1e-5
NEG_INF = -1e30

VMEM_LIMIT_BYTES = 62 * 1024 * 1024

TILES_K_MODEL = dict(tm=512, tn=1024)
TILES_K_FF = dict(tm=512, tn=512)
TILES_GLU = dict(tm=512, tn=2048)


def _params(*sem):
    return pltpu.CompilerParams(dimension_semantics=sem, vmem_limit_bytes=VMEM_LIMIT_BYTES)


def _rmsnorm_kernel(x_ref, g_ref, o_ref):
    x = x_ref[...]
    y = x * lax.rsqrt(jnp.mean(x * x, axis=-1, keepdims=True) + RMS_EPS)
    o_ref[...] = (y * g_ref[...]).astype(o_ref.dtype)


def rmsnorm(x, g, out_dtype):
    m, d = x.shape
    tm = min(m, 512)
    return pl.pallas_call(
        _rmsnorm_kernel,
        grid=(m // tm,),
        in_specs=[pl.BlockSpec((tm, d), lambda i: (i, 0)), pl.BlockSpec((1, d), lambda i: (0, 0))],
        out_specs=pl.BlockSpec((tm, d), lambda i: (i, 0)),
        out_shape=jax.ShapeDtypeStruct((m, d), out_dtype),
        compiler_params=_params("parallel"),
        name="rmsnorm",
    )(x, g.reshape(1, d))


def _mm_dual_kernel(*refs, nx, has_bias, mode, cast):
    it = iter(refs)
    xp_refs = [next(it) for _ in range(nx)]
    xs_refs = [next(it) for _ in range(nx)]
    w_ref = next(it)
    b_ref = next(it) if has_bias else None
    rp_ref, rs_ref = (next(it), next(it)) if mode == "res" else (None, None)
    op_ref, os_ref = next(it), next(it)
    wbf_ref = next(it) if cast else w_ref

    def run(x_refs, r_ref, o_ref):
        acc, k0 = None, 0
        for x_ref in x_refs:
            k1 = k0 + x_ref.shape[1]
            part = jnp.dot(x_ref[...].astype(BF16), wbf_ref[k0:k1, :], preferred_element_type=F32)
            acc = part if acc is None else acc + part
            k0 = k1
        if has_bias:
            acc = acc + b_ref[...]
        if mode == "res":
            acc = r_ref[...] + acc
        elif mode == "glu":
            acc = x_refs[0][...] * jax.nn.sigmoid(acc)
        o_ref[...] = acc.astype(o_ref.dtype)

    @pl.when(pl.program_id(1) == 0)
    def _():
        if cast:
            wbf_ref[...] = w_ref[...].astype(BF16)
        run(xs_refs, rs_ref, os_ref)

    run(xp_refs, rp_ref, op_ref)


def matmul_dual(xp, xs, w, layer, *, bias=None, res=None, mode="plain", out_dtype=F32, tm=1024, tn=512):
    xps = xp if isinstance(xp, tuple) else (xp,)
    xss = xs if isinstance(xs, tuple) else (xs,)
    mp, ms = xps[0].shape[0], xss[0].shape[0]
    kdim, n = w.shape[1], w.shape[2]
    tm, tn = min(tm, mp), min(tn, n)
    cast = w.dtype != BF16
    assert mode != "glu" or (len(xps) == 1 and tn == n == kdim)
    in_specs = ([pl.BlockSpec((tm, x.shape[1]), lambda j, i: (i, 0)) for x in xps]
                + [pl.BlockSpec((ms, x.shape[1]), lambda j, i: (0, 0)) for x in xss]
                + [pl.BlockSpec((None, kdim, tn), lambda j, i: (layer, 0, j))])
    args = [*xps, *xss, w]
    if bias is not None:
        in_specs.append(pl.BlockSpec((None, 1, tn), lambda j, i: (layer, 0, j)))
        args.append(bias.reshape(bias.shape[0], 1, n))
    if mode == "res":
        in_specs += [pl.BlockSpec((tm, tn), lambda j, i: (i, j)), pl.BlockSpec((ms, tn), lambda j, i: (0, j))]
        args += list(res)
    return pl.pallas_call(
        functools.partial(_mm_dual_kernel, nx=len(xps), has_bias=bias is not None, mode=mode, cast=cast),
        grid=(n // tn, mp // tm),
        in_specs=in_specs,
        out_specs=[pl.BlockSpec((tm, tn), lambda j, i: (i, j)), pl.BlockSpec((ms, tn), lambda j, i: (0, j))],
        out_shape=[jax.ShapeDtypeStruct((mp, n), out_dtype), jax.ShapeDtypeStruct((ms, n), out_dtype)],
        scratch_shapes=[pltpu.VMEM((kdim, tn), BF16)] if cast else [],
        compiler_params=_params("parallel", "arbitrary"),
        name="matmul_dual_" + mode,
    )(*args)


def _conv3_rows(g, w_ref, carry_ref):
    c1 = carry_ref[7:8, :]
    c2 = carry_ref[6:7, :]
    row = lax.broadcasted_iota(jnp.int32, g.shape, 0)
    g1 = jnp.where(row == 0, c1, pltpu.roll(g, 1, axis=0))
    g2 = jnp.where(row == 0, c2, jnp.where(row == 1, c1, pltpu.roll(g, 2, axis=0)))
    carry_ref[...] = g[g.shape[0] - 8:, :]
    return w_ref[0:1, :] * g2 + w_ref[1:2, :] * g1 + w_ref[2:3, :] * g


def _ffn_gu_kernel(xp_ref, xs_ref, wg_ref, wu_ref, cw_ref, prev_ref, ap_ref, stp_ref, as_ref, sts_ref,
                   wgb_ref, wub_ref, carry_ref, *, tiles_per_seq, nchunk):
    i = pl.program_id(1)

    @pl.when(i == 0)
    def _():
        wgb_ref[...] = wg_ref[...].astype(BF16)
        wub_ref[...] = wu_ref[...].astype(BF16)
        xs = xs_ref[...]
        gs = jnp.dot(xs, wgb_ref[...], preferred_element_type=F32)
        us = jnp.dot(xs, wub_ref[...], preferred_element_type=F32)
        prev = prev_ref[...]
        x2, x1 = prev[:, 0, :], prev[:, 1, :]
        y = cw_ref[0:1, :] * x2 + cw_ref[1:2, :] * x1 + cw_ref[2:3, :] * gs
        as_ref[...] = (jax.nn.silu(y) * us).astype(as_ref.dtype)
        sts_ref[:, 0, :] = x1
        sts_ref[:, 1, :] = gs

    @pl.when(i % tiles_per_seq == 0)
    def _():
        carry_ref[...] = jnp.zeros_like(carry_ref)

    rc = xp_ref.shape[0] // nchunk
    for c in range(nchunk):
        x = xp_ref[c * rc:(c + 1) * rc, :]
        g = jnp.dot(x, wgb_ref[...], preferred_element_type=F32)
        u = jnp.dot(x, wub_ref[...], preferred_element_type=F32)
        y = _conv3_rows(g, cw_ref, carry_ref)
        ap_ref[c * rc:(c + 1) * rc, :] = (jax.nn.silu(y) * u).astype(ap_ref.dtype)

    @pl.when(i % tiles_per_seq == tiles_per_seq - 1)
    def _():
        stp_ref[...] = carry_ref[...]


def ffn_gate_up(xp, xs, wg, wu, cw, prev_s, layer, *, batch, tm=2048, tn=256, nchunk=4):
    mp, d = xp.shape
    ms = xs.shape[0]
    f = wg.shape[2]
    tm = min(tm, mp // batch)
    tiles_per_seq = mp // batch // tm
    return pl.pallas_call(
        functools.partial(_ffn_gu_kernel, tiles_per_seq=tiles_per_seq, nchunk=nchunk),
        grid=(f // tn, mp // tm),
        in_specs=[pl.BlockSpec((tm, d), lambda j, i: (i, 0)),
                  pl.BlockSpec((ms, d), lambda j, i: (0, 0)),
                  pl.BlockSpec((None, d, tn), lambda j, i: (layer, 0, j)),
                  pl.BlockSpec((None, d, tn), lambda j, i: (layer, 0, j)),
                  pl.BlockSpec((None, CONV_W, tn), lambda j, i: (layer, 0, j)),
                  pl.BlockSpec((None, ms, CONV_W - 1, tn), lambda j, i: (layer, 0, 0, j))],
        out_specs=[pl.BlockSpec((tm, tn), lambda j, i: (i, j)),
                   pl.BlockSpec((None, 8, tn), lambda j, i: (i // tiles_per_seq, 0, j)),
                   pl.BlockSpec((ms, tn), lambda j, i: (0, j)),
                   pl.BlockSpec((ms, CONV_W - 1, tn), lambda j, i: (0, 0, j))],
        out_shape=[jax.ShapeDtypeStruct((mp, f), BF16),
                   jax.ShapeDtypeStruct((batch, 8, f), F32),
                   jax.ShapeDtypeStruct((ms, f), BF16),
                   jax.ShapeDtypeStruct((ms, CONV_W - 1, f), F32)],
        scratch_shapes=[pltpu.VMEM((d, tn), BF16), pltpu.VMEM((d, tn), BF16), pltpu.VMEM((8, tn), F32)],
        compiler_params=_params("parallel", "arbitrary"),
        name="ffn_gate_up",
    )(xp, xs, wg, wu, cw, prev_s)


def _conv_step_kernel(*refs, mul_in, act):
    it = iter(refs)
    p_ref = next(it)
    q_ref = next(it) if mul_in else None
    r_ref, w_ref, prev_ref, o_ref, st_ref = next(it), next(it), next(it), next(it), next(it)
    m = p_ref[...] * q_ref[...] if mul_in else p_ref[...]
    prev = prev_ref[...]
    x2, x1 = prev[:, 0, :], prev[:, 1, :]
    y = w_ref[0:1, :] * x2 + w_ref[1:2, :] * x1 + w_ref[2:3, :] * m
    if act == "silu":
        y = jax.nn.silu(y)
    o_ref[...] = (r_ref[...] * y).astype(o_ref.dtype)
    st_ref[:, 0, :] = x1
    st_ref[:, 1, :] = m


def conv_step(p, q, r, w, prev, *, ncols, act, out_dtype, tc=512):
    mul_in = q is not None
    rows = p[0].shape[0]
    nc = ncols // tc

    def spec(c0):
        b0 = c0 // tc
        return pl.BlockSpec((rows, tc), lambda j: (0, b0 + j))

    in_specs, args = [spec(p[1])], [p[0]]
    if mul_in:
        in_specs.append(spec(q[1]))
        args.append(q[0])
    in_specs += [spec(r[1]), pl.BlockSpec((CONV_W, tc), lambda j: (0, j)),
                 pl.BlockSpec((rows, CONV_W - 1, tc), lambda j: (0, 0, j))]
    args += [r[0], w, prev]
    return pl.pallas_call(
        functools.partial(_conv_step_kernel, mul_in=mul_in, act=act),
        grid=(nc,),
        in_specs=in_specs,
        out_specs=[pl.BlockSpec((rows, tc), lambda j: (0, j)),
                   pl.BlockSpec((rows, CONV_W - 1, tc), lambda j: (0, 0, j))],
        out_shape=[jax.ShapeDtypeStruct((rows, ncols), out_dtype),
                   jax.ShapeDtypeStruct((rows, CONV_W - 1, ncols), F32)],
        compiler_params=_params("parallel"),
        name="conv_step_" + act,
    )(*args)


def _cast_slab_specs(w, layer, nslabs):
    rows, cols = w.shape[1:]
    slab = rows // nslabs
    assert slab * nslabs == rows and slab % 16 == 0
    return (pl.BlockSpec((None, slab, cols), lambda o, i: (layer, o, 0)),
            pl.BlockSpec((slab, cols), lambda o, i: (o, 0)),
            jax.ShapeDtypeStruct((rows, cols), BF16))


def _cast_slab(src_ref, dst_ref):
    @pl.when(pl.program_id(1) == 0)
    def _():
        dst_ref[...] = src_ref[...].astype(BF16)


def _s5_prep_kernel(lr_ref, li_ref, ldt_ref, br_ref, bi_ref, pow_ref, bb_ref):
    lr, li = lr_ref[...], li_ref[...]
    dt = jnp.exp(ldt_ref[...])
    mag = jnp.exp(lr * dt)
    ar, ai = mag * jnp.cos(li * dt), mag * jnp.sin(li * dt)
    den = lr * lr + li * li
    cr = ((ar - 1.0) * lr + ai * li) / den
    ci = (ai * lr - (ar - 1.0) * li) / den
    br, bi = br_ref[...], bi_ref[...]
    bb_ref[:, 0:SSM_GROUP, :] = cr * br - ci * bi
    bb_ref[:, SSM_GROUP:2 * SSM_GROUP, :] = cr * bi + ci * br
    pr, pi = ar, ai
    for k in range(8):
        pow_ref[:, k:k + 1, :] = pr
        pow_ref[:, 8 + k:9 + k, :] = pi
        pr, pi = pr * ar - pi * ai, pr * ai + pi * ar


def s5_prepare(lam_re, lam_im, log_dt, b_re, b_im, c_re, c_im, d_skip):
    g, p = lam_re.shape
    i = SSM_GROUP
    nb, gb = g // GROUPS_PER_BLOCK, GROUPS_PER_BLOCK
    pw, bb = pl.pallas_call(
        _s5_prep_kernel,
        out_shape=[jax.ShapeDtypeStruct((g, 16, p), F32), jax.ShapeDtypeStruct((g, 2 * i, p), F32)],
        name="s5_prep",
    )(lam_re.reshape(g, 1, p), lam_im.reshape(g, 1, p), log_dt.reshape(g, 1, 1),
      b_re.transpose(0, 2, 1), b_im.transpose(0, 2, 1))

    def lanes(a):
        return a.reshape(nb, gb, 8, p).transpose(0, 2, 1, 3).reshape(nb, 8, gb * p)

    pr, pi = lanes(pw[:, 0:8]), lanes(pw[:, 8:16])

    def bcast(a, k):
        return jnp.broadcast_to(a[:, k - 1:k, :], a.shape)

    tab = jnp.concatenate([pr, pi, bcast(pr, 1), bcast(pi, 1), bcast(pr, 8), bcast(pi, 8)], axis=1)
    eye = jnp.eye(gb, dtype=F32)

    def wb_part(a):
        a = a.reshape(nb, gb, i, p)
        return jnp.einsum('jaip,ab->jaibp', a, eye).reshape(nb, gb * i, gb * p)

    def wc_part(a):
        a = a.reshape(nb, gb, i, p)
        return jnp.einsum('jaip,ab->jbpai', a, eye).reshape(nb, gb * p, gb * i)

    wb = jnp.concatenate([wb_part(bb[:, :i]), wb_part(bb[:, i:])], axis=2).astype(BF16)
    wc = jnp.concatenate([wc_part(c_re.astype(F32)), wc_part(-c_im.astype(F32))], axis=1).astype(BF16)
    return tab, wb, wc, d_skip.reshape(nb, 1, gb * i)


_T_P, _T_A1, _T_A8 = 0, 16, 32


def _s5_seq_kernel(u_ref, wb_ref, wc_ref, tab_ref, d_ref, wsrc_ref, v_ref, bg_ref, cg_ref, cw_ref,
                   g_ref, st_ref, wdst_ref, ob_ref, sc_ref, h_ref, pw_ref, carry_ref, *, chunk):
    _cast_slab(wsrc_ref, wdst_ref)
    carry_ref[...] = jnp.zeros_like(carry_ref)
    for c in range(v_ref.shape[0] // chunk):
        rows = slice(c * chunk, (c + 1) * chunk)
        y = _conv3_rows(cg_ref[rows, :] * v_ref[rows, :], cw_ref, carry_ref)
        ob_ref[rows, :] = (bg_ref[rows, :] * y).astype(ob_ref.dtype)
    sc_ref[...] = carry_ref[...]
    seq, width = h_ref.shape
    half, lj = width // 2, seq // S5_SUBSEQ
    re, im = slice(0, half), slice(half, width)

    def tab(off):
        return tab_ref[off:off + 8, :], tab_ref[off + 8:off + 16, :]

    @pl.when(pl.program_id(1) == 0)
    def _():
        a8r, a8i = tab(_T_A8)

        def fill(m, c):
            pr, pi = c
            r0 = pl.multiple_of(m * 8, 8)
            pw_ref[pl.ds(r0, 8), re] = pr
            pw_ref[pl.ds(r0, 8), im] = pi
            return pr * a8r - pi * a8i, pr * a8i + pi * a8r

        lax.fori_loop(0, lj // 8, fill, tab(_T_P))

    for c in range(seq // chunk):
        rows = slice(c * chunk, (c + 1) * chunk)
        h_ref[rows, :] = jnp.dot(u_ref[rows, :].astype(BF16), wb_ref[...], preferred_element_type=F32)

    a1r, a1i = tab(_T_A1)

    def pass1(j, c):
        hr, hi = c
        r0 = pl.multiple_of(j * 8, 8)
        hr, hi = (a1r * hr - a1i * hi) + h_ref[pl.ds(r0, 8), re], (a1r * hi + a1i * hr) + h_ref[pl.ds(r0, 8), im]
        h_ref[pl.ds(r0, 8), re] = hr
        h_ref[pl.ds(r0, 8), im] = hi
        return hr, hi

    zero = jnp.zeros((S5_SUBSEQ, half), F32)
    xr, xi = lax.fori_loop(0, lj, pass1, (zero, zero), unroll=4)

    row = lax.broadcasted_iota(jnp.int32, (S5_SUBSEQ, half), 0)
    qr, qi = pw_ref[lj - 1:lj, re], pw_ref[lj - 1:lj, im]
    for k in (1, 2, 4):
        sr = jnp.where(row >= k, pltpu.roll(xr, k, axis=0), 0.0)
        si = jnp.where(row >= k, pltpu.roll(xi, k, axis=0), 0.0)
        xr, xi = xr + (qr * sr - qi * si), xi + (qr * si + qi * sr)
        qr, qi = qr * qr - qi * qi, 2.0 * (qr * qi)
    st_ref[:, re] = jnp.broadcast_to(xr[S5_SUBSEQ - 1:, :], xr.shape)
    st_ref[:, im] = jnp.broadcast_to(xi[S5_SUBSEQ - 1:, :], xi.shape)
    er = jnp.where(row >= 1, pltpu.roll(xr, 1, axis=0), 0.0)
    ei = jnp.where(row >= 1, pltpu.roll(xi, 1, axis=0), 0.0)

    def pass2(j, _):
        r0 = pl.multiple_of(j * 8, 8)
        pr = jnp.broadcast_to(pw_ref[pl.ds(j, 1), re], er.shape)
        pi = jnp.broadcast_to(pw_ref[pl.ds(j, 1), im], er.shape)
        h_ref[pl.ds(r0, 8), re] += pr * er - pi * ei
        h_ref[pl.ds(r0, 8), im] += pr * ei + pi * er
        return 0

    lax.fori_loop(0, lj, pass2, 0, unroll=4)
    for c in range(seq // chunk):
        rows = slice(c * chunk, (c + 1) * chunk)
        y = jnp.dot(h_ref[rows, :].astype(BF16), wc_ref[...], preferred_element_type=F32)
        g_ref[rows, :] = jax.nn.gelu(y + d_ref[...] * u_ref[rows, :])


def s5_seq(z, prep, sconv_w, sconv_layer, w_cast, layer, *, batch):
    tab, wb, wc, d = prep
    nb, cin, cst = wb.shape
    cast_in, cast_out, cast_shape = _cast_slab_specs(w_cast, layer, nb)
    m_rows = z.shape[0]
    seq = m_rows // batch
    lj = seq // S5_SUBSEQ

    def to_kernel_order(a):
        return a.reshape(batch, S5_SUBSEQ, lj, -1).transpose(0, 2, 1, 3).reshape(m_rows, -1)

    def to_time_order(a):
        return a.reshape(batch, lj, S5_SUBSEQ, -1).transpose(0, 2, 1, 3).reshape(m_rows, -1)

    def cols(k):
        return pl.BlockSpec((seq, cin), lambda j, b: (b, k * nb + j))

    g, st, w_bf16, ob, sc = pl.pallas_call(
        functools.partial(_s5_seq_kernel, chunk=min(seq, 512)),
        grid=(nb, batch),
        in_specs=[pl.BlockSpec((seq, cin), lambda j, b: (b, j)),
                  pl.BlockSpec((None, cin, cst), lambda j, b: (j, 0, 0)),
                  pl.BlockSpec((None, cst, cin), lambda j, b: (j, 0, 0)),
                  pl.BlockSpec((None, tab.shape[1], cst // 2), lambda j, b: (j, 0, 0)),
                  pl.BlockSpec((None, 1, cin), lambda j, b: (j, 0, 0)), cast_in,
                  cols(1), cols(2), cols(3),
                  pl.BlockSpec((None, CONV_W, cin), lambda j, b: (sconv_layer, 0, j))],
        out_specs=[pl.BlockSpec((seq, cin), lambda j, b: (b, j)),
                   pl.BlockSpec((None, None, 8, cst), lambda j, b: (j, b, 0, 0)), cast_out,
                   pl.BlockSpec((seq, cin), lambda j, b: (b, j)),
                   pl.BlockSpec((None, 8, cin), lambda j, b: (b, 0, j))],
        out_shape=[jax.ShapeDtypeStruct((m_rows, nb * cin), F32),
                   jax.ShapeDtypeStruct((nb, batch, 8, cst), F32), cast_shape,
                   jax.ShapeDtypeStruct((m_rows, nb * cin), BF16),
                   jax.ShapeDtypeStruct((batch, 8, nb * cin), F32)],
        scratch_shapes=[pltpu.VMEM((seq, cst), F32), pltpu.VMEM((lj, cst), F32), pltpu.VMEM((8, cin), F32)],
        compiler_params=_params("parallel", "arbitrary"),
        name="s5_seq",
    )(to_kernel_order(z[:, :nb * cin]), wb, wc, tab, d, w_cast, z, z, z, sconv_w)
    return to_time_order(g), st, w_bf16, ob, sc


def _s5_step_kernel(u_ref, s_ref, wb_ref, wc_ref, tab_ref, d_ref, g_ref, st_ref):
    half = s_ref.shape[1] // 2
    u = u_ref[...]
    bu = jnp.dot(u.astype(BF16), wb_ref[...], preferred_element_type=F32)
    ar, ai = tab_ref[_T_P:_T_P + 1, :], tab_ref[_T_P + 8:_T_P + 9, :]
    sr, si = s_ref[:, 0:half], s_ref[:, half:2 * half]
    hr = ar * sr - ai * si + bu[:, 0:half]
    hi = ar * si + ai * sr + bu[:, half:2 * half]
    st_ref[:, 0:half] = hr
    st_ref[:, half:2 * half] = hi
    h = jnp.concatenate([hr, hi], axis=1)
    y = jnp.dot(h.astype(BF16), wc_ref[...], preferred_element_type=F32) + d_ref[...] * u
    g_ref[...] = jax.nn.gelu(y)


def s5_step(z, state, prep):
    tab, wb, wc, d = prep
    nb, cin, cst = wb.shape
    rows = z.shape[0]
    return pl.pallas_call(
        _s5_step_kernel,
        grid=(nb,),
        in_specs=[pl.BlockSpec((rows, cin), lambda j: (0, j)),
                  pl.BlockSpec((None, rows, cst), lambda j: (j, 0, 0)),
                  pl.BlockSpec((None, cin, cst), lambda j: (j, 0, 0)),
                  pl.BlockSpec((None, cst, cin), lambda j: (j, 0, 0)),
                  pl.BlockSpec((None, tab.shape[1], cst // 2), lambda j: (j, 0, 0)),
                  pl.BlockSpec((None, 1, cin), lambda j: (j, 0, 0))],
        out_specs=[pl.BlockSpec((rows, cin), lambda j: (0, j)),
                   pl.BlockSpec((None, rows, cst), lambda j: (j, 0, 0))],
        out_shape=[jax.ShapeDtypeStruct((rows, nb * cin), F32),
                   jax.ShapeDtypeStruct((nb, rows, cst), F32)],
        compiler_params=_params("parallel"),
        name="s5_step",
    )(z, state, wb, wc, tab, d)


def _state_to_blocks(s_re, s_im):
    b, g, p = s_re.shape
    nb = g // GROUPS_PER_BLOCK

    def f(s):
        return s.reshape(b, nb, GROUPS_PER_BLOCK * p).transpose(1, 0, 2)

    return jnp.concatenate([f(s_re), f(s_im)], axis=2)


def _blocks_to_state(st, p=SSM_STATE):
    nb, b, c = st.shape
    half = c // 2

    def f(s):
        return s.transpose(1, 0, 2).reshape(b, nb * half // p, p)

    return f(st[:, :, :half]), f(st[:, :, half:])


def _attn_seq_kernel(q_ref, kp_ref, kc_ref, vp_ref, vc_ref, sk_ref, wsrc_ref, o_ref, wdst_ref):
    _cast_slab(wsrc_ref, wdst_ref)
    hd, blk = HEAD_DIM, WINDOW
    n = pl.program_id(0)
    shape = (KV_GROUP * blk, 2 * blk)
    qi = lax.broadcasted_iota(jnp.int32, shape, 0) % blk
    kj = lax.broadcasted_iota(jnp.int32, shape, 1)
    vis = (kj >= qi) & (kj <= qi + WINDOW) & ((kj >= blk) | (n > 0))
    for h in range(N_KV_HEADS):
        c0 = h * KV_GROUP * hd
        qa = jnp.concatenate([q_ref[:, c0 + g * hd:c0 + (g + 1) * hd] for g in range(KV_GROUP)], axis=0)
        kb = jnp.concatenate([kp_ref[:, h * hd:(h + 1) * hd], kc_ref[:, h * hd:(h + 1) * hd]], axis=0)
        vb = jnp.concatenate([vp_ref[:, h * hd:(h + 1) * hd], vc_ref[:, h * hd:(h + 1) * hd]], axis=0)
        s = lax.dot_general(qa.astype(BF16), kb.astype(BF16), (((1,), (1,)), ((), ())),
                            preferred_element_type=F32) * (hd ** -0.5)
        s = jnp.where(vis, s, NEG_INF)
        sk = sk_ref[h]
        m = jnp.maximum(jnp.max(s, axis=-1, keepdims=True), sk)
        p = jnp.exp(s - m)
        w = p * (1.0 / (jnp.sum(p, axis=-1, keepdims=True) + jnp.exp(sk - m)))
        o = jnp.dot(w.astype(BF16), vb.astype(BF16), preferred_element_type=F32)
        for g in range(KV_GROUP):
            o_ref[:, c0 + g * hd:c0 + (g + 1) * hd] = o[g * blk:(g + 1) * blk].astype(o_ref.dtype)


def attn_seq(qkv, sinks, w_cast, layer, *, batch, out_dtype):
    m_rows = qkv.shape[0]
    nblk = m_rows // batch // WINDOW
    nq = N_KV_HEADS * KV_GROUP
    qw, kvw = nq * HEAD_DIM, N_KV_HEADS * HEAD_DIM
    kblk, vblk = qw // kvw, qw // kvw + 1
    sk = jnp.repeat(sinks.astype(F32).reshape(N_KV_HEADS, KV_GROUP), WINDOW, axis=1)[..., None]

    cast_in, cast_out, cast_shape = _cast_slab_specs(w_cast, layer, nblk)

    def cur(c):
        return pl.BlockSpec((WINDOW, kvw), lambda n, b: (b * nblk + n, c))

    def prev(c):
        return pl.BlockSpec((WINDOW, kvw), lambda n, b: (b * nblk + jnp.maximum(n - 1, 0), c))

    return pl.pallas_call(
        _attn_seq_kernel,
        grid=(nblk, batch),
        in_specs=[pl.BlockSpec((WINDOW, qw), lambda n, b: (b * nblk + n, 0)),
                  prev(kblk), cur(kblk), prev(vblk), cur(vblk),
                  pl.BlockSpec((N_KV_HEADS, KV_GROUP * WINDOW, 1), lambda n, b: (0, 0, 0)), cast_in],
        out_specs=[pl.BlockSpec((WINDOW, qw), lambda n, b: (b * nblk + n, 0)), cast_out],
        out_shape=[jax.ShapeDtypeStruct((m_rows, qw), out_dtype), cast_shape],
        compiler_params=_params("parallel", "arbitrary"),
        name="attn_seq",
    )(qkv, qkv, qkv, qkv, qkv, sk, w_cast)


def _attn_step_kernel(q_ref, kn_ref, vn_ref, kc_ref, vc_ref, sk_ref, o_ref):
    scale = HEAD_DIM ** -0.5
    q = q_ref[...].astype(BF16)
    s = lax.dot_general(q, kc_ref[...].astype(BF16), (((1,), (1,)), ((), ())),
                        preferred_element_type=F32) * scale
    own = (lax.broadcasted_iota(jnp.int32, s.shape, 1) % N_KV_HEADS
           == lax.broadcasted_iota(jnp.int32, s.shape, 0) // KV_GROUP)
    s = jnp.where(own, s, NEG_INF)
    sn = jnp.sum(q.astype(F32) * kn_ref[...].astype(BF16).astype(F32), axis=-1, keepdims=True) * scale
    sk = sk_ref[...]
    m = jnp.maximum(jnp.maximum(jnp.max(s, axis=-1, keepdims=True), sn), sk)
    p, pn = jnp.exp(s - m), jnp.exp(sn - m)
    r = 1.0 / (jnp.sum(p, axis=-1, keepdims=True) + pn + jnp.exp(sk - m))
    o = jnp.dot((p * r).astype(BF16), vc_ref[...].astype(BF16), preferred_element_type=F32)
    o = o + (pn * r).astype(BF16).astype(F32) * vn_ref[...].astype(BF16).astype(F32)
    o_ref[...] = o.astype(o_ref.dtype)


def attn_step(qkv, cache_k, cache_v, layer, sinks, *, out_dtype):
    rows = qkv.shape[0]
    nq = N_KV_HEADS * KV_GROUP
    layers, wb = cache_k.shape[0], cache_k.shape[2]
    q3 = qkv[:, :nq * HEAD_DIM].reshape(rows, nq, HEAD_DIM)
    kn = qkv[:, nq * HEAD_DIM:(nq + N_KV_HEADS) * HEAD_DIM].reshape(rows, N_KV_HEADS, HEAD_DIM)
    vn = qkv[:, (nq + N_KV_HEADS) * HEAD_DIM:].reshape(rows, N_KV_HEADS, HEAD_DIM)
    nkeys = wb * N_KV_HEADS
    head_spec = pl.BlockSpec((None, nq, HEAD_DIM), lambda b: (b, 0, 0))
    cache_spec = pl.BlockSpec((None, None, nkeys, HEAD_DIM), lambda b: (layer, b, 0, 0))
    o = pl.pallas_call(
        _attn_step_kernel,
        grid=(rows,),
        in_specs=[head_spec, head_spec, head_spec, cache_spec, cache_spec,
                  pl.BlockSpec((nq, 1), lambda b: (0, 0))],
        out_specs=head_spec,
        out_shape=jax.ShapeDtypeStruct((rows, nq, HEAD_DIM), out_dtype),
        compiler_params=_params("parallel"),
        name="attn_step",
    )(q3, jnp.repeat(kn, KV_GROUP, axis=1), jnp.repeat(vn, KV_GROUP, axis=1),
      cache_k.reshape(layers, rows, nkeys, HEAD_DIM), cache_v.reshape(layers, rows, nkeys, HEAD_DIM),
      sinks.astype(F32).reshape(nq, 1))
    new_k = jnp.concatenate([cache_k[layer, :, 1:], kn[:, None]], axis=1)
    new_v = jnp.concatenate([cache_v[layer, :, 1:], vn[:, None]], axis=1)
    return o.reshape(rows, nq * HEAD_DIM), new_k, new_v


def kernel(x_prompt, x_sample, state_ssm_re, state_ssm_im, state_sconv, cache_k, cache_v, state_ffn_conv, norm_mix_g, norm_ffn_g, norm_final_g, w_in_even, ssm_lambda_re, ssm_lambda_im, ssm_log_dt, ssm_b_re, ssm_b_im, ssm_c_re, ssm_c_im, ssm_d, w_glu, b_glu, sconv_w, w_out_even, w_qkv, b_qkv, attn_sinks, w_o, b_o, w_ffn_gate, w_ffn_up, ffn_conv_w, w_ffn_down):
    bp, lp, d_model = x_prompt.shape
    bs = x_sample.shape[0]
    d_ssm, d_conv = ssm_d.shape[1], sconv_w.shape[2]
    nq, nkv = N_KV_HEADS * KV_GROUP * HEAD_DIM, N_KV_HEADS * HEAD_DIM
    keep = min(WINDOW, lp)
    xp, xs = x_prompt.reshape(bp * lp, d_model), x_sample.reshape(bs, d_model)
    p_out = [[] for _ in range(6)]
    s_out = [[] for _ in range(6)]
    for l in range(DEPTH):
        j = l // 2
        hp, hs = rmsnorm(xp, norm_mix_g[l], BF16), rmsnorm(xs, norm_mix_g[l], BF16)
        if l % 2 == 0:
            prep = s5_prepare(ssm_lambda_re[j], ssm_lambda_im[j], ssm_log_dt[j], ssm_b_re[j], ssm_b_im[j],
                              ssm_c_re[j], ssm_c_im[j], ssm_d[j])
            zp, zs = matmul_dual(hp, hs, w_in_even, j, **TILES_K_MODEL)
            cv, cb, cc = d_ssm, d_ssm + d_conv, d_ssm + 2 * d_conv
            gp, stp, w_down, bp_out, scp = s5_seq(zp, prep, sconv_w, j, w_ffn_down, l, batch=bp)
            gs, sts = s5_step(zs, _state_to_blocks(state_ssm_re[j], state_ssm_im[j]), prep)
            for out, st in ((p_out, stp[:, :, 0, :]), (s_out, sts)):
                re, im = _blocks_to_state(st)
                out[0].append(re)
                out[1].append(im)
            bs_out, scs = conv_step((zs, cc), (zs, cv), (zs, cb), sconv_w[j], state_sconv[j], ncols=d_conv,
                                    act="none", out_dtype=BF16)
            p_out[2].append(scp[:, 8 - (CONV_W - 1):, :])
            s_out[2].append(scs)
            ap, a_s = matmul_dual(gp, gs, w_glu, j, bias=b_glu, mode="glu", out_dtype=BF16, **TILES_GLU)
            xp, xs = matmul_dual((ap, bp_out), (a_s, bs_out), w_out_even, j, res=(xp, xs), mode="res",
                                 **TILES_K_MODEL)
        else:
            qkvp, qkvs = matmul_dual(hp, hs, w_qkv, j, bias=b_qkv, **TILES_K_MODEL)
            op, w_down = attn_seq(qkvp, attn_sinks[j], w_ffn_down, l, batch=bp, out_dtype=BF16)
            kv = qkvp.reshape(bp, lp, -1)[:, lp - keep:, nq:]
            p_out[3].append(kv[..., :nkv].reshape(bp, keep, N_KV_HEADS, HEAD_DIM))
            p_out[4].append(kv[..., nkv:].reshape(bp, keep, N_KV_HEADS, HEAD_DIM))
            o_s, kk, vv = attn_step(qkvs, cache_k, cache_v, j, attn_sinks[j], out_dtype=BF16)
            s_out[3].append(kk)
            s_out[4].append(vv)
            xp, xs = matmul_dual(op, o_s, w_o, j, bias=b_o, res=(xp, xs), mode="res", **TILES_K_MODEL)
        hp, hs = rmsnorm(xp, norm_ffn_g[l], BF16), rmsnorm(xs, norm_ffn_g[l], BF16)
        ap, fcp, a_s, fcs = ffn_gate_up(hp, hs, w_ffn_gate, w_ffn_up, ffn_conv_w, state_ffn_conv, l, batch=bp)
        p_out[5].append(fcp[:, 8 - (CONV_W - 1):, :])
        s_out[5].append(fcs)
        xp, xs = matmul_dual(ap, a_s, w_down[None], 0, res=(xp, xs), mode="res", **TILES_K_FF)
    yp, ys = rmsnorm(xp, norm_final_g, F32), rmsnorm(xs, norm_final_g, F32)
    return (yp.reshape(bp, lp, d_model), ys.reshape(bs, 1, d_model), *[jnp.stack(t) for t in p_out],
            *[jnp.stack(t) for t in s_out])
```

```python
import functools

import jax
import jax.numpy as jnp
from jax import lax
from jax.experimental import pallas as pl
from jax.experimental.pallas import tpu as pltpu

F32 = jnp.float32
BF16 = jnp.bfloat16

DEPTH = 4
SSM_GROUP = 16
SSM_STATE = 64
GROUPS_PER_BLOCK = 8
S5_SUBSEQ = 8
CONV_W = 3
HEAD_DIM = 128
N_KV_HEADS = 8
KV_GROUP = 4
WINDOW = 128
RMS_EPS = 1e-5
NEG_INF = -1e30

VMEM_LIMIT_BYTES = 62 * 1024 * 1024

TILES_K_MODEL = dict(tm=512, tn=1024)
TILES_K_FF = dict(tm=512, tn=512)
TILES_GLU = dict(tm=512, tn=2048)
TILES_FFN = dict(tm=2048, tn=256, nchunk=4)
ROW_CHUNK = 512
BF16_SUBLANES = 16


def _params(*sem):
    return pltpu.CompilerParams(dimension_semantics=sem, vmem_limit_bytes=VMEM_LIMIT_BYTES)


def _rmsnorm_kernel(x_ref, g_ref, o_ref):
    x = x_ref[...]
    y = x * lax.rsqrt(jnp.mean(x * x, axis=-1, keepdims=True) + RMS_EPS)
    o_ref[...] = (y * g_ref[...]).astype(o_ref.dtype)


def rmsnorm(x, g, out_dtype):
    m, d = x.shape
    tm = min(m, ROW_CHUNK)
    return pl.pallas_call(
        _rmsnorm_kernel,
        grid=(m // tm,),
        in_specs=[pl.BlockSpec((tm, d), lambda i: (i, 0)), pl.BlockSpec((1, d), lambda i: (0, 0))],
        out_specs=pl.BlockSpec((tm, d), lambda i: (i, 0)),
        out_shape=jax.ShapeDtypeStruct((m, d), out_dtype),
        compiler_params=_params("parallel"),
        name="rmsnorm",
    )(x, g.reshape(1, d))


def _mm_dual_kernel(*refs, nx, has_bias, mode, cast):
    it = iter(refs)
    xp_refs = [next(it) for _ in range(nx)]
    xs_refs = [next(it) for _ in range(nx)]
    w_ref = next(it)
    b_ref = next(it) if has_bias else None
    rp_ref, rs_ref = (next(it), next(it)) if mode == "res" else (None, None)
    op_ref, os_ref = next(it), next(it)
    wbf_ref = next(it) if cast else w_ref

    def run(x_refs, r_ref, o_ref):
        acc, k0 = None, 0
        for x_ref in x_refs:
            k1 = k0 + x_ref.shape[1]
            part = jnp.dot(x_ref[...].astype(BF16), wbf_ref[k0:k1, :], preferred_element_type=F32)
            acc = part if acc is None else acc + part
            k0 = k1
        if has_bias:
            acc = acc + b_ref[...]
        if mode == "res":
            acc = r_ref[...] + acc
        elif mode == "glu":
            acc = x_refs[0][...] * jax.nn.sigmoid(acc)
        o_ref[...] = acc.astype(o_ref.dtype)

    @pl.when(pl.program_id(1) == 0)
    def _():
        if cast:
            wbf_ref[...] = w_ref[...].astype(BF16)
        run(xs_refs, rs_ref, os_ref)

    run(xp_refs, rp_ref, op_ref)


def matmul_dual(xp, xs, w, layer, *, tm, tn, bias=None, res=None, mode="plain", out_dtype=F32):
    xps = xp if isinstance(xp, tuple) else (xp,)
    xss = xs if isinstance(xs, tuple) else (xs,)
    mp, ms = xps[0].shape[0], xss[0].shape[0]
    kdim, n = w.shape[1], w.shape[2]
    tm, tn = min(tm, mp), min(tn, n)
    cast = w.dtype != BF16
    assert mode != "glu" or (len(xps) == 1 and tn == n == kdim)
    in_specs = ([pl.BlockSpec((tm, x.shape[1]), lambda j, i: (i, 0)) for x in xps]
                + [pl.BlockSpec((ms, x.shape[1]), lambda j, i: (0, 0)) for x in xss]
                + [pl.BlockSpec((None, kdim, tn), lambda j, i: (layer, 0, j))])
    args = [*xps, *xss, w]
    if bias is not None:
        in_specs.append(pl.BlockSpec((None, 1, tn), lambda j, i: (layer, 0, j)))
        args.append(bias.reshape(bias.shape[0], 1, n))
    if mode == "res":
        in_specs += [pl.BlockSpec((tm, tn), lambda j, i: (i, j)), pl.BlockSpec((ms, tn), lambda j, i: (0, j))]
        args += list(res)
    return pl.pallas_call(
        functools.partial(_mm_dual_kernel, nx=len(xps), has_bias=bias is not None, mode=mode, cast=cast),
        grid=(n // tn, mp // tm),
        in_specs=in_specs,
        out_specs=[pl.BlockSpec((tm, tn), lambda j, i: (i, j)), pl.BlockSpec((ms, tn), lambda j, i: (0, j))],
        out_shape=[jax.ShapeDtypeStruct((mp, n), out_dtype), jax.ShapeDtypeStruct((ms, n), out_dtype)],
        scratch_shapes=[pltpu.VMEM((kdim, tn), BF16)] if cast else [],
        compiler_params=_params("parallel", "arbitrary"),
        name="matmul_dual_" + mode,
    )(*args)


def _conv3_rows(g, w_ref, carry_ref):
    c1 = carry_ref[7:8, :]
    c2 = carry_ref[6:7, :]
    row = lax.broadcasted_iota(jnp.int32, g.shape, 0)
    g1 = jnp.where(row == 0, c1, pltpu.roll(g, 1, axis=0))
    g2 = jnp.where(row == 0, c2, jnp.where(row == 1, c1, pltpu.roll(g, 2, axis=0)))
    carry_ref[...] = g[g.shape[0] - 8:, :]
    return w_ref[0:1, :] * g2 + w_ref[1:2, :] * g1 + w_ref[2:3, :] * g


def _ffn_gu_kernel(xp_ref, xs_ref, wg_ref, wu_ref, cw_ref, prev_ref, ap_ref, stp_ref, as_ref, sts_ref,
                   wgb_ref, wub_ref, carry_ref, *, tiles_per_seq, nchunk):
    i = pl.program_id(1)

    @pl.when(i == 0)
    def _():
        wgb_ref[...] = wg_ref[...].astype(BF16)
        wub_ref[...] = wu_ref[...].astype(BF16)
        xs = xs_ref[...]
        gs = jnp.dot(xs, wgb_ref[...], preferred_element_type=F32)
        us = jnp.dot(xs, wub_ref[...], preferred_element_type=F32)
        prev = prev_ref[...]
        x2, x1 = prev[:, 0, :], prev[:, 1, :]
        y = cw_ref[0:1, :] * x2 + cw_ref[1:2, :] * x1 + cw_ref[2:3, :] * gs
        as_ref[...] = (jax.nn.silu(y) * us).astype(as_ref.dtype)
        sts_ref[:, 0, :] = x1
        sts_ref[:, 1, :] = gs

    @pl.when(i % tiles_per_seq == 0)
    def _():
        carry_ref[...] = jnp.zeros_like(carry_ref)

    rc = xp_ref.shape[0] // nchunk
    for c in range(nchunk):
        x = xp_ref[c * rc:(c + 1) * rc, :]
        g = jnp.dot(x, wgb_ref[...], preferred_element_type=F32)
        u = jnp.dot(x, wub_ref[...], preferred_element_type=F32)
        y = _conv3_rows(g, cw_ref, carry_ref)
        ap_ref[c * rc:(c + 1) * rc, :] = (jax.nn.silu(y) * u).astype(ap_ref.dtype)

    @pl.when(i % tiles_per_seq == tiles_per_seq - 1)
    def _():
        stp_ref[...] = carry_ref[...]


def ffn_gate_up(xp, xs, wg, wu, cw, prev_s, layer, *, batch, tm, tn, nchunk):
    mp, d = xp.shape
    ms = xs.shape[0]
    f = wg.shape[2]
    tm = min(tm, mp // batch)
    tiles_per_seq = mp // batch // tm
    return pl.pallas_call(
        functools.partial(_ffn_gu_kernel, tiles_per_seq=tiles_per_seq, nchunk=nchunk),
        grid=(f // tn, mp // tm),
        in_specs=[pl.BlockSpec((tm, d), lambda j, i: (i, 0)),
                  pl.BlockSpec((ms, d), lambda j, i: (0, 0)),
                  pl.BlockSpec((None, d, tn), lambda j, i: (layer, 0, j)),
                  pl.BlockSpec((None, d, tn), lambda j, i: (layer, 0, j)),
                  pl.BlockSpec((None, CONV_W, tn), lambda j, i: (layer, 0, j)),
                  pl.BlockSpec((None, ms, CONV_W - 1, tn), lambda j, i: (layer, 0, 0, j))],
        out_specs=[pl.BlockSpec((tm, tn), lambda j, i: (i, j)),
                   pl.BlockSpec((None, 8, tn), lambda j, i: (i // tiles_per_seq, 0, j)),
                   pl.BlockSpec((ms, tn), lambda j, i: (0, j)),
                   pl.BlockSpec((ms, CONV_W - 1, tn), lambda j, i: (0, 0, j))],
        out_shape=[jax.ShapeDtypeStruct((mp, f), BF16),
                   jax.ShapeDtypeStruct((batch, 8, f), F32),
                   jax.ShapeDtypeStruct((ms, f), BF16),
                   jax.ShapeDtypeStruct((ms, CONV_W - 1, f), F32)],
        scratch_shapes=[pltpu.VMEM((d, tn), BF16), pltpu.VMEM((d, tn), BF16), pltpu.VMEM((8, tn), F32)],
        compiler_params=_params("parallel", "arbitrary"),
        name="ffn_gate_up",
    )(xp, xs, wg, wu, cw, prev_s)


def _sconv_step_kernel(v_ref, bg_ref, cg_ref, w_ref, prev_ref, o_ref, st_ref):
    m = cg_ref[...] * v_ref[...]
    prev = prev_ref[...]
    x2, x1 = prev[:, 0, :], prev[:, 1, :]
    y = w_ref[0:1, :] * x2 + w_ref[1:2, :] * x1 + w_ref[2:3, :] * m
    o_ref[...] = (bg_ref[...] * y).astype(o_ref.dtype)
    st_ref[:, 0, :] = x1
    st_ref[:, 1, :] = m


def sconv_step(z, sconv_w, layer, prev, *, ncols, out_dtype, tc=ROW_CHUNK):
    rows = z.shape[0]
    nc = ncols // tc

    def cols(k):
        return pl.BlockSpec((rows, tc), lambda j: (0, k * nc + j))

    return pl.pallas_call(
        _sconv_step_kernel,
        grid=(nc,),
        in_specs=[cols(1), cols(2), cols(3),
                  pl.BlockSpec((None, CONV_W, tc), lambda j: (layer, 0, j)),
                  pl.BlockSpec((rows, CONV_W - 1, tc), lambda j: (0, 0, j))],
        out_specs=[pl.BlockSpec((rows, tc), lambda j: (0, j)),
                   pl.BlockSpec((rows, CONV_W - 1, tc), lambda j: (0, 0, j))],
        out_shape=[jax.ShapeDtypeStruct((rows, ncols), out_dtype),
                   jax.ShapeDtypeStruct((rows, CONV_W - 1, ncols), F32)],
        compiler_params=_params("parallel"),
        name="sconv_step",
    )(z, z, z, sconv_w, prev)


def _cast_slab_specs(w, layer, nslabs):
    rows, cols = w.shape[1:]
    slab = rows // nslabs
    assert slab * nslabs == rows and slab % BF16_SUBLANES == 0
    return (pl.BlockSpec((None, slab, cols), lambda o, i: (layer, o, 0)),
            pl.BlockSpec((slab, cols), lambda o, i: (o, 0)),
            jax.ShapeDtypeStruct((rows, cols), BF16))


def _cast_slab(src_ref, dst_ref):
    @pl.when(pl.program_id(1) == 0)
    def _():
        dst_ref[...] = src_ref[...].astype(BF16)


def _s5_prep_kernel(lr_ref, li_ref, ldt_ref, br_ref, bi_ref, pow_ref, bb_ref):
    lr, li = lr_ref[...], li_ref[...]
    dt = jnp.exp(ldt_ref[...])
    mag = jnp.exp(lr * dt)
    ar, ai = mag * jnp.cos(li * dt), mag * jnp.sin(li * dt)
    den = lr * lr + li * li
    cr = ((ar - 1.0) * lr + ai * li) / den
    ci = (ai * lr - (ar - 1.0) * li) / den
    br, bi = br_ref[...], bi_ref[...]
    bb_ref[:, 0:SSM_GROUP, :] = cr * br - ci * bi
    bb_ref[:, SSM_GROUP:2 * SSM_GROUP, :] = cr * bi + ci * br
    pr, pi = ar, ai
    for k in range(8):
        pow_ref[:, k:k + 1, :] = pr
        pow_ref[:, 8 + k:9 + k, :] = pi
        pr, pi = pr * ar - pi * ai, pr * ai + pi * ar


def s5_prepare(lam_re, lam_im, log_dt, b_re, b_im, c_re, c_im, d_skip):
    g, p = lam_re.shape
    i = SSM_GROUP
    nb, gb = g // GROUPS_PER_BLOCK, GROUPS_PER_BLOCK
    pw, bb = pl.pallas_call(
        _s5_prep_kernel,
        out_shape=[jax.ShapeDtypeStruct((g, 16, p), F32), jax.ShapeDtypeStruct((g, 2 * i, p), F32)],
        name="s5_prep",
    )(lam_re.reshape(g, 1, p), lam_im.reshape(g, 1, p), log_dt.reshape(g, 1, 1),
      b_re.transpose(0, 2, 1), b_im.transpose(0, 2, 1))

    def lanes(a):
        return a.reshape(nb, gb, 8, p).transpose(0, 2, 1, 3).reshape(nb, 8, gb * p)

    pr, pi = lanes(pw[:, 0:8]), lanes(pw[:, 8:16])

    def bcast(a, k):
        return jnp.broadcast_to(a[:, k - 1:k, :], a.shape)

    tab = jnp.concatenate([pr, pi, bcast(pr, 1), bcast(pi, 1), bcast(pr, 8), bcast(pi, 8)], axis=1)
    eye = jnp.eye(gb, dtype=F32)

    def wb_part(a):
        a = a.reshape(nb, gb, i, p)
        return jnp.einsum('jaip,ab->jaibp', a, eye).reshape(nb, gb * i, gb * p)

    def wc_part(a):
        a = a.reshape(nb, gb, i, p)
        return jnp.einsum('jaip,ab->jbpai', a, eye).reshape(nb, gb * p, gb * i)

    wb = jnp.concatenate([wb_part(bb[:, :i]), wb_part(bb[:, i:])], axis=2).astype(BF16)
    wc = jnp.concatenate([wc_part(c_re.astype(F32)), wc_part(-c_im.astype(F32))], axis=1).astype(BF16)
    return tab, wb, wc, d_skip.reshape(nb, 1, gb * i)


_T_P, _T_A1, _T_A8 = 0, 16, 32


def _s5_seq_kernel(u_ref, wb_ref, wc_ref, tab_ref, d_ref, wsrc_ref, v_ref, bg_ref, cg_ref, cw_ref,
                   g_ref, st_ref, wdst_ref, ob_ref, sc_ref, h_ref, pw_ref, carry_ref, *, chunk):
    _cast_slab(wsrc_ref, wdst_ref)
    carry_ref[...] = jnp.zeros_like(carry_ref)
    for c in range(v_ref.shape[0] // chunk):
        rows = slice(c * chunk, (c + 1) * chunk)
        y = _conv3_rows(cg_ref[rows, :] * v_ref[rows, :], cw_ref, carry_ref)
        ob_ref[rows, :] = (bg_ref[rows, :] * y).astype(ob_ref.dtype)
    sc_ref[...] = carry_ref[...]
    seq, width = h_ref.shape
    half, lj = width // 2, seq // S5_SUBSEQ
    re, im = slice(0, half), slice(half, width)

    def tab(off):
        return tab_ref[off:off + 8, :], tab_ref[off + 8:off + 16, :]

    @pl.when(pl.program_id(1) == 0)
    def _():
        a8r, a8i = tab(_T_A8)

        def fill(m, c):
            pr, pi = c
            r0 = pl.multiple_of(m * 8, 8)
            pw_ref[pl.ds(r0, 8), re] = pr
            pw_ref[pl.ds(r0, 8), im] = pi
            return pr * a8r - pi * a8i, pr * a8i + pi * a8r

        lax.fori_loop(0, lj // 8, fill, tab(_T_P))

    for c in range(seq // chunk):
        rows = slice(c * chunk, (c + 1) * chunk)
        h_ref[rows, :] = jnp.dot(u_ref[rows, :].astype(BF16), wb_ref[...], preferred_element_type=F32)

    a1r, a1i = tab(_T_A1)

    def pass1(j, c):
        hr, hi = c
        r0 = pl.multiple_of(j * 8, 8)
        hr, hi = (a1r * hr - a1i * hi) + h_ref[pl.ds(r0, 8), re], (a1r * hi + a1i * hr) + h_ref[pl.ds(r0, 8), im]
        h_ref[pl.ds(r0, 8), re] = hr
        h_ref[pl.ds(r0, 8), im] = hi
        return hr, hi

    zero = jnp.zeros((S5_SUBSEQ, half), F32)
    xr, xi = lax.fori_loop(0, lj, pass1, (zero, zero), unroll=4)

    row = lax.broadcasted_iota(jnp.int32, (S5_SUBSEQ, half), 0)
    qr, qi = pw_ref[lj - 1:lj, re], pw_ref[lj - 1:lj, im]
    for k in (1, 2, 4):
        sr = jnp.where(row >= k, pltpu.roll(xr, k, axis=0), 0.0)
        si = jnp.where(row >= k, pltpu.roll(xi, k, axis=0), 0.0)
        xr, xi = xr + (qr * sr - qi * si), xi + (qr * si + qi * sr)
        qr, qi = qr * qr - qi * qi, 2.0 * (qr * qi)
    st_ref[:, re] = jnp.broadcast_to(xr[S5_SUBSEQ - 1:, :], xr.shape)
    st_ref[:, im] = jnp.broadcast_to(xi[S5_SUBSEQ - 1:, :], xi.shape)
    er = jnp.where(row >= 1, pltpu.roll(xr, 1, axis=0), 0.0)
    ei = jnp.where(row >= 1, pltpu.roll(xi, 1, axis=0), 0.0)

    def pass2(j, _):
        r0 = pl.multiple_of(j * 8, 8)
        pr = jnp.broadcast_to(pw_ref[pl.ds(j, 1), re], er.shape)
        pi = jnp.broadcast_to(pw_ref[pl.ds(j, 1), im], er.shape)
        h_ref[pl.ds(r0, 8), re] += pr * er - pi * ei
        h_ref[pl.ds(r0, 8), im] += pr * ei + pi * er
        return 0

    lax.fori_loop(0, lj, pass2, 0, unroll=4)
    for c in range(seq // chunk):
        rows = slice(c * chunk, (c + 1) * chunk)
        y = jnp.dot(h_ref[rows, :].astype(BF16), wc_ref[...], preferred_element_type=F32)
        g_ref[rows, :] = jax.nn.gelu(y + d_ref[...] * u_ref[rows, :])


def s5_seq(z, prep, sconv_w, sconv_layer, w_cast, layer, *, batch):
    tab, wb, wc, d = prep
    nb, cin, cst = wb.shape
    cast_in, cast_out, cast_shape = _cast_slab_specs(w_cast, layer, nb)
    m_rows = z.shape[0]
    seq = m_rows // batch
    lj = seq // S5_SUBSEQ

    def to_kernel_order(a):
        return a.reshape(batch, S5_SUBSEQ, lj, -1).transpose(0, 2, 1, 3).reshape(m_rows, -1)

    def to_time_order(a):
        return a.reshape(batch, lj, S5_SUBSEQ, -1).transpose(0, 2, 1, 3).reshape(m_rows, -1)

    def cols(k):
        return pl.BlockSpec((seq, cin), lambda j, b: (b, k * nb + j))

    g, st, w_bf16, ob, sc = pl.pallas_call(
        functools.partial(_s5_seq_kernel, chunk=min(seq, ROW_CHUNK)),
        grid=(nb, batch),
        in_specs=[pl.BlockSpec((seq, cin), lambda j, b: (b, j)),
                  pl.BlockSpec((None, cin, cst), lambda j, b: (j, 0, 0)),
                  pl.BlockSpec((None, cst, cin), lambda j, b: (j, 0, 0)),
                  pl.BlockSpec((None, tab.shape[1], cst // 2), lambda j, b: (j, 0, 0)),
                  pl.BlockSpec((None, 1, cin), lambda j, b: (j, 0, 0)), cast_in,
                  cols(1), cols(2), cols(3),
                  pl.BlockSpec((None, CONV_W, cin), lambda j, b: (sconv_layer, 0, j))],
        out_specs=[pl.BlockSpec((seq, cin), lambda j, b: (b, j)),
                   pl.BlockSpec((None, None, 8, cst), lambda j, b: (j, b, 0, 0)), cast_out,
                   pl.BlockSpec((seq, cin), lambda j, b: (b, j)),
                   pl.BlockSpec((None, 8, cin), lambda j, b: (b, 0, j))],
        out_shape=[jax.ShapeDtypeStruct((m_rows, nb * cin), F32),
                   jax.ShapeDtypeStruct((nb, batch, 8, cst), F32), cast_shape,
                   jax.ShapeDtypeStruct((m_rows, nb * cin), BF16),
                   jax.ShapeDtypeStruct((batch, 8, nb * cin), F32)],
        scratch_shapes=[pltpu.VMEM((seq, cst), F32), pltpu.VMEM((lj, cst), F32), pltpu.VMEM((8, cin), F32)],
        compiler_params=_params("parallel", "arbitrary"),
        name="s5_seq",
    )(to_kernel_order(z[:, :nb * cin]), wb, wc, tab, d, w_cast, z, z, z, sconv_w)
    return to_time_order(g), st, w_bf16, ob, sc


def _s5_step_kernel(u_ref, s_ref, wb_ref, wc_ref, tab_ref, d_ref, g_ref, st_ref):
    half = s_ref.shape[1] // 2
    u = u_ref[...]
    bu = jnp.dot(u.astype(BF16), wb_ref[...], preferred_element_type=F32)
    ar, ai = tab_ref[_T_P:_T_P + 1, :], tab_ref[_T_P + 8:_T_P + 9, :]
    sr, si = s_ref[:, 0:half], s_ref[:, half:2 * half]
    hr = ar * sr - ai * si + bu[:, 0:half]
    hi = ar * si + ai * sr + bu[:, half:2 * half]
    st_ref[:, 0:half] = hr
    st_ref[:, half:2 * half] = hi
    h = jnp.concatenate([hr, hi], axis=1)
    y = jnp.dot(h.astype(BF16), wc_ref[...], preferred_element_type=F32) + d_ref[...] * u
    g_ref[...] = jax.nn.gelu(y)


def s5_step(z, state, prep):
    tab, wb, wc, d = prep
    nb, cin, cst = wb.shape
    rows = z.shape[0]
    return pl.pallas_call(
        _s5_step_kernel,
        grid=(nb,),
        in_specs=[pl.BlockSpec((rows, cin), lambda j: (0, j)),
                  pl.BlockSpec((None, rows, cst), lambda j: (j, 0, 0)),
                  pl.BlockSpec((None, cin, cst), lambda j: (j, 0, 0)),
                  pl.BlockSpec((None, cst, cin), lambda j: (j, 0, 0)),
                  pl.BlockSpec((None, tab.shape[1], cst // 2), lambda j: (j, 0, 0)),
                  pl.BlockSpec((None, 1, cin), lambda j: (j, 0, 0))],
        out_specs=[pl.BlockSpec((rows, cin), lambda j: (0, j)),
                   pl.BlockSpec((None, rows, cst), lambda j: (j, 0, 0))],
        out_shape=[jax.ShapeDtypeStruct((rows, nb * cin), F32),
                   jax.ShapeDtypeStruct((nb, rows, cst), F32)],
        compiler_params=_params("parallel"),
        name="s5_step",
    )(z, state, wb, wc, tab, d)


def _state_to_blocks(s_re, s_im):
    b, g, p = s_re.shape
    nb = g // GROUPS_PER_BLOCK

    def f(s):
        return s.reshape(b, nb, GROUPS_PER_BLOCK * p).transpose(1, 0, 2)

    return jnp.concatenate([f(s_re), f(s_im)], axis=2)


def _blocks_to_state(st, p=SSM_STATE):
    nb, b, c = st.shape
    half = c // 2

    def f(s):
        return s.transpose(1, 0, 2).reshape(b, nb * half // p, p)

    return f(st[:, :, :half]), f(st[:, :, half:])


def _attn_seq_kernel(q_ref, kp_ref, kc_ref, vp_ref, vc_ref, sk_ref, wsrc_ref, o_ref, wdst_ref):
    _cast_slab(wsrc_ref, wdst_ref)
    hd, blk = HEAD_DIM, WINDOW
    n = pl.program_id(0)
    shape = (KV_GROUP * blk, 2 * blk)
    qi = lax.broadcasted_iota(jnp.int32, shape, 0) % blk
    kj = lax.broadcasted_iota(jnp.int32, shape, 1)
    vis = (kj >= qi) & (kj <= qi + WINDOW) & ((kj >= blk) | (n > 0))
    for h in range(N_KV_HEADS):
        c0 = h * KV_GROUP * hd
        qa = jnp.concatenate([q_ref[:, c0 + g * hd:c0 + (g + 1) * hd] for g in range(KV_GROUP)], axis=0)
        kb = jnp.concatenate([kp_ref[:, h * hd:(h + 1) * hd], kc_ref[:, h * hd:(h + 1) * hd]], axis=0)
        vb = jnp.concatenate([vp_ref[:, h * hd:(h + 1) * hd], vc_ref[:, h * hd:(h + 1) * hd]], axis=0)
        s = lax.dot_general(qa.astype(BF16), kb.astype(BF16), (((1,), (1,)), ((), ())),
                            preferred_element_type=F32) * (hd ** -0.5)
        s = jnp.where(vis, s, NEG_INF)
        sk = sk_ref[h]
        m = jnp.maximum(jnp.max(s, axis=-1, keepdims=True), sk)
        p = jnp.exp(s - m)
        w = p * (1.0 / (jnp.sum(p, axis=-1, keepdims=True) + jnp.exp(sk - m)))
        o = jnp.dot(w.astype(BF16), vb.astype(BF16), preferred_element_type=F32)
        for g in range(KV_GROUP):
            o_ref[:, c0 + g * hd:c0 + (g + 1) * hd] = o[g * blk:(g + 1) * blk].astype(o_ref.dtype)


def attn_seq(qkv, sinks, w_cast, layer, *, batch, out_dtype):
    m_rows = qkv.shape[0]
    nblk = m_rows // batch // WINDOW
    nq = N_KV_HEADS * KV_GROUP
    qw, kvw = nq * HEAD_DIM, N_KV_HEADS * HEAD_DIM
    kblk, vblk = qw // kvw, qw // kvw + 1
    sk = jnp.repeat(sinks.astype(F32).reshape(N_KV_HEADS, KV_GROUP), WINDOW, axis=1)[..., None]

    cast_in, cast_out, cast_shape = _cast_slab_specs(w_cast, layer, nblk)

    def cur(c):
        return pl.BlockSpec((WINDOW, kvw), lambda n, b: (b * nblk + n, c))

    def prev(c):
        return pl.BlockSpec((WINDOW, kvw), lambda n, b: (b * nblk + jnp.maximum(n - 1, 0), c))

    return pl.pallas_call(
        _attn_seq_kernel,
        grid=(nblk, batch),
        in_specs=[pl.BlockSpec((WINDOW, qw), lambda n, b: (b * nblk + n, 0)),
                  prev(kblk), cur(kblk), prev(vblk), cur(vblk),
                  pl.BlockSpec((N_KV_HEADS, KV_GROUP * WINDOW, 1), lambda n, b: (0, 0, 0)), cast_in],
        out_specs=[pl.BlockSpec((WINDOW, qw), lambda n, b: (b * nblk + n, 0)), cast_out],
        out_shape=[jax.ShapeDtypeStruct((m_rows, qw), out_dtype), cast_shape],
        compiler_params=_params("parallel", "arbitrary"),
        name="attn_seq",
    )(qkv, qkv, qkv, qkv, qkv, sk, w_cast)


def _attn_step_kernel(q_ref, kn_ref, vn_ref, kc_ref, vc_ref, sk_ref, o_ref):
    scale = HEAD_DIM ** -0.5
    q = q_ref[...].astype(BF16)
    s = lax.dot_general(q, kc_ref[...].astype(BF16), (((1,), (1,)), ((), ())),
                        preferred_element_type=F32) * scale
    own = (lax.broadcasted_iota(jnp.int32, s.shape, 1) % N_KV_HEADS
           == lax.broadcasted_iota(jnp.int32, s.shape, 0) // KV_GROUP)
    s = jnp.where(own, s, NEG_INF)
    sn = jnp.sum(q.astype(F32) * kn_ref[...].astype(BF16).astype(F32), axis=-1, keepdims=True) * scale
    sk = sk_ref[...]
    m = jnp.maximum(jnp.maximum(jnp.max(s, axis=-1, keepdims=True), sn), sk)
    p, pn = jnp.exp(s - m), jnp.exp(sn - m)
    r = 1.0 / (jnp.sum(p, axis=-1, keepdims=True) + pn + jnp.exp(sk - m))
    o = jnp.dot((p * r).astype(BF16), vc_ref[...].astype(BF16), preferred_element_type=F32)
    o = o + (pn * r).astype(BF16).astype(F32) * vn_ref[...].astype(BF16).astype(F32)
    o_ref[...] = o.astype(o_ref.dtype)


def attn_step(qkv, cache_k, cache_v, layer, sinks, *, out_dtype):
    rows = qkv.shape[0]
    nq = N_KV_HEADS * KV_GROUP
    layers, wb = cache_k.shape[0], cache_k.shape[2]
    q3 = qkv[:, :nq * HEAD_DIM].reshape(rows, nq, HEAD_DIM)
    kn = qkv[:, nq * HEAD_DIM:(nq + N_KV_HEADS) * HEAD_DIM].reshape(rows, N_KV_HEADS, HEAD_DIM)
    vn = qkv[:, (nq + N_KV_HEADS) * HEAD_DIM:].reshape(rows, N_KV_HEADS, HEAD_DIM)
    nkeys = wb * N_KV_HEADS
    head_spec = pl.BlockSpec((None, nq, HEAD_DIM), lambda b: (b, 0, 0))
    cache_spec = pl.BlockSpec((None, None, nkeys, HEAD_DIM), lambda b: (layer, b, 0, 0))
    o = pl.pallas_call(
        _attn_step_kernel,
        grid=(rows,),
        in_specs=[head_spec, head_spec, head_spec, cache_spec, cache_spec,
                  pl.BlockSpec((nq, 1), lambda b: (0, 0))],
        out_specs=head_spec,
        out_shape=jax.ShapeDtypeStruct((rows, nq, HEAD_DIM), out_dtype),
        compiler_params=_params("parallel"),
        name="attn_step",
    )(q3, jnp.repeat(kn, KV_GROUP, axis=1), jnp.repeat(vn, KV_GROUP, axis=1),
      cache_k.reshape(layers, rows, nkeys, HEAD_DIM), cache_v.reshape(layers, rows, nkeys, HEAD_DIM),
      sinks.astype(F32).reshape(nq, 1))
    new_k = jnp.concatenate([cache_k[layer, :, 1:], kn[:, None]], axis=1)
    new_v = jnp.concatenate([cache_v[layer, :, 1:], vn[:, None]], axis=1)
    return o.reshape(rows, nq * HEAD_DIM), new_k, new_v


def kernel(x_prompt, x_sample, state_ssm_re, state_ssm_im, state_sconv, cache_k, cache_v, state_ffn_conv, norm_mix_g, norm_ffn_g, norm_final_g, w_in_even, ssm_lambda_re, ssm_lambda_im, ssm_log_dt, ssm_b_re, ssm_b_im, ssm_c_re, ssm_c_im, ssm_d, w_glu, b_glu, sconv_w, w_out_even, w_qkv, b_qkv, attn_sinks, w_o, b_o, w_ffn_gate, w_ffn_up, ffn_conv_w, w_ffn_down):
    bp, lp, d_model = x_prompt.shape
    bs = x_sample.shape[0]
    nq, nkv = N_KV_HEADS * KV_GROUP * HEAD_DIM, N_KV_HEADS * HEAD_DIM
    keep = min(WINDOW, lp)
    xp, xs = x_prompt.reshape(bp * lp, d_model), x_sample.reshape(bs, d_model)
    p_out = [[] for _ in range(6)]
    s_out = [[] for _ in range(6)]
    for l in range(DEPTH):
        j = l // 2
        hp, hs = rmsnorm(xp, norm_mix_g[l], BF16), rmsnorm(xs, norm_mix_g[l], BF16)
        if l % 2 == 0:
            prep = s5_prepare(ssm_lambda_re[j], ssm_lambda_im[j], ssm_log_dt[j], ssm_b_re[j], ssm_b_im[j],
                              ssm_c_re[j], ssm_c_im[j], ssm_d[j])
            zp, zs = matmul_dual(hp, hs, w_in_even, j, **TILES_K_MODEL)
            gp, stp, w_down, bp_out, scp = s5_seq(zp, prep, sconv_w, j, w_ffn_down, l, batch=bp)
            gs, sts = s5_step(zs, _state_to_blocks(state_ssm_re[j], state_ssm_im[j]), prep)
            for out, st in ((p_out, stp[:, :, 0, :]), (s_out, sts)):
                re, im = _blocks_to_state(st)
                out[0].append(re)
                out[1].append(im)
            bs_out, scs = sconv_step(zs, sconv_w, j, state_sconv[j], ncols=sconv_w.shape[2], out_dtype=BF16)
            p_out[2].append(scp[:, 8 - (CONV_W - 1):, :])
            s_out[2].append(scs)
            ap, a_s = matmul_dual(gp, gs, w_glu, j, bias=b_glu, mode="glu", out_dtype=BF16, **TILES_GLU)
            xp, xs = matmul_dual((ap, bp_out), (a_s, bs_out), w_out_even, j, res=(xp, xs), mode="res",
                                 **TILES_K_MODEL)
        else:
            qkvp, qkvs = matmul_dual(hp, hs, w_qkv, j, bias=b_qkv, **TILES_K_MODEL)
            op, w_down = attn_seq(qkvp, attn_sinks[j], w_ffn_down, l, batch=bp, out_dtype=BF16)
            kv = qkvp.reshape(bp, lp, -1)[:, lp - keep:, nq:]
            p_out[3].append(kv[..., :nkv].reshape(bp, keep, N_KV_HEADS, HEAD_DIM))
            p_out[4].append(kv[..., nkv:].reshape(bp, keep, N_KV_HEADS, HEAD_DIM))
            o_s, kk, vv = attn_step(qkvs, cache_k, cache_v, j, attn_sinks[j], out_dtype=BF16)
            s_out[3].append(kk)
            s_out[4].append(vv)
            xp, xs = matmul_dual(op, o_s, w_o, j, bias=b_o, res=(xp, xs), mode="res", **TILES_K_MODEL)
        hp, hs = rmsnorm(xp, norm_ffn_g[l], BF16), rmsnorm(xs, norm_ffn_g[l], BF16)
        ap, fcp, a_s, fcs = ffn_gate_up(hp, hs, w_ffn_gate, w_ffn_up, ffn_conv_w, state_ffn_conv, l, batch=bp,
                                        **TILES_FFN)
        p_out[5].append(fcp[:, 8 - (CONV_W - 1):, :])
        s_out[5].append(fcs)
        xp, xs = matmul_dual(ap, a_s, w_down[None], 0, res=(xp, xs), mode="res", **TILES_K_FF)
    yp, ys = rmsnorm(xp, norm_final_g, F32), rmsnorm(xs, norm_final_g, F32)
    return (yp.reshape(bp, lp, d_model), ys.reshape(bs, 1, d_model), *[jnp.stack(t) for t in p_out],
            *[jnp.stack(t) for t in s_out])
```

```python
import functools

import jax
import jax.numpy as jnp
from jax import lax
from jax.experimental import pallas as pl
from jax.experimental.pallas import tpu as pltpu

F32 = jnp.float32
BF16 = jnp.bfloat16

DEPTH = 4
SSM_GROUP = 16
SSM_STATE = 64
GROUPS_PER_BLOCK = 8
S5_SUBSEQ = 8
CONV_W = 3
HEAD_DIM = 128
N_KV_HEADS = 8
KV_GROUP = 4
WINDOW = 128
RMS_EPS = 1e-5
NEG_INF = -1e30

VMEM_LIMIT_BYTES = 62 * 1024 * 1024

TILES_K_MODEL = dict(tm=512, tn=1024)
TILES_K_FF = dict(tm=512, tn=512)
TILES_GLU = dict(tm=512, tn=2048)
TILES_FFN = dict(tm=2048, tn=256, nchunk=4)
ROW_CHUNK = 512
BF16_SUBLANES = 16


def _params(*sem):
    return pltpu.CompilerParams(dimension_semantics=sem, vmem_limit_bytes=VMEM_LIMIT_BYTES)


NORM_IN_SLOTS = 3
NORM_OUT_SLOTS = 2


def _rmsnorm_kernel(x_hbm, g_ref, o_hbm, xbuf, obuf, xsem, osem, *, tm, nt):
    def x_copy(t, slot):
        return pltpu.make_async_copy(x_hbm.at[pl.ds(t * tm, tm), :], xbuf.at[slot], xsem.at[slot])

    def o_copy(t, slot):
        return pltpu.make_async_copy(obuf.at[slot], o_hbm.at[pl.ds(t * tm, tm), :], osem.at[slot])

    ahead = NORM_IN_SLOTS - 1
    for t in range(min(ahead, nt)):
        x_copy(t, t).start()

    def tile(t, _):
        @pl.when(t + ahead < nt)
        def _():
            x_copy(t + ahead, (t + ahead) % NORM_IN_SLOTS).start()

        slot = t % NORM_IN_SLOTS
        x_copy(t, slot).wait()
        oslot = t % NORM_OUT_SLOTS

        @pl.when(t >= NORM_OUT_SLOTS)
        def _():
            o_copy(t - NORM_OUT_SLOTS, oslot).wait()

        x = xbuf[slot]
        y = x * lax.rsqrt(jnp.mean(x * x, axis=-1, keepdims=True) + RMS_EPS)
        obuf[oslot] = (y * g_ref[...]).astype(obuf.dtype)
        o_copy(t, oslot).start()
        return 0

    lax.fori_loop(0, nt, tile, 0)
    for t in range(max(nt - NORM_OUT_SLOTS, 0), nt):
        o_copy(t, t % NORM_OUT_SLOTS).wait()


def rmsnorm(x, g, out_dtype):
    m, d = x.shape
    tm = min(m, ROW_CHUNK)
    nt = m // tm
    return pl.pallas_call(
        functools.partial(_rmsnorm_kernel, tm=tm, nt=nt),
        in_specs=[pl.BlockSpec(memory_space=pl.ANY), pl.BlockSpec((1, d), lambda: (0, 0))],
        out_specs=pl.BlockSpec(memory_space=pl.ANY),
        out_shape=jax.ShapeDtypeStruct((m, d), out_dtype),
        scratch_shapes=[pltpu.VMEM((NORM_IN_SLOTS, tm, d), x.dtype),
                        pltpu.VMEM((NORM_OUT_SLOTS, tm, d), out_dtype),
                        pltpu.SemaphoreType.DMA((NORM_IN_SLOTS,)),
                        pltpu.SemaphoreType.DMA((NORM_OUT_SLOTS,))],
        compiler_params=pltpu.CompilerParams(vmem_limit_bytes=VMEM_LIMIT_BYTES),
        name="rmsnorm",
    )(x, g.reshape(1, d))


def _mm_dual_kernel(*refs, nx, has_bias, mode, cast):
    it = iter(refs)
    xp_refs = [next(it) for _ in range(nx)]
    xs_refs = [next(it) for _ in range(nx)]
    w_ref = next(it)
    b_ref = next(it) if has_bias else None
    rp_ref, rs_ref = (next(it), next(it)) if mode == "res" else (None, None)
    op_ref, os_ref = next(it), next(it)
    wbf_ref = next(it) if cast else w_ref

    def run(x_refs, r_ref, o_ref):
        acc, k0 = None, 0
        for x_ref in x_refs:
            k1 = k0 + x_ref.shape[1]
            part = jnp.dot(x_ref[...].astype(BF16), wbf_ref[k0:k1, :], preferred_element_type=F32)
            acc = part if acc is None else acc + part
            k0 = k1
        if has_bias:
            acc = acc + b_ref[...]
        if mode == "res":
            acc = r_ref[...] + acc
        elif mode == "glu":
            acc = x_refs[0][...] * jax.nn.sigmoid(acc)
        o_ref[...] = acc.astype(o_ref.dtype)

    @pl.when(pl.program_id(1) == 0)
    def _():
        if cast:
            wbf_ref[...] = w_ref[...].astype(BF16)
        run(xs_refs, rs_ref, os_ref)

    run(xp_refs, rp_ref, op_ref)


def matmul_dual(xp, xs, w, layer, *, tm, tn, bias=None, res=None, mode="plain", out_dtype=F32):
    xps = xp if isinstance(xp, tuple) else (xp,)
    xss = xs if isinstance(xs, tuple) else (xs,)
    mp, ms = xps[0].shape[0], xss[0].shape[0]
    kdim, n = w.shape[1], w.shape[2]
    tm, tn = min(tm, mp), min(tn, n)
    cast = w.dtype != BF16
    assert mode != "glu" or (len(xps) == 1 and tn == n == kdim)
    in_specs = ([pl.BlockSpec((tm, x.shape[1]), lambda j, i: (i, 0)) for x in xps]
                + [pl.BlockSpec((ms, x.shape[1]), lambda j, i: (0, 0)) for x in xss]
                + [pl.BlockSpec((None, kdim, tn), lambda j, i: (layer, 0, j))])
    args = [*xps, *xss, w]
    if bias is not None:
        in_specs.append(pl.BlockSpec((None, 1, tn), lambda j, i: (layer, 0, j)))
        args.append(bias.reshape(bias.shape[0], 1, n))
    if mode == "res":
        in_specs += [pl.BlockSpec((tm, tn), lambda j, i: (i, j)), pl.BlockSpec((ms, tn), lambda j, i: (0, j))]
        args += list(res)
    return pl.pallas_call(
        functools.partial(_mm_dual_kernel, nx=len(xps), has_bias=bias is not None, mode=mode, cast=cast),
        grid=(n // tn, mp // tm),
        in_specs=in_specs,
        out_specs=[pl.BlockSpec((tm, tn), lambda j, i: (i, j)), pl.BlockSpec((ms, tn), lambda j, i: (0, j))],
        out_shape=[jax.ShapeDtypeStruct((mp, n), out_dtype), jax.ShapeDtypeStruct((ms, n), out_dtype)],
        scratch_shapes=[pltpu.VMEM((kdim, tn), BF16)] if cast else [],
        compiler_params=_params("parallel", "arbitrary"),
        name="matmul_dual_" + mode,
    )(*args)


def _conv3_rows(g, w_ref, carry_ref):
    c1 = carry_ref[7:8, :]
    c2 = carry_ref[6:7, :]
    row = lax.broadcasted_iota(jnp.int32, g.shape, 0)
    g1 = jnp.where(row == 0, c1, pltpu.roll(g, 1, axis=0))
    g2 = jnp.where(row == 0, c2, jnp.where(row == 1, c1, pltpu.roll(g, 2, axis=0)))
    carry_ref[...] = g[g.shape[0] - 8:, :]
    return w_ref[0:1, :] * g2 + w_ref[1:2, :] * g1 + w_ref[2:3, :] * g


def _ffn_gu_kernel(xp_ref, xs_ref, wg_ref, wu_ref, cw_ref, prev_ref, ap_ref, stp_ref, as_ref, sts_ref,
                   wgb_ref, wub_ref, carry_ref, *, tiles_per_seq, nchunk):
    i = pl.program_id(1)

    @pl.when(i == 0)
    def _():
        wgb_ref[...] = wg_ref[...].astype(BF16)
        wub_ref[...] = wu_ref[...].astype(BF16)
        xs = xs_ref[...]
        gs = jnp.dot(xs, wgb_ref[...], preferred_element_type=F32)
        us = jnp.dot(xs, wub_ref[...], preferred_element_type=F32)
        prev = prev_ref[...]
        x2, x1 = prev[:, 0, :], prev[:, 1, :]
        y = cw_ref[0:1, :] * x2 + cw_ref[1:2, :] * x1 + cw_ref[2:3, :] * gs
        as_ref[...] = (jax.nn.silu(y) * us).astype(as_ref.dtype)
        sts_ref[:, 0, :] = x1
        sts_ref[:, 1, :] = gs

    @pl.when(i % tiles_per_seq == 0)
    def _():
        carry_ref[...] = jnp.zeros_like(carry_ref)

    rc = xp_ref.shape[0] // nchunk
    for c in range(nchunk):
        x = xp_ref[c * rc:(c + 1) * rc, :]
        g = jnp.dot(x, wgb_ref[...], preferred_element_type=F32)
        u = jnp.dot(x, wub_ref[...], preferred_element_type=F32)
        y = _conv3_rows(g, cw_ref, carry_ref)
        ap_ref[c * rc:(c + 1) * rc, :] = (jax.nn.silu(y) * u).astype(ap_ref.dtype)

    @pl.when(i % tiles_per_seq == tiles_per_seq - 1)
    def _():
        stp_ref[...] = carry_ref[...]


def ffn_gate_up(xp, xs, wg, wu, cw, prev_s, layer, *, batch, tm, tn, nchunk):
    mp, d = xp.shape
    ms = xs.shape[0]
    f = wg.shape[2]
    tm = min(tm, mp // batch)
    tiles_per_seq = mp // batch // tm
    return pl.pallas_call(
        functools.partial(_ffn_gu_kernel, tiles_per_seq=tiles_per_seq, nchunk=nchunk),
        grid=(f // tn, mp // tm),
        in_specs=[pl.BlockSpec((tm, d), lambda j, i: (i, 0)),
                  pl.BlockSpec((ms, d), lambda j, i: (0, 0)),
                  pl.BlockSpec((None, d, tn), lambda j, i: (layer, 0, j)),
                  pl.BlockSpec((None, d, tn), lambda j, i: (layer, 0, j)),
                  pl.BlockSpec((None, CONV_W, tn), lambda j, i: (layer, 0, j)),
                  pl.BlockSpec((None, ms, CONV_W - 1, tn), lambda j, i: (layer, 0, 0, j))],
        out_specs=[pl.BlockSpec((tm, tn), lambda j, i: (i, j)),
                   pl.BlockSpec((None, 8, tn), lambda j, i: (i // tiles_per_seq, 0, j)),
                   pl.BlockSpec((ms, tn), lambda j, i: (0, j)),
                   pl.BlockSpec((ms, CONV_W - 1, tn), lambda j, i: (0, 0, j))],
        out_shape=[jax.ShapeDtypeStruct((mp, f), BF16),
                   jax.ShapeDtypeStruct((batch, 8, f), F32),
                   jax.ShapeDtypeStruct((ms, f), BF16),
                   jax.ShapeDtypeStruct((ms, CONV_W - 1, f), F32)],
        scratch_shapes=[pltpu.VMEM((d, tn), BF16), pltpu.VMEM((d, tn), BF16), pltpu.VMEM((8, tn), F32)],
        compiler_params=_params("parallel", "arbitrary"),
        name="ffn_gate_up",
    )(xp, xs, wg, wu, cw, prev_s)


def _sconv_step_kernel(v_ref, bg_ref, cg_ref, w_ref, prev_ref, o_ref, st_ref):
    m = cg_ref[...] * v_ref[...]
    prev = prev_ref[...]
    x2, x1 = prev[:, 0, :], prev[:, 1, :]
    y = w_ref[0:1, :] * x2 + w_ref[1:2, :] * x1 + w_ref[2:3, :] * m
    o_ref[...] = (bg_ref[...] * y).astype(o_ref.dtype)
    st_ref[:, 0, :] = x1
    st_ref[:, 1, :] = m


def sconv_step(z, sconv_w, layer, prev, *, ncols, out_dtype, tc=ROW_CHUNK):
    rows = z.shape[0]
    nc = ncols // tc

    def cols(k):
        return pl.BlockSpec((rows, tc), lambda j: (0, k * nc + j))

    return pl.pallas_call(
        _sconv_step_kernel,
        grid=(nc,),
        in_specs=[cols(1), cols(2), cols(3),
                  pl.BlockSpec((None, CONV_W, tc), lambda j: (layer, 0, j)),
                  pl.BlockSpec((rows, CONV_W - 1, tc), lambda j: (0, 0, j))],
        out_specs=[pl.BlockSpec((rows, tc), lambda j: (0, j)),
                   pl.BlockSpec((rows, CONV_W - 1, tc), lambda j: (0, 0, j))],
        out_shape=[jax.ShapeDtypeStruct((rows, ncols), out_dtype),
                   jax.ShapeDtypeStruct((rows, CONV_W - 1, ncols), F32)],
        compiler_params=_params("parallel"),
        name="sconv_step",
    )(z, z, z, sconv_w, prev)


def _cast_slab_specs(w, layer, nslabs):
    rows, cols = w.shape[1:]
    slab = rows // nslabs
    assert slab * nslabs == rows and slab % BF16_SUBLANES == 0
    return (pl.BlockSpec((None, slab, cols), lambda o, i: (layer, o, 0)),
            pl.BlockSpec((slab, cols), lambda o, i: (o, 0)),
            jax.ShapeDtypeStruct((rows, cols), BF16))


def _cast_slab(src_ref, dst_ref):
    @pl.when(pl.program_id(1) == 0)
    def _():
        dst_ref[...] = src_ref[...].astype(BF16)


def _s5_prep_kernel(lr_ref, li_ref, ldt_ref, br_ref, bi_ref, pow_ref, bb_ref):
    lr, li = lr_ref[...], li_ref[...]
    dt = jnp.exp(ldt_ref[...])
    mag = jnp.exp(lr * dt)
    ar, ai = mag * jnp.cos(li * dt), mag * jnp.sin(li * dt)
    den = lr * lr + li * li
    cr = ((ar - 1.0) * lr + ai * li) / den
    ci = (ai * lr - (ar - 1.0) * li) / den
    br, bi = br_ref[...], bi_ref[...]
    bb_ref[:, 0:SSM_GROUP, :] = cr * br - ci * bi
    bb_ref[:, SSM_GROUP:2 * SSM_GROUP, :] = cr * bi + ci * br
    pr, pi = ar, ai
    for k in range(8):
        pow_ref[:, k:k + 1, :] = pr
        pow_ref[:, 8 + k:9 + k, :] = pi
        pr, pi = pr * ar - pi * ai, pr * ai + pi * ar


def s5_prepare(lam_re, lam_im, log_dt, b_re, b_im, c_re, c_im, d_skip):
    g, p = lam_re.shape
    i = SSM_GROUP
    nb, gb = g // GROUPS_PER_BLOCK, GROUPS_PER_BLOCK
    pw, bb = pl.pallas_call(
        _s5_prep_kernel,
        out_shape=[jax.ShapeDtypeStruct((g, 16, p), F32), jax.ShapeDtypeStruct((g, 2 * i, p), F32)],
        name="s5_prep",
    )(lam_re.reshape(g, 1, p), lam_im.reshape(g, 1, p), log_dt.reshape(g, 1, 1),
      b_re.transpose(0, 2, 1), b_im.transpose(0, 2, 1))

    def lanes(a):
        return a.reshape(nb, gb, 8, p).transpose(0, 2, 1, 3).reshape(nb, 8, gb * p)

    pr, pi = lanes(pw[:, 0:8]), lanes(pw[:, 8:16])

    def bcast(a, k):
        return jnp.broadcast_to(a[:, k - 1:k, :], a.shape)

    tab = jnp.concatenate([pr, pi, bcast(pr, 1), bcast(pi, 1), bcast(pr, 8), bcast(pi, 8)], axis=1)
    eye = jnp.eye(gb, dtype=F32)

    def wb_part(a):
        a = a.reshape(nb, gb, i, p)
        return jnp.einsum('jaip,ab->jaibp', a, eye).reshape(nb, gb * i, gb * p)

    def wc_part(a):
        a = a.reshape(nb, gb, i, p)
        return jnp.einsum('jaip,ab->jbpai', a, eye).reshape(nb, gb * p, gb * i)

    wb = jnp.concatenate([wb_part(bb[:, :i]), wb_part(bb[:, i:])], axis=2).astype(BF16)
    wc = jnp.concatenate([wc_part(c_re.astype(F32)), wc_part(-c_im.astype(F32))], axis=1).astype(BF16)
    return tab, wb, wc, d_skip.reshape(nb, 1, gb * i)


_T_P, _T_A1, _T_A8 = 0, 16, 32


def _s5_seq_kernel(u_ref, wb_ref, wc_ref, tab_ref, d_ref, wsrc_ref, v_ref, bg_ref, cg_ref, cw_ref,
                   g_ref, st_ref, wdst_ref, ob_ref, sc_ref, h_ref, pw_ref, carry_ref, *, chunk):
    _cast_slab(wsrc_ref, wdst_ref)
    carry_ref[...] = jnp.zeros_like(carry_ref)
    for c in range(v_ref.shape[0] // chunk):
        rows = slice(c * chunk, (c + 1) * chunk)
        y = _conv3_rows(cg_ref[rows, :] * v_ref[rows, :], cw_ref, carry_ref)
        ob_ref[rows, :] = (bg_ref[rows, :] * y).astype(ob_ref.dtype)
    sc_ref[...] = carry_ref[...]
    seq, width = h_ref.shape
    half, lj = width // 2, seq // S5_SUBSEQ
    re, im = slice(0, half), slice(half, width)

    def tab(off):
        return tab_ref[off:off + 8, :], tab_ref[off + 8:off + 16, :]

    @pl.when(pl.program_id(1) == 0)
    def _():
        a8r, a8i = tab(_T_A8)

        def fill(m, c):
            pr, pi = c
            r0 = pl.multiple_of(m * 8, 8)
            pw_ref[pl.ds(r0, 8), re] = pr
            pw_ref[pl.ds(r0, 8), im] = pi
            return pr * a8r - pi * a8i, pr * a8i + pi * a8r

        lax.fori_loop(0, lj // 8, fill, tab(_T_P))

    for c in range(seq // chunk):
        rows = slice(c * chunk, (c + 1) * chunk)
        h_ref[rows, :] = jnp.dot(u_ref[rows, :].astype(BF16), wb_ref[...], preferred_element_type=F32)

    a1r, a1i = tab(_T_A1)

    def pass1(j, c):
        hr, hi = c
        r0 = pl.multiple_of(j * 8, 8)
        hr, hi = (a1r * hr - a1i * hi) + h_ref[pl.ds(r0, 8), re], (a1r * hi + a1i * hr) + h_ref[pl.ds(r0, 8), im]
        h_ref[pl.ds(r0, 8), re] = hr
        h_ref[pl.ds(r0, 8), im] = hi
        return hr, hi

    zero = jnp.zeros((S5_SUBSEQ, half), F32)
    xr, xi = lax.fori_loop(0, lj, pass1, (zero, zero), unroll=4)

    row = lax.broadcasted_iota(jnp.int32, (S5_SUBSEQ, half), 0)
    qr, qi = pw_ref[lj - 1:lj, re], pw_ref[lj - 1:lj, im]
    for k in (1, 2, 4):
        sr = jnp.where(row >= k, pltpu.roll(xr, k, axis=0), 0.0)
        si = jnp.where(row >= k, pltpu.roll(xi, k, axis=0), 0.0)
        xr, xi = xr + (qr * sr - qi * si), xi + (qr * si + qi * sr)
        qr, qi = qr * qr - qi * qi, 2.0 * (qr * qi)
    st_ref[:, re] = jnp.broadcast_to(xr[S5_SUBSEQ - 1:, :], xr.shape)
    st_ref[:, im] = jnp.broadcast_to(xi[S5_SUBSEQ - 1:, :], xi.shape)
    er = jnp.where(row >= 1, pltpu.roll(xr, 1, axis=0), 0.0)
    ei = jnp.where(row >= 1, pltpu.roll(xi, 1, axis=0), 0.0)

    def pass2(j, _):
        r0 = pl.multiple_of(j * 8, 8)
        pr = jnp.broadcast_to(pw_ref[pl.ds(j, 1), re], er.shape)
        pi = jnp.broadcast_to(pw_ref[pl.ds(j, 1), im], er.shape)
        h_ref[pl.ds(r0, 8), re] += pr * er - pi * ei
        h_ref[pl.ds(r0, 8), im] += pr * ei + pi * er
        return 0

    lax.fori_loop(0, lj, pass2, 0, unroll=4)
    for c in range(seq // chunk):
        rows = slice(c * chunk, (c + 1) * chunk)
        y = jnp.dot(h_ref[rows, :].astype(BF16), wc_ref[...], preferred_element_type=F32)
        g_ref[rows, :] = jax.nn.gelu(y + d_ref[...] * u_ref[rows, :])


def s5_seq(z, prep, sconv_w, sconv_layer, w_cast, layer, *, batch):
    tab, wb, wc, d = prep
    nb, cin, cst = wb.shape
    cast_in, cast_out, cast_shape = _cast_slab_specs(w_cast, layer, nb)
    m_rows = z.shape[0]
    seq = m_rows // batch
    lj = seq // S5_SUBSEQ

    def to_kernel_order(a):
        return a.reshape(batch, S5_SUBSEQ, lj, -1).transpose(0, 2, 1, 3).reshape(m_rows, -1)

    def to_time_order(a):
        return a.reshape(batch, lj, S5_SUBSEQ, -1).transpose(0, 2, 1, 3).reshape(m_rows, -1)

    def cols(k):
        return pl.BlockSpec((seq, cin), lambda j, b: (b, k * nb + j))

    g, st, w_bf16, ob, sc = pl.pallas_call(
        functools.partial(_s5_seq_kernel, chunk=min(seq, ROW_CHUNK)),
        grid=(nb, batch),
        in_specs=[pl.BlockSpec((seq, cin), lambda j, b: (b, j)),
                  pl.BlockSpec((None, cin, cst), lambda j, b: (j, 0, 0)),
                  pl.BlockSpec((None, cst, cin), lambda j, b: (j, 0, 0)),
                  pl.BlockSpec((None, tab.shape[1], cst // 2), lambda j, b: (j, 0, 0)),
                  pl.BlockSpec((None, 1, cin), lambda j, b: (j, 0, 0)), cast_in,
                  cols(1), cols(2), cols(3),
                  pl.BlockSpec((None, CONV_W, cin), lambda j, b: (sconv_layer, 0, j))],
        out_specs=[pl.BlockSpec((seq, cin), lambda j, b: (b, j)),
                   pl.BlockSpec((None, None, 8, cst), lambda j, b: (j, b, 0, 0)), cast_out,
                   pl.BlockSpec((seq, cin), lambda j, b: (b, j)),
                   pl.BlockSpec((None, 8, cin), lambda j, b: (b, 0, j))],
        out_shape=[jax.ShapeDtypeStruct((m_rows, nb * cin), F32),
                   jax.ShapeDtypeStruct((nb, batch, 8, cst), F32), cast_shape,
                   jax.ShapeDtypeStruct((m_rows, nb * cin), BF16),
                   jax.ShapeDtypeStruct((batch, 8, nb * cin), F32)],
        scratch_shapes=[pltpu.VMEM((seq, cst), F32), pltpu.VMEM((lj, cst), F32), pltpu.VMEM((8, cin), F32)],
        compiler_params=_params("parallel", "arbitrary"),
        name="s5_seq",
    )(to_kernel_order(z[:, :nb * cin]), wb, wc, tab, d, w_cast, z, z, z, sconv_w)
    return to_time_order(g), st, w_bf16, ob, sc


def _s5_step_kernel(u_ref, s_ref, wb_ref, wc_ref, tab_ref, d_ref, g_ref, st_ref):
    half = s_ref.shape[1] // 2
    u = u_ref[...]
    bu = jnp.dot(u.astype(BF16), wb_ref[...], preferred_element_type=F32)
    ar, ai = tab_ref[_T_P:_T_P + 1, :], tab_ref[_T_P + 8:_T_P + 9, :]
    sr, si = s_ref[:, 0:half], s_ref[:, half:2 * half]
    hr = ar * sr - ai * si + bu[:, 0:half]
    hi = ar * si + ai * sr + bu[:, half:2 * half]
    st_ref[:, 0:half] = hr
    st_ref[:, half:2 * half] = hi
    h = jnp.concatenate([hr, hi], axis=1)
    y = jnp.dot(h.astype(BF16), wc_ref[...], preferred_element_type=F32) + d_ref[...] * u
    g_ref[...] = jax.nn.gelu(y)


def s5_step(z, state, prep):
    tab, wb, wc, d = prep
    nb, cin, cst = wb.shape
    rows = z.shape[0]
    return pl.pallas_call(
        _s5_step_kernel,
        grid=(nb,),
        in_specs=[pl.BlockSpec((rows, cin), lambda j: (0, j)),
                  pl.BlockSpec((None, rows, cst), lambda j: (j, 0, 0)),
                  pl.BlockSpec((None, cin, cst), lambda j: (j, 0, 0)),
                  pl.BlockSpec((None, cst, cin), lambda j: (j, 0, 0)),
                  pl.BlockSpec((None, tab.shape[1], cst // 2), lambda j: (j, 0, 0)),
                  pl.BlockSpec((None, 1, cin), lambda j: (j, 0, 0))],
        out_specs=[pl.BlockSpec((rows, cin), lambda j: (0, j)),
                   pl.BlockSpec((None, rows, cst), lambda j: (j, 0, 0))],
        out_shape=[jax.ShapeDtypeStruct((rows, nb * cin), F32),
                   jax.ShapeDtypeStruct((nb, rows, cst), F32)],
        compiler_params=_params("parallel"),
        name="s5_step",
    )(z, state, wb, wc, tab, d)


def _state_to_blocks(s_re, s_im):
    b, g, p = s_re.shape
    nb = g // GROUPS_PER_BLOCK

    def f(s):
        return s.reshape(b, nb, GROUPS_PER_BLOCK * p).transpose(1, 0, 2)

    return jnp.concatenate([f(s_re), f(s_im)], axis=2)


def _blocks_to_state(st, p=SSM_STATE):
    nb, b, c = st.shape
    half = c // 2

    def f(s):
        return s.transpose(1, 0, 2).reshape(b, nb * half // p, p)

    return f(st[:, :, :half]), f(st[:, :, half:])


def _attn_seq_kernel(q_ref, kp_ref, kc_ref, vp_ref, vc_ref, sk_ref, wsrc_ref, o_ref, wdst_ref):
    _cast_slab(wsrc_ref, wdst_ref)
    hd, blk = HEAD_DIM, WINDOW
    n = pl.program_id(0)
    shape = (KV_GROUP * blk, 2 * blk)
    qi = lax.broadcasted_iota(jnp.int32, shape, 0) % blk
    kj = lax.broadcasted_iota(jnp.int32, shape, 1)
    vis = (kj >= qi) & (kj <= qi + WINDOW) & ((kj >= blk) | (n > 0))
    for h in range(N_KV_HEADS):
        c0 = h * KV_GROUP * hd
        qa = jnp.concatenate([q_ref[:, c0 + g * hd:c0 + (g + 1) * hd] for g in range(KV_GROUP)], axis=0)
        kb = jnp.concatenate([kp_ref[:, h * hd:(h + 1) * hd], kc_ref[:, h * hd:(h + 1) * hd]], axis=0)
        vb = jnp.concatenate([vp_ref[:, h * hd:(h + 1) * hd], vc_ref[:, h * hd:(h + 1) * hd]], axis=0)
        s = lax.dot_general(qa.astype(BF16), kb.astype(BF16), (((1,), (1,)), ((), ())),
                            preferred_element_type=F32) * (hd ** -0.5)
        s = jnp.where(vis, s, NEG_INF)
        sk = sk_ref[h]
        m = jnp.maximum(jnp.max(s, axis=-1, keepdims=True), sk)
        p = jnp.exp(s - m)
        w = p * (1.0 / (jnp.sum(p, axis=-1, keepdims=True) + jnp.exp(sk - m)))
        o = jnp.dot(w.astype(BF16), vb.astype(BF16), preferred_element_type=F32)
        for g in range(KV_GROUP):
            o_ref[:, c0 + g * hd:c0 + (g + 1) * hd] = o[g * blk:(g + 1) * blk].astype(o_ref.dtype)


def attn_seq(qkv, sinks, w_cast, layer, *, batch, out_dtype):
    m_rows = qkv.shape[0]
    nblk = m_rows // batch // WINDOW
    nq = N_KV_HEADS * KV_GROUP
    qw, kvw = nq * HEAD_DIM, N_KV_HEADS * HEAD_DIM
    kblk, vblk = qw // kvw, qw // kvw + 1
    sk = jnp.repeat(sinks.astype(F32).reshape(N_KV_HEADS, KV_GROUP), WINDOW, axis=1)[..., None]

    cast_in, cast_out, cast_shape = _cast_slab_specs(w_cast, layer, nblk)

    def cur(c):
        return pl.BlockSpec((WINDOW, kvw), lambda n, b: (b * nblk + n, c))

    def prev(c):
        return pl.BlockSpec((WINDOW, kvw), lambda n, b: (b * nblk + jnp.maximum(n - 1, 0), c))

    return pl.pallas_call(
        _attn_seq_kernel,
        grid=(nblk, batch),
        in_specs=[pl.BlockSpec((WINDOW, qw), lambda n, b: (b * nblk + n, 0)),
                  prev(kblk), cur(kblk), prev(vblk), cur(vblk),
                  pl.BlockSpec((N_KV_HEADS, KV_GROUP * WINDOW, 1), lambda n, b: (0, 0, 0)), cast_in],
        out_specs=[pl.BlockSpec((WINDOW, qw), lambda n, b: (b * nblk + n, 0)), cast_out],
        out_shape=[jax.ShapeDtypeStruct((m_rows, qw), out_dtype), cast_shape],
        compiler_params=_params("parallel", "arbitrary"),
        name="attn_seq",
    )(qkv, qkv, qkv, qkv, qkv, sk, w_cast)


def _attn_step_kernel(q_ref, kn_ref, vn_ref, kc_ref, vc_ref, sk_ref, o_ref):
    scale = HEAD_DIM ** -0.5
    q = q_ref[...].astype(BF16)
    s = lax.dot_general(q, kc_ref[...].astype(BF16), (((1,), (1,)), ((), ())),
                        preferred_element_type=F32) * scale
    own = (lax.broadcasted_iota(jnp.int32, s.shape, 1) % N_KV_HEADS
           == lax.broadcasted_iota(jnp.int32, s.shape, 0) // KV_GROUP)
    s = jnp.where(own, s, NEG_INF)
    sn = jnp.sum(q.astype(F32) * kn_ref[...].astype(BF16).astype(F32), axis=-1, keepdims=True) * scale
    sk = sk_ref[...]
    m = jnp.maximum(jnp.maximum(jnp.max(s, axis=-1, keepdims=True), sn), sk)
    p, pn = jnp.exp(s - m), jnp.exp(sn - m)
    r = 1.0 / (jnp.sum(p, axis=-1, keepdims=True) + pn + jnp.exp(sk - m))
    o = jnp.dot((p * r).astype(BF16), vc_ref[...].astype(BF16), preferred_element_type=F32)
    o = o + (pn * r).astype(BF16).astype(F32) * vn_ref[...].astype(BF16).astype(F32)
    o_ref[...] = o.astype(o_ref.dtype)


def attn_step(qkv, cache_k, cache_v, layer, sinks, *, out_dtype):
    rows = qkv.shape[0]
    nq = N_KV_HEADS * KV_GROUP
    layers, wb = cache_k.shape[0], cache_k.shape[2]
    q3 = qkv[:, :nq * HEAD_DIM].reshape(rows, nq, HEAD_DIM)
    kn = qkv[:, nq * HEAD_DIM:(nq + N_KV_HEADS) * HEAD_DIM].reshape(rows, N_KV_HEADS, HEAD_DIM)
    vn = qkv[:, (nq + N_KV_HEADS) * HEAD_DIM:].reshape(rows, N_KV_HEADS, HEAD_DIM)
    nkeys = wb * N_KV_HEADS
    head_spec = pl.BlockSpec((None, nq, HEAD_DIM), lambda b: (b, 0, 0))
    cache_spec = pl.BlockSpec((None, None, nkeys, HEAD_DIM), lambda b: (layer, b, 0, 0))
    o = pl.pallas_call(
        _attn_step_kernel,
        grid=(rows,),
        in_specs=[head_spec, head_spec, head_spec, cache_spec, cache_spec,
                  pl.BlockSpec((nq, 1), lambda b: (0, 0))],
        out_specs=head_spec,
        out_shape=jax.ShapeDtypeStruct((rows, nq, HEAD_DIM), out_dtype),
        compiler_params=_params("parallel"),
        name="attn_step",
    )(q3, jnp.repeat(kn, KV_GROUP, axis=1), jnp.repeat(vn, KV_GROUP, axis=1),
      cache_k.reshape(layers, rows, nkeys, HEAD_DIM), cache_v.reshape(layers, rows, nkeys, HEAD_DIM),
      sinks.astype(F32).reshape(nq, 1))
    new_k = jnp.concatenate([cache_k[layer, :, 1:], kn[:, None]], axis=1)
    new_v = jnp.concatenate([cache_v[layer, :, 1:], vn[:, None]], axis=1)
    return o.reshape(rows, nq * HEAD_DIM), new_k, new_v


def kernel(x_prompt, x_sample, state_ssm_re, state_ssm_im, state_sconv, cache_k, cache_v, state_ffn_conv, norm_mix_g, norm_ffn_g, norm_final_g, w_in_even, ssm_lambda_re, ssm_lambda_im, ssm_log_dt, ssm_b_re, ssm_b_im, ssm_c_re, ssm_c_im, ssm_d, w_glu, b_glu, sconv_w, w_out_even, w_qkv, b_qkv, attn_sinks, w_o, b_o, w_ffn_gate, w_ffn_up, ffn_conv_w, w_ffn_down):
    bp, lp, d_model = x_prompt.shape
    bs = x_sample.shape[0]
    nq, nkv = N_KV_HEADS * KV_GROUP * HEAD_DIM, N_KV_HEADS * HEAD_DIM
    keep = min(WINDOW, lp)
    xp, xs = x_prompt.reshape(bp * lp, d_model), x_sample.reshape(bs, d_model)
    p_out = [[] for _ in range(6)]
    s_out = [[] for _ in range(6)]
    for l in range(DEPTH):
        j = l // 2
        hp, hs = rmsnorm(xp, norm_mix_g[l], BF16), rmsnorm(xs, norm_mix_g[l], BF16)
        if l % 2 == 0:
            prep = s5_prepare(ssm_lambda_re[j], ssm_lambda_im[j], ssm_log_dt[j], ssm_b_re[j], ssm_b_im[j],
                              ssm_c_re[j], ssm_c_im[j], ssm_d[j])
            zp, zs = matmul_dual(hp, hs, w_in_even, j, **TILES_K_MODEL)
            gp, stp, w_down, bp_out, scp = s5_seq(zp, prep, sconv_w, j, w_ffn_down, l, batch=bp)
            gs, sts = s5_step(zs, _state_to_blocks(state_ssm_re[j], state_ssm_im[j]), prep)
            for out, st in ((p_out, stp[:, :, 0, :]), (s_out, sts)):
                re, im = _blocks_to_state(st)
                out[0].append(re)
                out[1].append(im)
            bs_out, scs = sconv_step(zs, sconv_w, j, state_sconv[j], ncols=sconv_w.shape[2], out_dtype=BF16)
            p_out[2].append(scp[:, 8 - (CONV_W - 1):, :])
            s_out[2].append(scs)
            ap, a_s = matmul_dual(gp, gs, w_glu, j, bias=b_glu, mode="glu", out_dtype=BF16, **TILES_GLU)
            xp, xs = matmul_dual((ap, bp_out), (a_s, bs_out), w_out_even, j, res=(xp, xs), mode="res",
                                 **TILES_K_MODEL)
        else:
            qkvp, qkvs = matmul_dual(hp, hs, w_qkv, j, bias=b_qkv, **TILES_K_MODEL)
            op, w_down = attn_seq(qkvp, attn_sinks[j], w_ffn_down, l, batch=bp, out_dtype=BF16)
            kv = qkvp.reshape(bp, lp, -1)[:, lp - keep:, nq:]
            p_out[3].append(kv[..., :nkv].reshape(bp, keep, N_KV_HEADS, HEAD_DIM))
            p_out[4].append(kv[..., nkv:].reshape(bp, keep, N_KV_HEADS, HEAD_DIM))
            o_s, kk, vv = attn_step(qkvs, cache_k, cache_v, j, attn_sinks[j], out_dtype=BF16)
            s_out[3].append(kk)
            s_out[4].append(vv)
            xp, xs = matmul_dual(op, o_s, w_o, j, bias=b_o, res=(xp, xs), mode="res", **TILES_K_MODEL)
        hp, hs = rmsnorm(xp, norm_ffn_g[l], BF16), rmsnorm(xs, norm_ffn_g[l], BF16)
        ap, fcp, a_s, fcs = ffn_gate_up(hp, hs, w_ffn_gate, w_ffn_up, ffn_conv_w, state_ffn_conv, l, batch=bp,
                                        **TILES_FFN)
        p_out[5].append(fcp[:, 8 - (CONV_W - 1):, :])
        s_out[5].append(fcs)
        xp, xs = matmul_dual(ap, a_s, w_down[None], 0, res=(xp, xs), mode="res", **TILES_K_FF)
    yp, ys = rmsnorm(xp, norm_final_g, F32), rmsnorm(xs, norm_final_g, F32)
    return (yp.reshape(bp, lp, d_model), ys.reshape(bs, 1, d_model), *[jnp.stack(t) for t in p_out],
            *[jnp.stack(t) for t in s_out])
```

```python
import functools

import jax
import jax.numpy as jnp
from jax import lax
from jax.experimental import pallas as pl
from jax.experimental.pallas import tpu as pltpu

F32 = jnp.float32
BF16 = jnp.bfloat16

DEPTH = 4
SSM_GROUP = 16
SSM_STATE = 64
GROUPS_PER_BLOCK = 8
S5_SUBSEQ = 8
CONV_W = 3
HEAD_DIM = 128
N_KV_HEADS = 8
KV_GROUP = 4
WINDOW = 128
RMS_EPS = 1e-5
NEG_INF = -1e30

VMEM_LIMIT_BYTES = 62 * 1024 * 1024

TILES_K_MODEL = dict(tm=512, tn=1024)
TILES_K_FF = dict(tm=512, tn=512)
TILES_GLU = dict(tm=512, tn=2048)
TILES_FFN = dict(tm=2048, tn=256, nchunk=4)
ROW_CHUNK = 512
BF16_SUBLANES = 16


def _params(*sem):
    return pltpu.CompilerParams(dimension_semantics=sem, vmem_limit_bytes=VMEM_LIMIT_BYTES)


def _rmsnorm_kernel(x_ref, g_ref, o_ref):
    x = x_ref[...]
    y = x * lax.rsqrt(jnp.mean(x * x, axis=-1, keepdims=True) + RMS_EPS)
    o_ref[...] = (y * g_ref[...]).astype(o_ref.dtype)


def rmsnorm(x, g, out_dtype):
    m, d = x.shape
    tm = min(m, ROW_CHUNK)
    return pl.pallas_call(
        _rmsnorm_kernel,
        grid=(m // tm,),
        in_specs=[pl.BlockSpec((tm, d), lambda i: (i, 0)), pl.BlockSpec((1, d), lambda i: (0, 0))],
        out_specs=pl.BlockSpec((tm, d), lambda i: (i, 0)),
        out_shape=jax.ShapeDtypeStruct((m, d), out_dtype),
        compiler_params=_params("parallel"),
        name="rmsnorm",
    )(x, g.reshape(1, d))


def _mm_dual_kernel(*refs, nx, has_bias, mode, cast):
    it = iter(refs)
    xp_refs = [next(it) for _ in range(nx)]
    xs_refs = [next(it) for _ in range(nx)]
    w_ref = next(it)
    b_ref = next(it) if has_bias else None
    rp_ref, rs_ref = (next(it), next(it)) if mode == "res" else (None, None)
    op_ref, os_ref = next(it), next(it)
    wbf_ref = next(it) if cast else w_ref

    def product(xs_of):
        acc, k0 = None, 0
        for k, x_ref in enumerate(xp_refs):
            k1 = k0 + x_ref.shape[1]
            part = jnp.dot(xs_of(k), wbf_ref[k0:k1, :], preferred_element_type=F32)
            acc = part if acc is None else acc + part
            k0 = k1
        return acc

    def finish(acc, x_refs, r_ref, o_ref):
        if has_bias:
            acc = acc + b_ref[...]
        if mode == "res":
            acc = r_ref[...] + acc
        elif mode == "glu":
            acc = x_refs[0][...] * jax.nn.sigmoid(acc)
        o_ref[...] = acc.astype(o_ref.dtype)

    tm = op_ref.shape[0]

    @pl.when(pl.program_id(1) == 0)
    def _():
        if cast:
            wbf_ref[...] = w_ref[...].astype(BF16)
        acc = product(lambda k: jnp.concatenate([xp_refs[k][...].astype(BF16), xs_refs[k][...].astype(BF16)],
                                                axis=0))
        finish(acc[:tm], xp_refs, rp_ref, op_ref)
        finish(acc[tm:], xs_refs, rs_ref, os_ref)

    @pl.when(pl.program_id(1) > 0)
    def _():
        finish(product(lambda k: xp_refs[k][...].astype(BF16)), xp_refs, rp_ref, op_ref)


def matmul_dual(xp, xs, w, layer, *, tm, tn, bias=None, res=None, mode="plain", out_dtype=F32):
    xps = xp if isinstance(xp, tuple) else (xp,)
    xss = xs if isinstance(xs, tuple) else (xs,)
    mp, ms = xps[0].shape[0], xss[0].shape[0]
    kdim, n = w.shape[1], w.shape[2]
    tm, tn = min(tm, mp), min(tn, n)
    cast = w.dtype != BF16
    assert mode != "glu" or (len(xps) == 1 and tn == n == kdim)
    in_specs = ([pl.BlockSpec((tm, x.shape[1]), lambda j, i: (i, 0)) for x in xps]
                + [pl.BlockSpec((ms, x.shape[1]), lambda j, i: (0, 0)) for x in xss]
                + [pl.BlockSpec((None, kdim, tn), lambda j, i: (layer, 0, j))])
    args = [*xps, *xss, w]
    if bias is not None:
        in_specs.append(pl.BlockSpec((None, 1, tn), lambda j, i: (layer, 0, j)))
        args.append(bias.reshape(bias.shape[0], 1, n))
    if mode == "res":
        in_specs += [pl.BlockSpec((tm, tn), lambda j, i: (i, j)), pl.BlockSpec((ms, tn), lambda j, i: (0, j))]
        args += list(res)
    return pl.pallas_call(
        functools.partial(_mm_dual_kernel, nx=len(xps), has_bias=bias is not None, mode=mode, cast=cast),
        grid=(n // tn, mp // tm),
        in_specs=in_specs,
        out_specs=[pl.BlockSpec((tm, tn), lambda j, i: (i, j)), pl.BlockSpec((ms, tn), lambda j, i: (0, j))],
        out_shape=[jax.ShapeDtypeStruct((mp, n), out_dtype), jax.ShapeDtypeStruct((ms, n), out_dtype)],
        scratch_shapes=[pltpu.VMEM((kdim, tn), BF16)] if cast else [],
        compiler_params=_params("parallel", "arbitrary"),
        name="matmul_dual_" + mode,
    )(*args)


def _conv3_rows(g, w_ref, carry_ref):
    c1 = carry_ref[7:8, :]
    c2 = carry_ref[6:7, :]
    row = lax.broadcasted_iota(jnp.int32, g.shape, 0)
    g1 = jnp.where(row == 0, c1, pltpu.roll(g, 1, axis=0))
    g2 = jnp.where(row == 0, c2, jnp.where(row == 1, c1, pltpu.roll(g, 2, axis=0)))
    carry_ref[...] = g[g.shape[0] - 8:, :]
    return w_ref[0:1, :] * g2 + w_ref[1:2, :] * g1 + w_ref[2:3, :] * g


def _ffn_gu_kernel(xp_ref, xs_ref, wg_ref, wu_ref, cw_ref, prev_ref, ap_ref, stp_ref, as_ref, sts_ref,
                   wgb_ref, wub_ref, carry_ref, *, tiles_per_seq, nchunk):
    i = pl.program_id(1)

    @pl.when(i == 0)
    def _():
        wgb_ref[...] = wg_ref[...].astype(BF16)
        wub_ref[...] = wu_ref[...].astype(BF16)
        xs = xs_ref[...]
        gs = jnp.dot(xs, wgb_ref[...], preferred_element_type=F32)
        us = jnp.dot(xs, wub_ref[...], preferred_element_type=F32)
        prev = prev_ref[...]
        x2, x1 = prev[:, 0, :], prev[:, 1, :]
        y = cw_ref[0:1, :] * x2 + cw_ref[1:2, :] * x1 + cw_ref[2:3, :] * gs
        as_ref[...] = (jax.nn.silu(y) * us).astype(as_ref.dtype)
        sts_ref[:, 0, :] = x1
        sts_ref[:, 1, :] = gs

    @pl.when(i % tiles_per_seq == 0)
    def _():
        carry_ref[...] = jnp.zeros_like(carry_ref)

    rc = xp_ref.shape[0] // nchunk
    for c in range(nchunk):
        x = xp_ref[c * rc:(c + 1) * rc, :]
        g = jnp.dot(x, wgb_ref[...], preferred_element_type=F32)
        u = jnp.dot(x, wub_ref[...], preferred_element_type=F32)
        y = _conv3_rows(g, cw_ref, carry_ref)
        ap_ref[c * rc:(c + 1) * rc, :] = (jax.nn.silu(y) * u).astype(ap_ref.dtype)

    @pl.when(i % tiles_per_seq == tiles_per_seq - 1)
    def _():
        stp_ref[...] = carry_ref[...]


def ffn_gate_up(xp, xs, wg, wu, cw, prev_s, layer, *, batch, tm, tn, nchunk):
    mp, d = xp.shape
    ms = xs.shape[0]
    f = wg.shape[2]
    tm = min(tm, mp // batch)
    tiles_per_seq = mp // batch // tm
    return pl.pallas_call(
        functools.partial(_ffn_gu_kernel, tiles_per_seq=tiles_per_seq, nchunk=nchunk),
        grid=(f // tn, mp // tm),
        in_specs=[pl.BlockSpec((tm, d), lambda j, i: (i, 0)),
                  pl.BlockSpec((ms, d), lambda j, i: (0, 0)),
                  pl.BlockSpec((None, d, tn), lambda j, i: (layer, 0, j)),
                  pl.BlockSpec((None, d, tn), lambda j, i: (layer, 0, j)),
                  pl.BlockSpec((None, CONV_W, tn), lambda j, i: (layer, 0, j)),
                  pl.BlockSpec((None, ms, CONV_W - 1, tn), lambda j, i: (layer, 0, 0, j))],
        out_specs=[pl.BlockSpec((tm, tn), lambda j, i: (i, j)),
                   pl.BlockSpec((None, 8, tn), lambda j, i: (i // tiles_per_seq, 0, j)),
                   pl.BlockSpec((ms, tn), lambda j, i: (0, j)),
                   pl.BlockSpec((ms, CONV_W - 1, tn), lambda j, i: (0, 0, j))],
        out_shape=[jax.ShapeDtypeStruct((mp, f), BF16),
                   jax.ShapeDtypeStruct((batch, 8, f), F32),
                   jax.ShapeDtypeStruct((ms, f), BF16),
                   jax.ShapeDtypeStruct((ms, CONV_W - 1, f), F32)],
        scratch_shapes=[pltpu.VMEM((d, tn), BF16), pltpu.VMEM((d, tn), BF16), pltpu.VMEM((8, tn), F32)],
        compiler_params=_params("parallel", "arbitrary"),
        name="ffn_gate_up",
    )(xp, xs, wg, wu, cw, prev_s)


def _sconv_step_kernel(v_ref, bg_ref, cg_ref, w_ref, prev_ref, o_ref, st_ref):
    m = cg_ref[...] * v_ref[...]
    prev = prev_ref[...]
    x2, x1 = prev[:, 0, :], prev[:, 1, :]
    y = w_ref[0:1, :] * x2 + w_ref[1:2, :] * x1 + w_ref[2:3, :] * m
    o_ref[...] = (bg_ref[...] * y).astype(o_ref.dtype)
    st_ref[:, 0, :] = x1
    st_ref[:, 1, :] = m


def sconv_step(z, sconv_w, layer, prev, *, ncols, out_dtype, tc=ROW_CHUNK):
    rows = z.shape[0]
    nc = ncols // tc

    def cols(k):
        return pl.BlockSpec((rows, tc), lambda j: (0, k * nc + j))

    return pl.pallas_call(
        _sconv_step_kernel,
        grid=(nc,),
        in_specs=[cols(1), cols(2), cols(3),
                  pl.BlockSpec((None, CONV_W, tc), lambda j: (layer, 0, j)),
                  pl.BlockSpec((rows, CONV_W - 1, tc), lambda j: (0, 0, j))],
        out_specs=[pl.BlockSpec((rows, tc), lambda j: (0, j)),
                   pl.BlockSpec((rows, CONV_W - 1, tc), lambda j: (0, 0, j))],
        out_shape=[jax.ShapeDtypeStruct((rows, ncols), out_dtype),
                   jax.ShapeDtypeStruct((rows, CONV_W - 1, ncols), F32)],
        compiler_params=_params("parallel"),
        name="sconv_step",
    )(z, z, z, sconv_w, prev)


def _cast_slab_specs(w, layer, nslabs):
    rows, cols = w.shape[1:]
    slab = rows // nslabs
    assert slab * nslabs == rows and slab % BF16_SUBLANES == 0
    return (pl.BlockSpec((None, slab, cols), lambda o, i: (layer, o, 0)),
            pl.BlockSpec((slab, cols), lambda o, i: (o, 0)),
            jax.ShapeDtypeStruct((rows, cols), BF16))


def _cast_slab(src_ref, dst_ref):
    @pl.when(pl.program_id(1) == 0)
    def _():
        dst_ref[...] = src_ref[...].astype(BF16)


def _s5_prep_kernel(lr_ref, li_ref, ldt_ref, br_ref, bi_ref, pow_ref, bb_ref):
    lr, li = lr_ref[...], li_ref[...]
    dt = jnp.exp(ldt_ref[...])
    mag = jnp.exp(lr * dt)
    ar, ai = mag * jnp.cos(li * dt), mag * jnp.sin(li * dt)
    den = lr * lr + li * li
    cr = ((ar - 1.0) * lr + ai * li) / den
    ci = (ai * lr - (ar - 1.0) * li) / den
    br, bi = br_ref[...], bi_ref[...]
    bb_ref[:, 0:SSM_GROUP, :] = cr * br - ci * bi
    bb_ref[:, SSM_GROUP:2 * SSM_GROUP, :] = cr * bi + ci * br
    pr, pi = ar, ai
    for k in range(8):
        pow_ref[:, k:k + 1, :] = pr
        pow_ref[:, 8 + k:9 + k, :] = pi
        pr, pi = pr * ar - pi * ai, pr * ai + pi * ar


def s5_prepare(lam_re, lam_im, log_dt, b_re, b_im, c_re, c_im, d_skip):
    g, p = lam_re.shape
    i = SSM_GROUP
    nb, gb = g // GROUPS_PER_BLOCK, GROUPS_PER_BLOCK
    pw, bb = pl.pallas_call(
        _s5_prep_kernel,
        out_shape=[jax.ShapeDtypeStruct((g, 16, p), F32), jax.ShapeDtypeStruct((g, 2 * i, p), F32)],
        name="s5_prep",
    )(lam_re.reshape(g, 1, p), lam_im.reshape(g, 1, p), log_dt.reshape(g, 1, 1),
      b_re.transpose(0, 2, 1), b_im.transpose(0, 2, 1))

    def lanes(a):
        return a.reshape(nb, gb, 8, p).transpose(0, 2, 1, 3).reshape(nb, 8, gb * p)

    pr, pi = lanes(pw[:, 0:8]), lanes(pw[:, 8:16])

    def bcast(a, k):
        return jnp.broadcast_to(a[:, k - 1:k, :], a.shape)

    tab = jnp.concatenate([pr, pi, bcast(pr, 1), bcast(pi, 1), bcast(pr, 8), bcast(pi, 8)], axis=1)
    eye = jnp.eye(gb, dtype=F32)

    def wb_part(a):
        a = a.reshape(nb, gb, i, p)
        return jnp.einsum('jaip,ab->jaibp', a, eye).reshape(nb, gb * i, gb * p)

    def wc_part(a):
        a = a.reshape(nb, gb, i, p)
        return jnp.einsum('jaip,ab->jbpai', a, eye).reshape(nb, gb * p, gb * i)

    wb = jnp.concatenate([wb_part(bb[:, :i]), wb_part(bb[:, i:])], axis=2).astype(BF16)
    wc = jnp.concatenate([wc_part(c_re.astype(F32)), wc_part(-c_im.astype(F32))], axis=1).astype(BF16)
    return tab, wb, wc, d_skip.reshape(nb, 1, gb * i)


_T_P, _T_A1, _T_A8 = 0, 16, 32


def _s5_seq_kernel(u_ref, wb_ref, wc_ref, tab_ref, d_ref, wsrc_ref, v_ref, bg_ref, cg_ref, cw_ref,
                   g_ref, st_ref, wdst_ref, ob_ref, sc_ref, h_ref, pw_ref, carry_ref, *, chunk):
    _cast_slab(wsrc_ref, wdst_ref)
    carry_ref[...] = jnp.zeros_like(carry_ref)
    for c in range(v_ref.shape[0] // chunk):
        rows = slice(c * chunk, (c + 1) * chunk)
        y = _conv3_rows(cg_ref[rows, :] * v_ref[rows, :], cw_ref, carry_ref)
        ob_ref[rows, :] = (bg_ref[rows, :] * y).astype(ob_ref.dtype)
    sc_ref[...] = carry_ref[...]
    seq, width = h_ref.shape
    half, lj = width // 2, seq // S5_SUBSEQ
    re, im = slice(0, half), slice(half, width)

    def tab(off):
        return tab_ref[off:off + 8, :], tab_ref[off + 8:off + 16, :]

    @pl.when(pl.program_id(1) == 0)
    def _():
        a8r, a8i = tab(_T_A8)

        def fill(m, c):
            pr, pi = c
            r0 = pl.multiple_of(m * 8, 8)
            pw_ref[pl.ds(r0, 8), re] = pr
            pw_ref[pl.ds(r0, 8), im] = pi
            return pr * a8r - pi * a8i, pr * a8i + pi * a8r

        lax.fori_loop(0, lj // 8, fill, tab(_T_P))

    for c in range(seq // chunk):
        rows = slice(c * chunk, (c + 1) * chunk)
        h_ref[rows, :] = jnp.dot(u_ref[rows, :].astype(BF16), wb_ref[...], preferred_element_type=F32)

    a1r, a1i = tab(_T_A1)

    def pass1(j, c):
        hr, hi = c
        r0 = pl.multiple_of(j * 8, 8)
        hr, hi = (a1r * hr - a1i * hi) + h_ref[pl.ds(r0, 8), re], (a1r * hi + a1i * hr) + h_ref[pl.ds(r0, 8), im]
        h_ref[pl.ds(r0, 8), re] = hr
        h_ref[pl.ds(r0, 8), im] = hi
        return hr, hi

    zero = jnp.zeros((S5_SUBSEQ, half), F32)
    xr, xi = lax.fori_loop(0, lj, pass1, (zero, zero), unroll=4)

    row = lax.broadcasted_iota(jnp.int32, (S5_SUBSEQ, half), 0)
    qr, qi = pw_ref[lj - 1:lj, re], pw_ref[lj - 1:lj, im]
    for k in (1, 2, 4):
        sr = jnp.where(row >= k, pltpu.roll(xr, k, axis=0), 0.0)
        si = jnp.where(row >= k, pltpu.roll(xi, k, axis=0), 0.0)
        xr, xi = xr + (qr * sr - qi * si), xi + (qr * si + qi * sr)
        qr, qi = qr * qr - qi * qi, 2.0 * (qr * qi)
    st_ref[:, re] = jnp.broadcast_to(xr[S5_SUBSEQ - 1:, :], xr.shape)
    st_ref[:, im] = jnp.broadcast_to(xi[S5_SUBSEQ - 1:, :], xi.shape)
    er = jnp.where(row >= 1, pltpu.roll(xr, 1, axis=0), 0.0)
    ei = jnp.where(row >= 1, pltpu.roll(xi, 1, axis=0), 0.0)

    def pass2(j, _):
        r0 = pl.multiple_of(j * 8, 8)
        pr = jnp.broadcast_to(pw_ref[pl.ds(j, 1), re], er.shape)
        pi = jnp.broadcast_to(pw_ref[pl.ds(j, 1), im], er.shape)
        h_ref[pl.ds(r0, 8), re] += pr * er - pi * ei
        h_ref[pl.ds(r0, 8), im] += pr * ei + pi * er
        return 0

    lax.fori_loop(0, lj, pass2, 0, unroll=4)
    for c in range(seq // chunk):
        rows = slice(c * chunk, (c + 1) * chunk)
        y = jnp.dot(h_ref[rows, :].astype(BF16), wc_ref[...], preferred_element_type=F32)
        g_ref[rows, :] = jax.nn.gelu(y + d_ref[...] * u_ref[rows, :])


def s5_seq(z, prep, sconv_w, sconv_layer, w_cast, layer, *, batch):
    tab, wb, wc, d = prep
    nb, cin, cst = wb.shape
    cast_in, cast_out, cast_shape = _cast_slab_specs(w_cast, layer, nb)
    m_rows = z.shape[0]
    seq = m_rows // batch
    lj = seq // S5_SUBSEQ

    def to_kernel_order(a):
        return a.reshape(batch, S5_SUBSEQ, lj, -1).transpose(0, 2, 1, 3).reshape(m_rows, -1)

    def to_time_order(a):
        return a.reshape(batch, lj, S5_SUBSEQ, -1).transpose(0, 2, 1, 3).reshape(m_rows, -1)

    def cols(k):
        return pl.BlockSpec((seq, cin), lambda j, b: (b, k * nb + j))

    g, st, w_bf16, ob, sc = pl.pallas_call(
        functools.partial(_s5_seq_kernel, chunk=min(seq, ROW_CHUNK)),
        grid=(nb, batch),
        in_specs=[pl.BlockSpec((seq, cin), lambda j, b: (b, j)),
                  pl.BlockSpec((None, cin, cst), lambda j, b: (j, 0, 0)),
                  pl.BlockSpec((None, cst, cin), lambda j, b: (j, 0, 0)),
                  pl.BlockSpec((None, tab.shape[1], cst // 2), lambda j, b: (j, 0, 0)),
                  pl.BlockSpec((None, 1, cin), lambda j, b: (j, 0, 0)), cast_in,
                  cols(1), cols(2), cols(3),
                  pl.BlockSpec((None, CONV_W, cin), lambda j, b: (sconv_layer, 0, j))],
        out_specs=[pl.BlockSpec((seq, cin), lambda j, b: (b, j)),
                   pl.BlockSpec((None, None, 8, cst), lambda j, b: (j, b, 0, 0)), cast_out,
                   pl.BlockSpec((seq, cin), lambda j, b: (b, j)),
                   pl.BlockSpec((None, 8, cin), lambda j, b: (b, 0, j))],
        out_shape=[jax.ShapeDtypeStruct((m_rows, nb * cin), F32),
                   jax.ShapeDtypeStruct((nb, batch, 8, cst), F32), cast_shape,
                   jax.ShapeDtypeStruct((m_rows, nb * cin), BF16),
                   jax.ShapeDtypeStruct((batch, 8, nb * cin), F32)],
        scratch_shapes=[pltpu.VMEM((seq, cst), F32), pltpu.VMEM((lj, cst), F32), pltpu.VMEM((8, cin), F32)],
        compiler_params=_params("parallel", "arbitrary"),
        name="s5_seq",
    )(to_kernel_order(z[:, :nb * cin]), wb, wc, tab, d, w_cast, z, z, z, sconv_w)
    return to_time_order(g), st, w_bf16, ob, sc


def _s5_step_kernel(u_ref, s_ref, wb_ref, wc_ref, tab_ref, d_ref, g_ref, st_ref):
    half = s_ref.shape[1] // 2
    u = u_ref[...]
    bu = jnp.dot(u.astype(BF16), wb_ref[...], preferred_element_type=F32)
    ar, ai = tab_ref[_T_P:_T_P + 1, :], tab_ref[_T_P + 8:_T_P + 9, :]
    sr, si = s_ref[:, 0:half], s_ref[:, half:2 * half]
    hr = ar * sr - ai * si + bu[:, 0:half]
    hi = ar * si + ai * sr + bu[:, half:2 * half]
    st_ref[:, 0:half] = hr
    st_ref[:, half:2 * half] = hi
    h = jnp.concatenate([hr, hi], axis=1)
    y = jnp.dot(h.astype(BF16), wc_ref[...], preferred_element_type=F32) + d_ref[...] * u
    g_ref[...] = jax.nn.gelu(y)


def s5_step(z, state, prep):
    tab, wb, wc, d = prep
    nb, cin, cst = wb.shape
    rows = z.shape[0]
    return pl.pallas_call(
        _s5_step_kernel,
        grid=(nb,),
        in_specs=[pl.BlockSpec((rows, cin), lambda j: (0, j)),
                  pl.BlockSpec((None, rows, cst), lambda j: (j, 0, 0)),
                  pl.BlockSpec((None, cin, cst), lambda j: (j, 0, 0)),
                  pl.BlockSpec((None, cst, cin), lambda j: (j, 0, 0)),
                  pl.BlockSpec((None, tab.shape[1], cst // 2), lambda j: (j, 0, 0)),
                  pl.BlockSpec((None, 1, cin), lambda j: (j, 0, 0))],
        out_specs=[pl.BlockSpec((rows, cin), lambda j: (0, j)),
                   pl.BlockSpec((None, rows, cst), lambda j: (j, 0, 0))],
        out_shape=[jax.ShapeDtypeStruct((rows, nb * cin), F32),
                   jax.ShapeDtypeStruct((nb, rows, cst), F32)],
        compiler_params=_params("parallel"),
        name="s5_step",
    )(z, state, wb, wc, tab, d)


def _state_to_blocks(s_re, s_im):
    b, g, p = s_re.shape
    nb = g // GROUPS_PER_BLOCK

    def f(s):
        return s.reshape(b, nb, GROUPS_PER_BLOCK * p).transpose(1, 0, 2)

    return jnp.concatenate([f(s_re), f(s_im)], axis=2)


def _blocks_to_state(st, p=SSM_STATE):
    nb, b, c = st.shape
    half = c // 2

    def f(s):
        return s.transpose(1, 0, 2).reshape(b, nb * half // p, p)

    return f(st[:, :, :half]), f(st[:, :, half:])


def _attn_seq_kernel(q_ref, kp_ref, kc_ref, vp_ref, vc_ref, sk_ref, wsrc_ref, o_ref, wdst_ref):
    _cast_slab(wsrc_ref, wdst_ref)
    hd, blk = HEAD_DIM, WINDOW
    n = pl.program_id(0)
    shape = (KV_GROUP * blk, 2 * blk)
    qi = lax.broadcasted_iota(jnp.int32, shape, 0) % blk
    kj = lax.broadcasted_iota(jnp.int32, shape, 1)
    vis = (kj >= qi) & (kj <= qi + WINDOW) & ((kj >= blk) | (n > 0))
    for h in range(N_KV_HEADS):
        c0 = h * KV_GROUP * hd
        qa = jnp.concatenate([q_ref[:, c0 + g * hd:c0 + (g + 1) * hd] for g in range(KV_GROUP)], axis=0)
        kb = jnp.concatenate([kp_ref[:, h * hd:(h + 1) * hd], kc_ref[:, h * hd:(h + 1) * hd]], axis=0)
        vb = jnp.concatenate([vp_ref[:, h * hd:(h + 1) * hd], vc_ref[:, h * hd:(h + 1) * hd]], axis=0)
        s = lax.dot_general(qa.astype(BF16), kb.astype(BF16), (((1,), (1,)), ((), ())),
                            preferred_element_type=F32) * (hd ** -0.5)
        s = jnp.where(vis, s, NEG_INF)
        sk = sk_ref[h]
        m = jnp.maximum(jnp.max(s, axis=-1, keepdims=True), sk)
        p = jnp.exp(s - m)
        w = p * (1.0 / (jnp.sum(p, axis=-1, keepdims=True) + jnp.exp(sk - m)))
        o = jnp.dot(w.astype(BF16), vb.astype(BF16), preferred_element_type=F32)
        for g in range(KV_GROUP):
            o_ref[:, c0 + g * hd:c0 + (g + 1) * hd] = o[g * blk:(g + 1) * blk].astype(o_ref.dtype)


def attn_seq(qkv, sinks, w_cast, layer, *, batch, out_dtype):
    m_rows = qkv.shape[0]
    nblk = m_rows // batch // WINDOW
    nq = N_KV_HEADS * KV_GROUP
    qw, kvw = nq * HEAD_DIM, N_KV_HEADS * HEAD_DIM
    kblk, vblk = qw // kvw, qw // kvw + 1
    sk = jnp.repeat(sinks.astype(F32).reshape(N_KV_HEADS, KV_GROUP), WINDOW, axis=1)[..., None]

    cast_in, cast_out, cast_shape = _cast_slab_specs(w_cast, layer, nblk)

    def cur(c):
        return pl.BlockSpec((WINDOW, kvw), lambda n, b: (b * nblk + n, c))

    def prev(c):
        return pl.BlockSpec((WINDOW, kvw), lambda n, b: (b * nblk + jnp.maximum(n - 1, 0), c))

    return pl.pallas_call(
        _attn_seq_kernel,
        grid=(nblk, batch),
        in_specs=[pl.BlockSpec((WINDOW, qw), lambda n, b: (b * nblk + n, 0)),
                  prev(kblk), cur(kblk), prev(vblk), cur(vblk),
                  pl.BlockSpec((N_KV_HEADS, KV_GROUP * WINDOW, 1), lambda n, b: (0, 0, 0)), cast_in],
        out_specs=[pl.BlockSpec((WINDOW, qw), lambda n, b: (b * nblk + n, 0)), cast_out],
        out_shape=[jax.ShapeDtypeStruct((m_rows, qw), out_dtype), cast_shape],
        compiler_params=_params("parallel", "arbitrary"),
        name="attn_seq",
    )(qkv, qkv, qkv, qkv, qkv, sk, w_cast)


def _attn_step_kernel(q_ref, kn_ref, vn_ref, kc_ref, vc_ref, sk_ref, o_ref):
    scale = HEAD_DIM ** -0.5
    q = q_ref[...].astype(BF16)
    s = lax.dot_general(q, kc_ref[...].astype(BF16), (((1,), (1,)), ((), ())),
                        preferred_element_type=F32) * scale
    own = (lax.broadcasted_iota(jnp.int32, s.shape, 1) % N_KV_HEADS
           == lax.broadcasted_iota(jnp.int32, s.shape, 0) // KV_GROUP)
    s = jnp.where(own, s, NEG_INF)
    sn = jnp.sum(q.astype(F32) * kn_ref[...].astype(BF16).astype(F32), axis=-1, keepdims=True) * scale
    sk = sk_ref[...]
    m = jnp.maximum(jnp.maximum(jnp.max(s, axis=-1, keepdims=True), sn), sk)
    p, pn = jnp.exp(s - m), jnp.exp(sn - m)
    r = 1.0 / (jnp.sum(p, axis=-1, keepdims=True) + pn + jnp.exp(sk - m))
    o = jnp.dot((p * r).astype(BF16), vc_ref[...].astype(BF16), preferred_element_type=F32)
    o = o + (pn * r).astype(BF16).astype(F32) * vn_ref[...].astype(BF16).astype(F32)
    o_ref[...] = o.astype(o_ref.dtype)


def attn_step(qkv, cache_k, cache_v, layer, sinks, *, out_dtype):
    rows = qkv.shape[0]
    nq = N_KV_HEADS * KV_GROUP
    layers, wb = cache_k.shape[0], cache_k.shape[2]
    q3 = qkv[:, :nq * HEAD_DIM].reshape(rows, nq, HEAD_DIM)
    kn = qkv[:, nq * HEAD_DIM:(nq + N_KV_HEADS) * HEAD_DIM].reshape(rows, N_KV_HEADS, HEAD_DIM)
    vn = qkv[:, (nq + N_KV_HEADS) * HEAD_DIM:].reshape(rows, N_KV_HEADS, HEAD_DIM)
    nkeys = wb * N_KV_HEADS
    head_spec = pl.BlockSpec((None, nq, HEAD_DIM), lambda b: (b, 0, 0))
    cache_spec = pl.BlockSpec((None, None, nkeys, HEAD_DIM), lambda b: (layer, b, 0, 0))
    o = pl.pallas_call(
        _attn_step_kernel,
        grid=(rows,),
        in_specs=[head_spec, head_spec, head_spec, cache_spec, cache_spec,
                  pl.BlockSpec((nq, 1), lambda b: (0, 0))],
        out_specs=head_spec,
        out_shape=jax.ShapeDtypeStruct((rows, nq, HEAD_DIM), out_dtype),
        compiler_params=_params("parallel"),
        name="attn_step",
    )(q3, jnp.repeat(kn, KV_GROUP, axis=1), jnp.repeat(vn, KV_GROUP, axis=1),
      cache_k.reshape(layers, rows, nkeys, HEAD_DIM), cache_v.reshape(layers, rows, nkeys, HEAD_DIM),
      sinks.astype(F32).reshape(nq, 1))
    new_k = jnp.concatenate([cache_k[layer, :, 1:], kn[:, None]], axis=1)
    new_v = jnp.concatenate([cache_v[layer, :, 1:], vn[:, None]], axis=1)
    return o.reshape(rows, nq * HEAD_DIM), new_k, new_v


def kernel(x_prompt, x_sample, state_ssm_re, state_ssm_im, state_sconv, cache_k, cache_v, state_ffn_conv, norm_mix_g, norm_ffn_g, norm_final_g, w_in_even, ssm_lambda_re, ssm_lambda_im, ssm_log_dt, ssm_b_re, ssm_b_im, ssm_c_re, ssm_c_im, ssm_d, w_glu, b_glu, sconv_w, w_out_even, w_qkv, b_qkv, attn_sinks, w_o, b_o, w_ffn_gate, w_ffn_up, ffn_conv_w, w_ffn_down):
    bp, lp, d_model = x_prompt.shape
    bs = x_sample.shape[0]
    nq, nkv = N_KV_HEADS * KV_GROUP * HEAD_DIM, N_KV_HEADS * HEAD_DIM
    keep = min(WINDOW, lp)
    xp, xs = x_prompt.reshape(bp * lp, d_model), x_sample.reshape(bs, d_model)
    p_out = [[] for _ in range(6)]
    s_out = [[] for _ in range(6)]
    for l in range(DEPTH):
        j = l // 2
        hp, hs = rmsnorm(xp, norm_mix_g[l], BF16), rmsnorm(xs, norm_mix_g[l], BF16)
        if l % 2 == 0:
            prep = s5_prepare(ssm_lambda_re[j], ssm_lambda_im[j], ssm_log_dt[j], ssm_b_re[j], ssm_b_im[j],
                              ssm_c_re[j], ssm_c_im[j], ssm_d[j])
            zp, zs = matmul_dual(hp, hs, w_in_even, j, **TILES_K_MODEL)
            gp, stp, w_down, bp_out, scp = s5_seq(zp, prep, sconv_w, j, w_ffn_down, l, batch=bp)
            gs, sts = s5_step(zs, _state_to_blocks(state_ssm_re[j], state_ssm_im[j]), prep)
            for out, st in ((p_out, stp[:, :, 0, :]), (s_out, sts)):
                re, im = _blocks_to_state(st)
                out[0].append(re)
                out[1].append(im)
            bs_out, scs = sconv_step(zs, sconv_w, j, state_sconv[j], ncols=sconv_w.shape[2], out_dtype=BF16)
            p_out[2].append(scp[:, 8 - (CONV_W - 1):, :])
            s_out[2].append(scs)
            ap, a_s = matmul_dual(gp, gs, w_glu, j, bias=b_glu, mode="glu", out_dtype=BF16, **TILES_GLU)
            xp, xs = matmul_dual((ap, bp_out), (a_s, bs_out), w_out_even, j, res=(xp, xs), mode="res",
                                 **TILES_K_MODEL)
        else:
            qkvp, qkvs = matmul_dual(hp, hs, w_qkv, j, bias=b_qkv, **TILES_K_MODEL)
            op, w_down = attn_seq(qkvp, attn_sinks[j], w_ffn_down, l, batch=bp, out_dtype=BF16)
            kv = qkvp.reshape(bp, lp, -1)[:, lp - keep:, nq:]
            p_out[3].append(kv[..., :nkv].reshape(bp, keep, N_KV_HEADS, HEAD_DIM))
            p_out[4].append(kv[..., nkv:].reshape(bp, keep, N_KV_HEADS, HEAD_DIM))
            o_s, kk, vv = attn_step(qkvs, cache_k, cache_v, j, attn_sinks[j], out_dtype=BF16)
            s_out[3].append(kk)
            s_out[4].append(vv)
            xp, xs = matmul_dual(op, o_s, w_o, j, bias=b_o, res=(xp, xs), mode="res", **TILES_K_MODEL)
        hp, hs = rmsnorm(xp, norm_ffn_g[l], BF16), rmsnorm(xs, norm_ffn_g[l], BF16)
        ap, fcp, a_s, fcs = ffn_gate_up(hp, hs, w_ffn_gate, w_ffn_up, ffn_conv_w, state_ffn_conv, l, batch=bp,
                                        **TILES_FFN)
        p_out[5].append(fcp[:, 8 - (CONV_W - 1):, :])
        s_out[5].append(fcs)
        xp, xs = matmul_dual(ap, a_s, w_down[None], 0, res=(xp, xs), mode="res", **TILES_K_FF)
    yp, ys = rmsnorm(xp, norm_final_g, F32), rmsnorm(xs, norm_final_g, F32)
    return (yp.reshape(bp, lp, d_model), ys.reshape(bs, 1, d_model), *[jnp.stack(t) for t in p_out],
            *[jnp.stack(t) for t in s_out])
```
